```python
import jax, jax.numpy as jnp
from jax import lax
import numpy as np

D_MODEL = 2048
BATCH = 16
SEQ = 2048
DEPTH = 2

HEAD_DIM = 128
A_GROUPS = ((128, 1), (512, 4), (2048, 16))
A_HEADS_PER_GROUP = 4
A_N_GROUPS = len(A_GROUPS)
A_HEADS = A_HEADS_PER_GROUP * A_N_GROUPS
A_WIDTH = A_HEADS * HEAD_DIM
A_OUT = A_HEADS_PER_GROUP * HEAD_DIM
ALIBI_MAX_BIAS = 8.0

B_HEADS = 8
B_NOPE = 128
B_ROPE = 64
B_V = 128
B_QK = B_NOPE + B_ROPE
Q_LORA = 512
KV_LORA = 512
B_OUT = B_HEADS * B_V
ROPE_THETA = 10000.0

N_BRANCH = 2
IN_COLS = 3 * A_WIDTH + Q_LORA + KV_LORA + B_ROPE + N_BRANCH * D_MODEL
QBLOCK = 128

N_GROUPS = 8
EXPERTS_PER_GROUP = 8
N_EXPERTS = N_GROUPS * EXPERTS_PER_GROUP
TOP_K = 2
D_EXPERT = 512
EXPERT_BLOCK = 128

N_MOD = 6
EPS = 1e-6

kernel_name = "hybrid_dilated_mla_hmoe_encoder"


def rms_norm(x, g):
    xf = x.astype(jnp.float32)
    y = xf * lax.rsqrt(jnp.mean(xf * xf, axis=-1, keepdims=True) + EPS)
    return (y * g.astype(jnp.float32)).astype(x.dtype)


def rope(x, pos):
    half = x.shape[-1] // 2
    inv = ROPE_THETA ** (-jnp.arange(half, dtype=jnp.float32) / half)
    ang = pos.astype(jnp.float32)[..., None] * inv
    ang = ang.reshape(ang.shape[:2] + (1,) * (x.ndim - 3) + (half,))
    cos, sin = jnp.cos(ang), jnp.sin(ang)
    xf = x.astype(jnp.float32)
    x1, x2 = xf[..., :half], xf[..., half:]
    return jnp.concatenate([x1 * cos - x2 * sin, x1 * sin + x2 * cos], axis=-1).astype(x.dtype)


def alibi_slopes(n):
    return 2.0 ** (-ALIBI_MAX_BIAS * jnp.arange(1, n + 1, dtype=jnp.float32) / n)


def dilated_group_attention(q, k, v, pos, slopes, dilation, radius):
    bsz, seq, nh, dh = q.shape
    offsets = dilation * jnp.arange(-radius, radius + 1)
    scale = dh ** -0.5
    vf = v.astype(jnp.float32)

    def block(ib):
        t0 = ib * QBLOCK
        t = t0 + jnp.arange(QBLOCK)
        idx = t[:, None] + offsets[None, :]
        valid = (idx >= 0) & (idx < seq)
        idx_c = jnp.clip(idx, 0, seq - 1)
        qb = lax.dynamic_slice_in_dim(q, t0, QBLOCK, axis=1)
        kb = jnp.take(k, idx_c, axis=1)
        vb = jnp.take(vf, idx_c, axis=1)
        s = jnp.einsum('bqhd,bqjhd->bhqj', qb, kb).astype(jnp.float32) * scale
        pq = lax.dynamic_slice_in_dim(pos, t0, QBLOCK, axis=1)
        pk = jnp.take(pos, idx_c, axis=1)
        dist = jnp.abs(pq[:, :, None] - pk).astype(jnp.float32)
        s = s - slopes[None, :, None, None] * dist[:, None]
        s = jnp.where(valid[None, None], s, -1e30)
        mx = jnp.max(s, axis=-1, keepdims=True)
        p = jnp.exp(s - mx)
        den = jnp.sum(p, axis=-1)
        o = jnp.einsum('bhqj,bqjhd->bqhd', p, vb) / jnp.transpose(den, (0, 2, 1))[..., None]
        lse = jnp.transpose(mx[..., 0] + jnp.log(den), (0, 2, 1))
        return o, lse

    o, lse = lax.map(block, jnp.arange(seq // QBLOCK))
    o = jnp.transpose(o, (1, 0, 2, 3, 4)).reshape(bsz, seq, nh, dh)
    lse = jnp.transpose(lse, (1, 0, 2, 3)).reshape(bsz, seq, nh)
    return o, lse


def dense_block_attention(q, k, v, scale):
    bsz, seq, nh, _ = q.shape
    vf = v.astype(jnp.float32)

    def block(ib):
        qb = lax.dynamic_slice_in_dim(q, ib * QBLOCK, QBLOCK, axis=1)
        s = jnp.einsum('bqhd,bkhd->bhqk', qb, k).astype(jnp.float32) * scale
        p = jax.nn.softmax(s, axis=-1)
        return jnp.einsum('bhqk,bkhd->bqhd', p, vf)

    o = lax.map(block, jnp.arange(seq // QBLOCK))
    return jnp.transpose(o, (1, 0, 2, 3, 4)).reshape(bsz, seq, nh, -1)


def hierarchical_route(h, w_grp, b_grp, w_exp, b_exp):
    n = h.shape[0]
    grp_logits = (h @ w_grp + b_grp).astype(jnp.float32)
    p_grp = jax.nn.softmax(grp_logits, axis=-1)
    p_top, g_idx = lax.top_k(p_grp, 1)
    exp_logits = (h @ w_exp + b_exp).astype(jnp.float32).reshape(n, N_GROUPS, EXPERTS_PER_GROUP)
    sel = jnp.take_along_axis(exp_logits, g_idx[:, :, None], axis=1)[:, 0]
    p_in = jax.nn.softmax(sel, axis=-1)
    vals, e_idx = lax.top_k(p_in, TOP_K)
    weights = p_top * vals / jnp.sum(vals, axis=-1, keepdims=True)
    ids = (g_idx * EXPERTS_PER_GROUP + e_idx).astype(jnp.int32)
    return ids, weights


def routed_experts(h, expert_ids, gate_w, w_gu, w_down):
    n_tok, d = h.shape
    m = n_tok * TOP_K
    flat_e = expert_ids.reshape(-1)
    flat_tok = jnp.repeat(jnp.arange(n_tok, dtype=jnp.int32), TOP_K)
    flat_w = gate_w.reshape(-1)
    order = jnp.argsort(flat_e)
    sorted_e = flat_e[order]
    counts = jnp.bincount(flat_e, length=N_EXPERTS)
    starts = jnp.cumsum(counts) - counts
    padded = (counts + EXPERT_BLOCK - 1) // EXPERT_BLOCK * EXPERT_BLOCK
    pends = jnp.cumsum(padded)
    pstarts = pends - padded
    dest = pstarts[sorted_e] + (jnp.arange(m) - starts[sorted_e])
    n_blocks = -(-m // EXPERT_BLOCK) + N_EXPERTS
    n_slots = n_blocks * EXPERT_BLOCK
    slot_tok = jnp.zeros((n_slots,), jnp.int32).at[dest].set(flat_tok[order])
    slot_w = jnp.zeros((n_slots,), h.dtype).at[dest].set(flat_w[order].astype(h.dtype))
    block_e = jnp.minimum(
        jnp.searchsorted(pends, jnp.arange(n_blocks) * EXPERT_BLOCK, side='right'), N_EXPERTS - 1)
    xs = h[slot_tok].reshape(n_blocks, EXPERT_BLOCK, d)

    def expert_block(args):
        xb, e = args
        g, u = jnp.split(xb @ w_gu[e], 2, axis=-1)
        return (jax.nn.silu(g) * u) @ w_down[e]

    ys = lax.map(expert_block, (xs, block_e)).reshape(n_slots, d)
    return jnp.zeros_like(h).at[slot_tok].add(ys * slot_w[:, None])


def setup_inputs(seed: int = 0) -> dict:
    key = jax.random.key(seed)
    ks = jax.random.split(key, 24)
    f32 = jnp.float32
    L, D = DEPTH, D_MODEL

    def nrm(k, shape, fan_in, mult=1.0):
        return (jax.random.normal(k, shape, f32) * (mult * fan_in ** -0.5)).astype(f32)

    x = jax.random.normal(ks[0], (BATCH, SEQ, D), f32)
    c = jax.random.normal(ks[1], (BATCH, D), f32)
    offs = jax.random.randint(ks[2], (BATCH, 1), 0, 1024)
    positions = (jnp.arange(SEQ)[None, :] + offs).astype(jnp.int32)
    return {
        "x": x,
        "c": c,
        "positions": positions,
        "ln1_g": 1.0 + 0.05 * jax.random.normal(ks[3], (L, D), f32),
        "ln2_g": 1.0 + 0.05 * jax.random.normal(ks[4], (L, D), f32),
        "w_ada": nrm(ks[5], (L, D, N_MOD * D), D, 0.1),
        "b_ada": 0.01 * jax.random.normal(ks[6], (L, N_MOD * D), f32),
        "w_in": nrm(ks[7], (L, D, IN_COLS), D),
        "q_norm_g": 1.0 + 0.05 * jax.random.normal(ks[8], (L, Q_LORA), f32),
        "w_uq": nrm(ks[9], (L, Q_LORA, B_HEADS * B_QK), Q_LORA),
        "kv_norm_g": 1.0 + 0.05 * jax.random.normal(ks[10], (L, KV_LORA), f32),
        "w_ukv": nrm(ks[11], (L, KV_LORA, B_HEADS * (B_NOPE + B_V)), KV_LORA),
        "w_a_up": nrm(ks[12], (L, A_OUT, D), A_OUT),
        "w_b_up": nrm(ks[13], (L, B_OUT, D), B_OUT),
        "w_o": nrm(ks[14], (L, D, D), D),
        "w_grp": nrm(ks[15], (L, D, N_GROUPS), D),
        "b_grp": 0.01 * jax.random.normal(ks[16], (L, N_GROUPS), f32),
        "w_exp": nrm(ks[17], (L, D, N_EXPERTS), D),
        "b_exp": 0.01 * jax.random.normal(ks[18], (L, N_EXPERTS), f32),
        "w_gu": nrm(ks[19], (L, N_EXPERTS, D, 2 * D_EXPERT), D),
        "w_down": nrm(ks[20], (L, N_EXPERTS, D_EXPERT, D), D_EXPERT),
        "final_g": 1.0 + 0.05 * jax.random.normal(ks[21], (D,), f32),
    }


def reference(x, c, positions, ln1_g, ln2_g, w_ada, b_ada, w_in, q_norm_g, w_uq,
              kv_norm_g, w_ukv, w_a_up, w_b_up, w_o, w_grp, b_grp, w_exp, b_exp,
              w_gu, w_down, final_g):
    bsz, seq, d = x.shape
    slopes = alibi_slopes(A_HEADS).reshape(A_N_GROUPS, A_HEADS_PER_GROUP)
    split_at = [A_WIDTH, 2 * A_WIDTH, 3 * A_WIDTH, 3 * A_WIDTH + Q_LORA,
                3 * A_WIDTH + Q_LORA + KV_LORA, 3 * A_WIDTH + Q_LORA + KV_LORA + B_ROPE]
    cs = jax.nn.silu(c)
    for l in range(DEPTH):
        mod = (cs @ w_ada[l] + b_ada[l])[:, None, :]
        sh1, sc1, gt1, sh2, sc2, gt2 = jnp.split(mod, N_MOD, axis=-1)

        h = rms_norm(x, ln1_g[l]) * (1.0 + sc1) + sh1
        proj = h @ w_in[l]
        qa, ka, va, cq, ckv, kr, gates = jnp.split(proj, split_at, axis=-1)

        qa = qa.reshape(bsz, seq, A_N_GROUPS, A_HEADS_PER_GROUP, HEAD_DIM)
        ka = ka.reshape(bsz, seq, A_N_GROUPS, A_HEADS_PER_GROUP, HEAD_DIM)
        va = va.reshape(bsz, seq, A_N_GROUPS, A_HEADS_PER_GROUP, HEAD_DIM)
        outs, lses = [], []
        for gi, (win, dil) in enumerate(A_GROUPS):
            o_g, lse_g = dilated_group_attention(qa[:, :, gi], ka[:, :, gi], va[:, :, gi],
                                                 positions, slopes[gi], dil, win // (2 * dil))
            outs.append(o_g)
            lses.append(lse_g)
        w_den = jax.nn.softmax(jnp.stack(lses, axis=0), axis=0)
        o_a = jnp.sum(w_den[..., None] * jnp.stack(outs, axis=0), axis=0)
        ya = o_a.reshape(bsz, seq, A_OUT).astype(x.dtype)

        q = (rms_norm(cq, q_norm_g[l]) @ w_uq[l]).reshape(bsz, seq, B_HEADS, B_QK)
        q = jnp.concatenate([q[..., :B_NOPE], rope(q[..., B_NOPE:], positions)], axis=-1)
        kv = (rms_norm(ckv, kv_norm_g[l]) @ w_ukv[l]).reshape(bsz, seq, B_HEADS, B_NOPE + B_V)
        k_rope = jnp.broadcast_to(rope(kr, positions)[:, :, None, :], (bsz, seq, B_HEADS, B_ROPE))
        k = jnp.concatenate([kv[..., :B_NOPE], k_rope], axis=-1)
        yb = dense_block_attention(q, k, kv[..., B_NOPE:], B_QK ** -0.5)
        yb = yb.reshape(bsz, seq, B_OUT).astype(x.dtype)

        g_a, g_b = jnp.split(jax.nn.sigmoid(gates.astype(jnp.float32)).astype(x.dtype), N_BRANCH, axis=-1)
        merged = g_a * (ya @ w_a_up[l]) + g_b * (yb @ w_b_up[l])
        x = x + (1.0 + gt1) * (merged @ w_o[l])

        h2 = rms_norm(x, ln2_g[l]) * (1.0 + sc2) + sh2
        hf = h2.reshape(-1, d)
        ids, wts = hierarchical_route(hf, w_grp[l], b_grp[l], w_exp[l], b_exp[l])
        y = routed_experts(hf, ids, wts, w_gu[l], w_down[l]).reshape(bsz, seq, d)
        x = x + (1.0 + gt2) * y
    return rms_norm(x, final_g)
```

```python
import functools

import jax
import jax.numpy as jnp
from jax import lax
from jax.experimental import pallas as pl
from jax.experimental.pallas import tpu as pltpu

F32 = jnp.float32
BF16 = jnp.bfloat16
I32 = jnp.int32
HIGHEST = lax.Precision.HIGHEST

HEAD_DIM = 128
A_GROUPS = ((128, 1), (512, 4), (2048, 16))
A_HPG = 4
A_NG = len(A_GROUPS)
A_HEADS = A_HPG * A_NG
A_WIDTH = A_HEADS * HEAD_DIM
A_OUT = A_HPG * HEAD_DIM
ALIBI_MAX_BIAS = 8.0
B_HEADS = 8
B_NOPE = 128
B_ROPE = 64
B_V = 128
B_QK = B_NOPE + B_ROPE
Q_LORA = 512
KV_LORA = 512
B_OUT = B_HEADS * B_V
ROPE_THETA = 10000.0
N_GROUPS = 8
EPG = 8
N_EXPERTS = N_GROUPS * EPG
TOP_K = 2
D_EXPERT = 512
N_MOD = 6
EPS = 1e-6
NEG = -1e30

LANE = 128
VMEM_LIMIT = 56 * 1024 * 1024
QBLK = 128
EXPERT_TILE = 256

COL_GA = 0
COL_GB = 16
COL_QA = 32
COL_KA = 44
COL_VA = 56
COL_CQ = 68
COL_CKV = 72
COL_KR = 76
N_COLBLK = 78
PROJ_COLS = N_COLBLK * LANE


def _cparams(sem):
    return pltpu.CompilerParams(dimension_semantics=sem, vmem_limit_bytes=VMEM_LIMIT)


def _rms(x, g):
    return x * lax.rsqrt(jnp.mean(x * x, axis=-1, keepdims=True) + EPS) * g


def _ada_kernel(c_ref, w_ref, b_ref, o_ref):
    c = c_ref[...]
    cs = c * jax.nn.sigmoid(c)
    o_ref[0] = jnp.dot(cs, w_ref[0], precision=HIGHEST, preferred_element_type=F32) + b_ref[0]


def _ada(c, w_ada, b_ada):
    nl, d, nm = w_ada.shape
    bsz = c.shape[0]
    tn = 1536
    return pl.pallas_call(
        _ada_kernel,
        grid=(nl, nm // tn),
        in_specs=[
            pl.BlockSpec((bsz, d), lambda l, j: (0, 0)),
            pl.BlockSpec((1, d, tn), lambda l, j: (l, 0, j)),
            pl.BlockSpec((1, 1, tn), lambda l, j: (l, 0, j)),
        ],
        out_specs=pl.BlockSpec((1, bsz, tn), lambda l, j: (l, 0, j)),
        out_shape=jax.ShapeDtypeStruct((nl, bsz, nm), F32),
        compiler_params=_cparams(("arbitrary", "arbitrary")),
        name="ada",
    )(c, w_ada, b_ada.reshape(nl, 1, nm))


def _rope_table_kernel(pos_ref, inv_ref, c_ref, s_ref):
    ang = pos_ref[...].astype(F32) * inv_ref[...]
    lane = lax.broadcasted_iota(I32, ang.shape, 1)
    half = B_ROPE // 2
    cs = jnp.cos(ang)
    sn = jnp.sin(ang)
    c_ref[...] = jnp.where(lane < B_ROPE, cs, 0.0)
    s_ref[...] = jnp.where(lane < half, -sn, jnp.where(lane < B_ROPE, sn, 0.0))


def _rope_tables(pos_col):
    n = pos_col.shape[0]
    half = B_ROPE // 2
    inv = ROPE_THETA ** (-jnp.arange(half, dtype=F32) / half)
    inv_row = jnp.concatenate([inv, inv, jnp.zeros((LANE - B_ROPE,), F32)]).reshape(1, LANE)
    tm = 2048
    return pl.pallas_call(
        _rope_table_kernel,
        grid=(n // tm,),
        in_specs=[pl.BlockSpec((tm, 1), lambda i: (i, 0)),
                  pl.BlockSpec((1, LANE), lambda i: (0, 0))],
        out_specs=[pl.BlockSpec((tm, LANE), lambda i: (i, 0)),
                   pl.BlockSpec((tm, LANE), lambda i: (i, 0))],
        out_shape=[jax.ShapeDtypeStruct((n, LANE), F32)] * 2,
        compiler_params=_cparams(("arbitrary",)),
        name="rope_tables",
    )(pos_col, inv_row)


def _norm_proj_kernel(x_ref, g_ref, sc_ref, sh_ref, w_ref, o_ref, h_ref):
    @pl.when(pl.program_id(1) == 0)
    def _():
        h = _rms(x_ref[...], g_ref[...]) * (1.0 + sc_ref[0]) + sh_ref[0]
        h_ref[...] = h.astype(BF16)

    o_ref[...] = jnp.dot(h_ref[...], w_ref[...], preferred_element_type=F32).astype(o_ref.dtype)


def _norm_proj(x2, g, sc, sh, w, seq):
    n, d = x2.shape
    ncol = w.shape[1]
    tm = 1024
    tn = 13 * LANE
    per_b = seq // tm
    return pl.pallas_call(
        _norm_proj_kernel,
        grid=(n // tm, ncol // tn),
        in_specs=[
            pl.BlockSpec((tm, d), lambda i, j: (i, 0)),
            pl.BlockSpec((1, d), lambda i, j: (0, 0)),
            pl.BlockSpec((1, 1, d), lambda i, j: (i // per_b, 0, 0)),
            pl.BlockSpec((1, 1, d), lambda i, j: (i // per_b, 0, 0)),
            pl.BlockSpec((d, tn), lambda i, j: (0, j)),
        ],
        out_specs=pl.BlockSpec((tm, tn), lambda i, j: (i, j)),
        out_shape=jax.ShapeDtypeStruct((n, ncol), BF16),
        scratch_shapes=[pltpu.VMEM((tm, d), BF16)],
        compiler_params=_cparams(("arbitrary", "arbitrary")),
        name="norm_proj",
    )(x2, g, sc, sh, w)


def _mla_q_kernel(cq_ref, g_ref, w_ref, c_ref, s_ref, q_ref):
    h = _rms(cq_ref[...].astype(F32), g_ref[...]).astype(BF16)
    r = jnp.dot(h, w_ref[...], preferred_element_type=F32)
    scale = B_QK ** -0.5
    c = c_ref[...]
    s = s_ref[...]
    nw = B_HEADS * LANE
    for hd in range(B_HEADS):
        nope = r[:, hd * LANE:(hd + 1) * LANE]
        ra = r[:, nw + hd * LANE: nw + (hd + 1) * LANE]
        rb = r[:, 2 * nw + hd * LANE: 2 * nw + (hd + 1) * LANE]
        q_ref[:, 2 * hd * LANE:(2 * hd + 1) * LANE] = (nope * scale).astype(BF16)
        q_ref[:, (2 * hd + 1) * LANE:(2 * hd + 2) * LANE] = ((ra * c + rb * s) * scale).astype(BF16)


def _mla_q(proj, g, w, ctab, stab):
    n = proj.shape[0]
    tm = 512
    return pl.pallas_call(
        _mla_q_kernel,
        grid=(n // tm,),
        in_specs=[
            pl.BlockSpec((tm, Q_LORA), lambda i: (i, COL_CQ * LANE // Q_LORA)),
            pl.BlockSpec((1, Q_LORA), lambda i: (0, 0)),
            pl.BlockSpec(w.shape, lambda i: (0, 0)),
            pl.BlockSpec((tm, LANE), lambda i: (i, 0)),
            pl.BlockSpec((tm, LANE), lambda i: (i, 0)),
        ],
        out_specs=pl.BlockSpec((tm, 2 * B_HEADS * LANE), lambda i: (i, 0)),
        out_shape=jax.ShapeDtypeStruct((n, 2 * B_HEADS * LANE), BF16),
        compiler_params=_cparams(("arbitrary",)),
        name="mla_q",
    )(proj, g, w, ctab, stab)


def _mla_kv_kernel(ckv_ref, kr_ref, g_ref, w_ref, c_ref, s_ref, k_ref, v_ref):
    h = _rms(ckv_ref[...].astype(F32), g_ref[...]).astype(BF16)
    r = jnp.dot(h, w_ref[...], preferred_element_type=F32)
    kr = kr_ref[...].astype(F32)
    krope = (kr[:, :LANE] * c_ref[...] + kr[:, LANE:] * s_ref[...]).astype(BF16)
    nw = B_HEADS * LANE
    for hd in range(B_HEADS):
        k_ref[:, 2 * hd * LANE:(2 * hd + 1) * LANE] = r[:, hd * LANE:(hd + 1) * LANE].astype(BF16)
        k_ref[:, (2 * hd + 1) * LANE:(2 * hd + 2) * LANE] = krope
    v_ref[...] = r[:, nw:].astype(BF16)


def _mla_kv(proj, g, w, ctab, stab):
    n = proj.shape[0]
    tm = 512
    return pl.pallas_call(
        _mla_kv_kernel,
        grid=(n // tm,),
        in_specs=[
            pl.BlockSpec((tm, KV_LORA), lambda i: (i, COL_CKV * LANE // KV_LORA)),
            pl.BlockSpec((tm, 2 * LANE), lambda i: (i, COL_KR // 2)),
            pl.BlockSpec((1, KV_LORA), lambda i: (0, 0)),
            pl.BlockSpec(w.shape, lambda i: (0, 0)),
            pl.BlockSpec((tm, LANE), lambda i: (i, 0)),
            pl.BlockSpec((tm, LANE), lambda i: (i, 0)),
        ],
        out_specs=[pl.BlockSpec((tm, 2 * B_HEADS * LANE), lambda i: (i, 0)),
                   pl.BlockSpec((tm, B_HEADS * LANE), lambda i: (i, 0))],
        out_shape=[jax.ShapeDtypeStruct((n, 2 * B_HEADS * LANE), BF16),
                   jax.ShapeDtypeStruct((n, B_HEADS * LANE), BF16)],
        compiler_params=_cparams(("arbitrary",)),
        name="mla_kv",
    )(proj, proj, g, w, ctab, stab)


def _mla_attn_kernel(q_ref, k_ref, v_ref, o_ref):
    s = lax.dot_general(q_ref[...], k_ref[...], (((1,), (1,)), ((), ())),
                        preferred_element_type=F32)
    m = jnp.max(s, axis=-1, keepdims=True)
    p = jnp.exp(s - m)
    den = jnp.sum(p, axis=-1, keepdims=True)
    o = jnp.dot(p.astype(BF16), v_ref[...], preferred_element_type=F32)
    o_ref[...] = (o / den).astype(o_ref.dtype)


def _mla_attn(q, k, v, bsz, seq):
    n = q.shape[0]
    tq = 512
    nq = seq // tq
    return pl.pallas_call(
        _mla_attn_kernel,
        grid=(bsz, B_HEADS, nq),
        in_specs=[
            pl.BlockSpec((tq, 2 * LANE), lambda b, h, i: (b * nq + i, h)),
            pl.BlockSpec((seq, 2 * LANE), lambda b, h, i: (b, h)),
            pl.BlockSpec((seq, LANE), lambda b, h, i: (b, h)),
        ],
        out_specs=pl.BlockSpec((tq, LANE), lambda b, h, i: (b * nq + i, h)),
        out_shape=jax.ShapeDtypeStruct((n, B_HEADS * LANE), BF16),
        compiler_params=_cparams(("arbitrary", "arbitrary", "arbitrary")),
        name="mla_attn",
    )(q, k, v)


def _dilated_kernel(*refs, seq):
    qkv = refs[:3 * A_NG]
    posc_ref = refs[3 * A_NG]
    posr = refs[3 * A_NG + 1: 3 * A_NG + 1 + A_NG]
    sl_ref = refs[3 * A_NG + 1 + A_NG]
    o_ref = refs[3 * A_NG + 2 + A_NG]
    qf, kf, vf, og, lg = refs[3 * A_NG + 3 + A_NG:]
    scale = HEAD_DIM ** -0.5

    for g, (win, dil) in enumerate(A_GROUPS):
        radius = win // (2 * dil)
        length = seq // dil
        kw = min(2 * QBLK, length)
        qf[...] = qkv[3 * g][0].astype(F32) * scale
        kf[...] = qkv[3 * g + 1][0].astype(F32)
        vf[...] = qkv[3 * g + 2][0].astype(F32)
        slope = sl_ref[0, g:g + 1, 0:1]

        def rows(start, size, dil=dil):
            if dil == 1:
                return pl.ds(start, size)
            return pl.ds(start, size, stride=dil)

        def residue(r, carry, g=g, dil=dil, radius=radius, length=length, kw=kw,
                    slope=slope, rows=rows):
            for ib in range(length // QBLK):
                i0 = ib * QBLK
                ws = min(max(i0 - radius, 0), length - kw)
                q_rows = rows(r + dil * i0, QBLK)
                k_rows = rows(r + dil * ws, kw)
                qb = qf[q_rows, :].astype(BF16)
                kb = kf[k_rows, :].astype(BF16)
                vb = vf[k_rows, :].astype(BF16)
                s = lax.dot_general(qb, kb, (((1,), (1,)), ((), ())),
                                    preferred_element_type=F32)
                pq = posc_ref[0, q_rows, :]
                pk = posr[g][0, pl.ds(r, 1), ws:ws + kw]
                dist = jnp.abs(pq - pk).astype(F32)
                s = s - slope * dist
                ii = i0 + lax.broadcasted_iota(I32, (QBLK, kw), 0)
                jj = ws + lax.broadcasted_iota(I32, (QBLK, kw), 1)
                s = jnp.where(jnp.abs(ii - jj) <= radius, s, NEG)
                m = jnp.max(s, axis=-1, keepdims=True)
                p = jnp.exp(s - m)
                den = jnp.sum(p, axis=-1, keepdims=True)
                o = jnp.dot(p.astype(BF16), vb, preferred_element_type=F32) / den
                og[g, q_rows, :] = o
                lg[g, q_rows, :] = jnp.broadcast_to(m + jnp.log(den), (QBLK, HEAD_DIM))
            return carry

        if dil == 1:
            residue(0, 0)
        else:
            lax.fori_loop(0, dil, residue, 0)

    chunk = 256

    def merge(ci, carry):
        rs = pl.ds(pl.multiple_of(ci * chunk, chunk), chunk)
        l0, l1, l2 = lg[0, rs, :], lg[1, rs, :], lg[2, rs, :]
        m = jnp.maximum(jnp.maximum(l0, l1), l2)
        w0, w1, w2 = jnp.exp(l0 - m), jnp.exp(l1 - m), jnp.exp(l2 - m)
        o = (w0 * og[0, rs, :] + w1 * og[1, rs, :] + w2 * og[2, rs, :]) / (w0 + w1 + w2)
        o_ref[0, rs, :] = o.astype(o_ref.dtype)
        return carry

    lax.fori_loop(0, seq // chunk, merge, 0)


def _dilated_attn(proj3, posc, posr, slopes):
    bsz, seq, _ = proj3.shape
    in_specs = []
    args = []
    for g in range(A_NG):
        for base in (COL_QA, COL_KA, COL_VA):
            in_specs.append(pl.BlockSpec(
                (1, seq, LANE), lambda b, h, base=base, g=g: (b, 0, base + A_HPG * g + h)))
            args.append(proj3)
    in_specs.append(pl.BlockSpec((1, seq, 1), lambda b, h: (b, 0, 0)))
    args.append(posc)
    for g, (_, dil) in enumerate(A_GROUPS):
        in_specs.append(pl.BlockSpec((1, dil, seq // dil), lambda b, h: (b, 0, 0)))
        args.append(posr[g])
    in_specs.append(pl.BlockSpec((1, 8, LANE), lambda b, h: (h, 0, 0)))
    args.append(slopes)
    return pl.pallas_call(
        functools.partial(_dilated_kernel, seq=seq),
        grid=(bsz, A_HPG),
        in_specs=in_specs,
        out_specs=pl.BlockSpec((1, seq, LANE), lambda b, h: (b, 0, h)),
        out_shape=jax.ShapeDtypeStruct((bsz, seq, A_OUT), BF16),
        scratch_shapes=[pltpu.VMEM((seq, HEAD_DIM), F32)] * 3
        + [pltpu.VMEM((A_NG, seq, HEAD_DIM), F32)] * 2,
        compiler_params=_cparams(("arbitrary", "arbitrary")),
        name="dilated_attn",
    )(*args)


def _merge_out_kernel(ya_ref, yb_ref, ga_ref, gb_ref, x_ref, wa_ref, wb_ref, wo_ref, gt_ref,
                      g2_ref, sc_ref, sh_ref, wr_ref, br_ref, xo_ref, h_ref, lg_ref):
    a = jnp.dot(ya_ref[...], wa_ref[...], preferred_element_type=F32)
    b = jnp.dot(yb_ref[...], wb_ref[...], preferred_element_type=F32)
    ga = jax.nn.sigmoid(ga_ref[...].astype(F32))
    gb = jax.nn.sigmoid(gb_ref[...].astype(F32))
    merged = (ga * a + gb * b).astype(BF16)
    xn = x_ref[...] + (1.0 + gt_ref[0]) * jnp.dot(merged, wo_ref[...], preferred_element_type=F32)
    xo_ref[...] = xn
    h = _rms(xn, g2_ref[...]) * (1.0 + sc_ref[0]) + sh_ref[0]
    h_ref[...] = h
    lg_ref[...] = jnp.dot(h, wr_ref[...], precision=HIGHEST,
                          preferred_element_type=F32) + br_ref[...]


def _merge_out(ya, yb, proj, x2, wa, wb, wo, gt, g2, sc, sh, wr, br, seq):
    n, d = x2.shape
    tm = 256
    per_b = seq // tm
    const = lambda i: (0, 0)
    bidx = lambda i: (i // per_b, 0, 0)
    return pl.pallas_call(
        _merge_out_kernel,
        grid=(n // tm,),
        in_specs=[
            pl.BlockSpec((tm, A_OUT), lambda i: (i, 0)),
            pl.BlockSpec((tm, B_OUT), lambda i: (i, 0)),
            pl.BlockSpec((tm, d), lambda i: (i, COL_GA * LANE // d)),
            pl.BlockSpec((tm, d), lambda i: (i, COL_GB * LANE // d)),
            pl.BlockSpec((tm, d), lambda i: (i, 0)),
            pl.BlockSpec(wa.shape, const),
            pl.BlockSpec(wb.shape, const),
            pl.BlockSpec(wo.shape, const),
            pl.BlockSpec((1, 1, d), bidx),
            pl.BlockSpec((1, d), const),
            pl.BlockSpec((1, 1, d), bidx),
            pl.BlockSpec((1, 1, d), bidx),
            pl.BlockSpec(wr.shape, const),
            pl.BlockSpec((1, LANE), const),
        ],
        out_specs=[pl.BlockSpec((tm, d), lambda i: (i, 0)),
                   pl.BlockSpec((tm, d), lambda i: (i, 0)),
                   pl.BlockSpec((tm, LANE), lambda i: (i, 0))],
        out_shape=[jax.ShapeDtypeStruct((n, d), F32),
                   jax.ShapeDtypeStruct((n, d), F32),
                   jax.ShapeDtypeStruct((n, LANE), F32)],
        compiler_params=_cparams(("arbitrary",)),
        name="merge_out",
    )(ya, yb, proj, proj, x2, wa, wb, wo, gt, g2, sc, sh, wr, br)


def _route_kernel(lg_ref, ri_ref, rw_ref, cnt_ref, carry_ref):
    @pl.when(pl.program_id(0) == 0)
    def _():
        carry_ref[...] = jnp.zeros_like(carry_ref)

    lg = lg_ref[...]
    tm = lg.shape[0]
    lane = lax.broadcasted_iota(I32, lg.shape, 1).astype(F32)
    big = float(2 * LANE)
    is_grp = lane < N_GROUPS
    gl = jnp.where(is_grp, lg, NEG)
    gm = jnp.max(gl, axis=-1, keepdims=True)
    ge = jnp.where(is_grp, jnp.exp(gl - gm), 0.0)
    pg = ge / jnp.sum(ge, axis=-1, keepdims=True)
    p_top = jnp.max(pg, axis=-1, keepdims=True)
    g_idx = jnp.min(jnp.where(is_grp & (pg == p_top), lane, big), axis=-1, keepdims=True)

    lo = N_GROUPS + EPG * g_idx
    in_grp = (lane >= lo) & (lane < lo + EPG)
    el = jnp.where(in_grp, lg, NEG)
    em = jnp.max(el, axis=-1, keepdims=True)
    ee = jnp.where(in_grp, jnp.exp(el - em), 0.0)
    pin = ee / jnp.sum(ee, axis=-1, keepdims=True)
    v1 = jnp.max(pin, axis=-1, keepdims=True)
    i1 = jnp.min(jnp.where(in_grp & (pin == v1), lane, big), axis=-1, keepdims=True)
    rest = in_grp & (lane != i1)
    p2 = jnp.where(rest, pin, -1.0)
    v2 = jnp.max(p2, axis=-1, keepdims=True)
    i2 = jnp.min(jnp.where(rest & (p2 == v2), lane, big), axis=-1, keepdims=True)
    vs = v1 + v2
    w1 = p_top * v1 / vs
    w2 = p_top * v2 / vs

    oh1 = lane == i1
    oh2 = lane == i2
    mm = (oh1 | oh2).astype(BF16)
    tri = (lax.broadcasted_iota(I32, (tm, tm), 0) > lax.broadcasted_iota(I32, (tm, tm), 1))
    cnt = jnp.dot(tri.astype(BF16), mm, preferred_element_type=F32) + carry_ref[...]
    r1 = jnp.sum(jnp.where(oh1, cnt, 0.0), axis=-1, keepdims=True)
    r2 = jnp.sum(jnp.where(oh2, cnt, 0.0), axis=-1, keepdims=True)
    carry_ref[...] += jnp.sum(mm.astype(F32), axis=0, keepdims=True)
    cnt_ref[...] = carry_ref[...]

    e1 = i1 - N_GROUPS
    e2 = i2 - N_GROUPS
    ri = jnp.where(lane == 0, e1, jnp.where(lane == 1, e2, jnp.where(lane == 2, r1, r2)))
    ri_ref[...] = ri.astype(I32)
    rw_ref[...] = jnp.where(lane == 0, w1, w2)


def _route(logits):
    n = logits.shape[0]
    tm = 512
    return pl.pallas_call(
        _route_kernel,
        grid=(n // tm,),
        in_specs=[pl.BlockSpec((tm, LANE), lambda i: (i, 0))],
        out_specs=[pl.BlockSpec((tm, LANE), lambda i: (i, 0)),
                   pl.BlockSpec((tm, LANE), lambda i: (i, 0)),
                   pl.BlockSpec((1, LANE), lambda i: (0, 0))],
        out_shape=[jax.ShapeDtypeStruct((n, LANE), I32),
                   jax.ShapeDtypeStruct((n, LANE), F32),
                   jax.ShapeDtypeStruct((1, LANE), F32)],
        scratch_shapes=[pltpu.VMEM((1, LANE), F32)],
        compiler_params=_cparams(("arbitrary",)),
        name="route",
    )(logits)


def _dispatch_kernel(dest_ref, h_ref, xs_in_ref, xs_ref, sem):
    del xs_in_ref
    tm = h_ref.shape[0]

    def copy(j, k):
        return pltpu.make_async_copy(h_ref.at[pl.ds(j, 1)],
                                     xs_ref.at[pl.ds(dest_ref[0, 0, 2 * j + k], 1)], sem)

    def start(j, c):
        copy(j, 0).start()
        copy(j, 1).start()
        return c

    def wait(j, c):
        copy(j, 0).wait()
        copy(j, 1).wait()
        return c

    lax.fori_loop(0, tm, start, 0)
    lax.fori_loop(0, tm, wait, 0)


def _dispatch(dest3, h, xs0):
    n, d = h.shape
    tm = dest3.shape[2] // TOP_K
    return pl.pallas_call(
        _dispatch_kernel,
        grid=(n // tm,),
        in_specs=[pl.BlockSpec((1, 1, TOP_K * tm), lambda i: (i, 0, 0), memory_space=pltpu.SMEM),
                  pl.BlockSpec((tm, d), lambda i: (i, 0)),
                  pl.BlockSpec(memory_space=pl.ANY)],
        out_specs=pl.BlockSpec(memory_space=pl.ANY),
        out_shape=jax.ShapeDtypeStruct(xs0.shape, xs0.dtype),
        scratch_shapes=[pltpu.SemaphoreType.DMA(())],
        input_output_aliases={2: 0},
        compiler_params=_cparams(("arbitrary",)),
        name="dispatch",
    )(dest3, h, xs0)


def _expert_kernel(be_ref, nu_ref, x_ref, wgu_ref, wd_ref, o_ref, wgu_s, wd_s):
    i = pl.program_id(0)

    @pl.when(i < nu_ref[0])
    def _():
        e = be_ref[i]
        prev = be_ref[jnp.maximum(i - 1, 0)]

        @pl.when((i == 0) | (e != prev))
        def _():
            wgu_s[...] = wgu_ref[0].astype(BF16)
            wd_s[...] = wd_ref[0].astype(BF16)

        gu = jnp.dot(x_ref[...].astype(BF16), wgu_s[...], preferred_element_type=F32)
        gate = gu[:, :D_EXPERT]
        up = gu[:, D_EXPERT:]
        act = (gate * jax.nn.sigmoid(gate) * up).astype(BF16)
        o_ref[...] = jnp.dot(act, wd_s[...], preferred_element_type=F32)

    @pl.when(i >= nu_ref[0])
    def _():
        o_ref[...] = jnp.zeros_like(o_ref)


def _experts(block_e, n_used, xs, w_gu, w_down):
    n_slots, d = xs.shape
    nb = n_slots // EXPERT_TILE

    def blk(i, be, nu):
        return jnp.minimum(i, nu[0] - 1)

    grid_spec = pltpu.PrefetchScalarGridSpec(
        num_scalar_prefetch=2,
        grid=(nb,),
        in_specs=[
            pl.BlockSpec((EXPERT_TILE, d), lambda i, be, nu: (blk(i, be, nu), 0)),
            pl.BlockSpec((1, d, 2 * D_EXPERT), lambda i, be, nu: (be[blk(i, be, nu)], 0, 0)),
            pl.BlockSpec((1, D_EXPERT, d), lambda i, be, nu: (be[blk(i, be, nu)], 0, 0)),
        ],
        out_specs=pl.BlockSpec((EXPERT_TILE, d), lambda i, be, nu: (i, 0)),
        scratch_shapes=[pltpu.VMEM((d, 2 * D_EXPERT), BF16), pltpu.VMEM((D_EXPERT, d), BF16)],
    )
    return pl.pallas_call(
        _expert_kernel,
        grid_spec=grid_spec,
        out_shape=jax.ShapeDtypeStruct((n_slots, d), F32),
        compiler_params=_cparams(("arbitrary",)),
        name="experts",
    )(block_e, n_used, xs, w_gu, w_down)


def _combine_kernel(dest_ref, ys_ref, x_ref, rw_ref, gt_ref, fg_ref, o_ref, y0, y1, sem, *, final):
    tm = x_ref.shape[0]

    def copy(j, k):
        dst = y0 if k == 0 else y1
        return pltpu.make_async_copy(ys_ref.at[pl.ds(dest_ref[0, 0, 2 * j + k], 1)],
                                     dst.at[pl.ds(j, 1)], sem)

    def start(j, c):
        copy(j, 0).start()
        copy(j, 1).start()
        return c

    def wait(j, c):
        copy(j, 0).wait()
        copy(j, 1).wait()
        return c

    lax.fori_loop(0, tm, start, 0)
    lax.fori_loop(0, tm, wait, 0)
    rw = rw_ref[...]
    y = rw[:, 0:1] * y0[...] + rw[:, 1:2] * y1[...]
    xn = x_ref[...] + (1.0 + gt_ref[0]) * y
    if final:
        xn = _rms(xn, fg_ref[...])
    o_ref[...] = xn


def _combine(dest3, ys, x2, rw, gt, fg, seq, final):
    n, d = x2.shape
    tm = dest3.shape[2] // TOP_K
    per_b = seq // tm
    return pl.pallas_call(
        functools.partial(_combine_kernel, final=final),
        grid=(n // tm,),
        in_specs=[pl.BlockSpec((1, 1, TOP_K * tm), lambda i: (i, 0, 0), memory_space=pltpu.SMEM),
                  pl.BlockSpec(memory_space=pl.ANY),
                  pl.BlockSpec((tm, d), lambda i: (i, 0)),
                  pl.BlockSpec((tm, LANE), lambda i: (i, 0)),
                  pl.BlockSpec((1, 1, d), lambda i: (i // per_b, 0, 0)),
                  pl.BlockSpec((1, d), lambda i: (0, 0))],
        out_specs=pl.BlockSpec((tm, d), lambda i: (i, 0)),
        out_shape=jax.ShapeDtypeStruct((n, d), F32),
        scratch_shapes=[pltpu.VMEM((tm, d), F32), pltpu.VMEM((tm, d), F32),
                        pltpu.SemaphoreType.DMA(())],
        compiler_params=_cparams(("arbitrary",)),
        name="combine",
    )(dest3, ys, x2, rw, gt, fg)


def _rope_cols(w3):
    half = B_ROPE // 2
    x1, x2 = w3[..., :half], w3[..., half:]
    z = jnp.zeros(w3.shape[:-1] + (LANE - B_ROPE,), w3.dtype)
    a = jnp.concatenate([x1, x2, z], axis=-1)
    b = jnp.concatenate([x2, x1, z], axis=-1)
    return a, b


def _layout_w_in(w):
    d = w.shape[0]
    o = 0
    qa = w[:, o:o + A_WIDTH]; o += A_WIDTH
    ka = w[:, o:o + A_WIDTH]; o += A_WIDTH
    va = w[:, o:o + A_WIDTH]; o += A_WIDTH
    cq = w[:, o:o + Q_LORA]; o += Q_LORA
    ckv = w[:, o:o + KV_LORA]; o += KV_LORA
    kr = w[:, o:o + B_ROPE]; o += B_ROPE
    gates = w[:, o:]
    kra, krb = _rope_cols(kr.reshape(d, 1, B_ROPE))
    return jnp.concatenate([gates, qa, ka, va, cq, ckv, kra.reshape(d, LANE),
                            krb.reshape(d, LANE)], axis=1).astype(BF16)


def _layout_w_uq(w):
    k = w.shape[0]
    w3 = w.reshape(k, B_HEADS, B_QK)
    nope = w3[:, :, :B_NOPE].reshape(k, B_HEADS * LANE)
    a, b = _rope_cols(w3[:, :, B_NOPE:])
    return jnp.concatenate([nope, a.reshape(k, -1), b.reshape(k, -1)], axis=1).astype(BF16)


def _layout_w_ukv(w):
    k = w.shape[0]
    w3 = w.reshape(k, B_HEADS, B_NOPE + B_V)
    return jnp.concatenate([w3[:, :, :B_NOPE].reshape(k, -1),
                            w3[:, :, B_NOPE:].reshape(k, -1)], axis=1).astype(BF16)


def kernel(x, c, positions, ln1_g, ln2_g, w_ada, b_ada, w_in, q_norm_g, w_uq, kv_norm_g, w_ukv,
           w_a_up, w_b_up, w_o, w_grp, b_grp, w_exp, b_exp, w_gu, w_down, final_g):
    bsz, seq, d = x.shape
    depth = w_in.shape[0]
    n = bsz * seq
    n_blocks = n * TOP_K // EXPERT_TILE + N_EXPERTS
    n_slots = n_blocks * EXPERT_TILE
    tile_tok = 256

    mod = _ada(c, w_ada, b_ada).reshape(depth, bsz, N_MOD, 1, d)

    pos_col = positions.reshape(n, 1)
    ctab, stab = _rope_tables(pos_col)
    posc = positions.reshape(bsz, seq, 1)
    posr = [positions.reshape(bsz, seq // dil, dil).transpose(0, 2, 1) for _, dil in A_GROUPS]
    slopes = 2.0 ** (-ALIBI_MAX_BIAS * jnp.arange(1, A_HEADS + 1, dtype=F32) / A_HEADS)
    slopes = slopes.reshape(A_NG, A_HPG).T
    slopes = jnp.pad(slopes, ((0, 0), (0, 8 - A_NG)))
    slopes = jnp.broadcast_to(slopes[:, :, None], (A_HPG, 8, LANE))

    x2 = x.reshape(n, d)
    for l in range(depth):
        sh1, sc1, gt1, sh2, sc2, gt2 = (mod[l, :, m] for m in range(N_MOD))

        proj = _norm_proj(x2, ln1_g[l].reshape(1, d), sc1, sh1, _layout_w_in(w_in[l]), seq)
        q = _mla_q(proj, q_norm_g[l].reshape(1, Q_LORA), _layout_w_uq(w_uq[l]), ctab, stab)
        k, v = _mla_kv(proj, kv_norm_g[l].reshape(1, KV_LORA), _layout_w_ukv(w_ukv[l]), ctab, stab)
        yb = _mla_attn(q, k, v, bsz, seq)
        ya = _dilated_attn(proj.reshape(bsz, seq, PROJ_COLS), posc, posr, slopes).reshape(n, A_OUT)

        w_r = jnp.concatenate([w_grp[l], w_exp[l],
                               jnp.zeros((d, LANE - N_GROUPS - N_EXPERTS), F32)], axis=1)
        b_r = jnp.concatenate([b_grp[l], b_exp[l],
                               jnp.zeros((LANE - N_GROUPS - N_EXPERTS,), F32)]).reshape(1, LANE)
        x2, h2, logits = _merge_out(ya, yb, proj, x2, w_a_up[l].astype(BF16),
                                    w_b_up[l].astype(BF16), w_o[l].astype(BF16), gt1,
                                    ln2_g[l].reshape(1, d), sc2, sh2, w_r, b_r, seq)

        ri, rw, cnt = _route(logits)
        counts = cnt[0, N_GROUPS:N_GROUPS + N_EXPERTS].astype(I32)
        padded = (counts + EXPERT_TILE - 1) // EXPERT_TILE * EXPERT_TILE
        pends = jnp.cumsum(padded)
        pstarts = pends - padded
        dest = pstarts[ri[:, :TOP_K]] + ri[:, TOP_K:2 * TOP_K]
        dest3 = dest.reshape(n // tile_tok, 1, TOP_K * tile_tok)
        n_used = (pends[-1:] // EXPERT_TILE).astype(I32)
        block_e = jnp.minimum(
            jnp.searchsorted(pends, jnp.arange(n_blocks, dtype=I32) * EXPERT_TILE, side="right"),
            N_EXPERTS - 1).astype(I32)

        xs = _dispatch(dest3, h2, jnp.zeros((n_slots, d), F32))
        ys = _experts(block_e, n_used, xs, w_gu[l], w_down[l])
        x2 = _combine(dest3, ys, x2, rw, gt2, final_g.reshape(1, d), seq, final=(l == depth - 1))
    return x2.reshape(bsz, seq, d)
```

```python
import functools

import jax
import jax.numpy as jnp
from jax import lax
from jax.experimental import pallas as pl
from jax.experimental.pallas import tpu as pltpu

F32 = jnp.float32
BF16 = jnp.bfloat16
I32 = jnp.int32
HIGHEST = lax.Precision.HIGHEST

HEAD_DIM = 128
A_GROUPS = ((128, 1), (512, 4), (2048, 16))
A_HPG = 4
A_NG = len(A_GROUPS)
A_HEADS = A_HPG * A_NG
A_WIDTH = A_HEADS * HEAD_DIM
A_OUT = A_HPG * HEAD_DIM
ALIBI_MAX_BIAS = 8.0
B_HEADS = 8
B_NOPE = 128
B_ROPE = 64
B_V = 128
B_QK = B_NOPE + B_ROPE
Q_LORA = 512
KV_LORA = 512
B_OUT = B_HEADS * B_V
ROPE_THETA = 10000.0
N_GROUPS = 8
EPG = 8
N_EXPERTS = N_GROUPS * EPG
TOP_K = 2
D_EXPERT = 512
N_MOD = 6
EPS = 1e-6
NEG = -1e30
LOG2E = 1.4426950408889634

LANE = 128
VMEM_LIMIT = 56 * 1024 * 1024
QBLK = 128
EXPERT_TILE = 256

COL_GA = 0
COL_GB = 16
COL_QA = 32
COL_KA = 44
COL_VA = 56
COL_CQ = 68
COL_CKV = 72
COL_KR = 76
N_COLBLK = 78
PROJ_COLS = N_COLBLK * LANE


def _cparams(sem):
    return pltpu.CompilerParams(dimension_semantics=sem, vmem_limit_bytes=VMEM_LIMIT)


def _rms(x, g):
    return x * lax.rsqrt(jnp.mean(x * x, axis=-1, keepdims=True) + EPS) * g


def _ada_kernel(c_ref, w_ref, b_ref, o_ref):
    c = c_ref[...]
    cs = c * jax.nn.sigmoid(c)
    o_ref[0] = jnp.dot(cs, w_ref[0], precision=HIGHEST, preferred_element_type=F32) + b_ref[0]


def _ada(c, w_ada, b_ada):
    nl, d, nm = w_ada.shape
    bsz = c.shape[0]
    tn = 1536
    return pl.pallas_call(
        _ada_kernel,
        grid=(nl, nm // tn),
        in_specs=[
            pl.BlockSpec((bsz, d), lambda l, j: (0, 0)),
            pl.BlockSpec((1, d, tn), lambda l, j: (l, 0, j)),
            pl.BlockSpec((1, 1, tn), lambda l, j: (l, 0, j)),
        ],
        out_specs=pl.BlockSpec((1, bsz, tn), lambda l, j: (l, 0, j)),
        out_shape=jax.ShapeDtypeStruct((nl, bsz, nm), F32),
        compiler_params=_cparams(("arbitrary", "arbitrary")),
        name="ada",
    )(c, w_ada, b_ada.reshape(nl, 1, nm))


def _rope_table_kernel(pos_ref, inv_ref, c_ref, s_ref):
    ang = pos_ref[...].astype(F32) * inv_ref[...]
    lane = lax.broadcasted_iota(I32, ang.shape, 1)
    half = B_ROPE // 2
    cs = jnp.cos(ang)
    sn = jnp.sin(ang)
    c_ref[...] = jnp.where(lane < B_ROPE, cs, 0.0)
    s_ref[...] = jnp.where(lane < half, -sn, jnp.where(lane < B_ROPE, sn, 0.0))


def _rope_tables(pos_col):
    n = pos_col.shape[0]
    half = B_ROPE // 2
    inv = ROPE_THETA ** (-jnp.arange(half, dtype=F32) / half)
    inv_row = jnp.concatenate([inv, inv, jnp.zeros((LANE - B_ROPE,), F32)]).reshape(1, LANE)
    tm = 2048
    return pl.pallas_call(
        _rope_table_kernel,
        grid=(n // tm,),
        in_specs=[pl.BlockSpec((tm, 1), lambda i: (i, 0)),
                  pl.BlockSpec((1, LANE), lambda i: (0, 0))],
        out_specs=[pl.BlockSpec((tm, LANE), lambda i: (i, 0)),
                   pl.BlockSpec((tm, LANE), lambda i: (i, 0))],
        out_shape=[jax.ShapeDtypeStruct((n, LANE), F32)] * 2,
        compiler_params=_cparams(("arbitrary",)),
        name="rope_tables",
    )(pos_col, inv_row)


def _norm_proj_kernel(x_ref, g_ref, sc_ref, sh_ref, w_ref, o_ref, h_ref):
    @pl.when(pl.program_id(1) == 0)
    def _():
        h = _rms(x_ref[...], g_ref[...]) * (1.0 + sc_ref[0]) + sh_ref[0]
        h_ref[...] = h.astype(BF16)

    o_ref[...] = jnp.dot(h_ref[...], w_ref[...], preferred_element_type=F32).astype(o_ref.dtype)


def _norm_proj(x2, g, sc, sh, w, seq):
    n, d = x2.shape
    ncol = w.shape[1]
    tm = 1024
    tn = 13 * LANE
    per_b = seq // tm
    return pl.pallas_call(
        _norm_proj_kernel,
        grid=(n // tm, ncol // tn),
        in_specs=[
            pl.BlockSpec((tm, d), lambda i, j: (i, 0)),
            pl.BlockSpec((1, d), lambda i, j: (0, 0)),
            pl.BlockSpec((1, 1, d), lambda i, j: (i // per_b, 0, 0)),
            pl.BlockSpec((1, 1, d), lambda i, j: (i // per_b, 0, 0)),
            pl.BlockSpec((d, tn), lambda i, j: (0, j)),
        ],
        out_specs=pl.BlockSpec((tm, tn), lambda i, j: (i, j)),
        out_shape=jax.ShapeDtypeStruct((n, ncol), BF16),
        scratch_shapes=[pltpu.VMEM((tm, d), BF16)],
        compiler_params=_cparams(("arbitrary", "arbitrary")),
        name="norm_proj",
    )(x2, g, sc, sh, w)


def _mla_q_kernel(cq_ref, g_ref, w_ref, c_ref, s_ref, q_ref):
    h = _rms(cq_ref[...].astype(F32), g_ref[...]).astype(BF16)
    r = jnp.dot(h, w_ref[...], preferred_element_type=F32)
    scale = B_QK ** -0.5 * LOG2E
    c = c_ref[...]
    s = s_ref[...]
    nw = B_HEADS * LANE
    for hd in range(B_HEADS):
        nope = r[:, hd * LANE:(hd + 1) * LANE]
        ra = r[:, nw + hd * LANE: nw + (hd + 1) * LANE]
        rb = r[:, 2 * nw + hd * LANE: 2 * nw + (hd + 1) * LANE]
        q_ref[:, 2 * hd * LANE:(2 * hd + 1) * LANE] = (nope * scale).astype(BF16)
        q_ref[:, (2 * hd + 1) * LANE:(2 * hd + 2) * LANE] = ((ra * c + rb * s) * scale).astype(BF16)


def _mla_q(proj, g, w, ctab, stab):
    n = proj.shape[0]
    tm = 512
    return pl.pallas_call(
        _mla_q_kernel,
        grid=(n // tm,),
        in_specs=[
            pl.BlockSpec((tm, Q_LORA), lambda i: (i, COL_CQ * LANE // Q_LORA)),
            pl.BlockSpec((1, Q_LORA), lambda i: (0, 0)),
            pl.BlockSpec(w.shape, lambda i: (0, 0)),
            pl.BlockSpec((tm, LANE), lambda i: (i, 0)),
            pl.BlockSpec((tm, LANE), lambda i: (i, 0)),
        ],
        out_specs=pl.BlockSpec((tm, 2 * B_HEADS * LANE), lambda i: (i, 0)),
        out_shape=jax.ShapeDtypeStruct((n, 2 * B_HEADS * LANE), BF16),
        compiler_params=_cparams(("arbitrary",)),
        name="mla_q",
    )(proj, g, w, ctab, stab)


def _mla_kv_kernel(ckv_ref, kr_ref, g_ref, w_ref, c_ref, s_ref, k_ref, v_ref):
    h = _rms(ckv_ref[...].astype(F32), g_ref[...]).astype(BF16)
    r = jnp.dot(h, w_ref[...], preferred_element_type=F32)
    kr = kr_ref[...].astype(F32)
    krope = (kr[:, :LANE] * c_ref[...] + kr[:, LANE:] * s_ref[...]).astype(BF16)
    nw = B_HEADS * LANE
    for hd in range(B_HEADS):
        k_ref[:, 2 * hd * LANE:(2 * hd + 1) * LANE] = r[:, hd * LANE:(hd + 1) * LANE].astype(BF16)
        k_ref[:, (2 * hd + 1) * LANE:(2 * hd + 2) * LANE] = krope
    v_ref[...] = r[:, nw:].astype(BF16)


def _mla_kv(proj, g, w, ctab, stab):
    n = proj.shape[0]
    tm = 512
    return pl.pallas_call(
        _mla_kv_kernel,
        grid=(n // tm,),
        in_specs=[
            pl.BlockSpec((tm, KV_LORA), lambda i: (i, COL_CKV * LANE // KV_LORA)),
            pl.BlockSpec((tm, 2 * LANE), lambda i: (i, COL_KR // 2)),
            pl.BlockSpec((1, KV_LORA), lambda i: (0, 0)),
            pl.BlockSpec(w.shape, lambda i: (0, 0)),
            pl.BlockSpec((tm, LANE), lambda i: (i, 0)),
            pl.BlockSpec((tm, LANE), lambda i: (i, 0)),
        ],
        out_specs=[pl.BlockSpec((tm, 2 * B_HEADS * LANE), lambda i: (i, 0)),
                   pl.BlockSpec((tm, B_HEADS * LANE), lambda i: (i, 0))],
        out_shape=[jax.ShapeDtypeStruct((n, 2 * B_HEADS * LANE), BF16),
                   jax.ShapeDtypeStruct((n, B_HEADS * LANE), BF16)],
        compiler_params=_cparams(("arbitrary",)),
        name="mla_kv",
    )(proj, proj, g, w, ctab, stab)


MLA_SUB = 512


def _mla_attn_kernel(q_ref, k_ref, v_ref, o_ref, vx_ref):
    @pl.when(pl.program_id(2) == 0)
    def _():
        vx_ref[:, :LANE] = v_ref[...]
        vx_ref[:, LANE:] = jnp.ones((vx_ref.shape[0], LANE), BF16)

    for j in range(q_ref.shape[0] // MLA_SUB):
        rows = slice(j * MLA_SUB, (j + 1) * MLA_SUB)
        s = lax.dot_general(q_ref[rows, :], k_ref[...], (((1,), (1,)), ((), ())),
                            preferred_element_type=F32)
        m = jnp.max(s, axis=-1, keepdims=True)
        p = jnp.exp2(s - m).astype(BF16)
        ox = jnp.dot(p, vx_ref[...], preferred_element_type=F32)
        o_ref[rows, :] = (ox[:, :LANE] / ox[:, LANE:LANE + 1]).astype(o_ref.dtype)


def _mla_attn(q, k, v, bsz, seq):
    n = q.shape[0]
    tq = 2048
    nq = seq // tq
    return pl.pallas_call(
        _mla_attn_kernel,
        grid=(bsz, B_HEADS, nq),
        in_specs=[
            pl.BlockSpec((tq, 2 * LANE), lambda b, h, i: (b * nq + i, h)),
            pl.BlockSpec((seq, 2 * LANE), lambda b, h, i: (b, h)),
            pl.BlockSpec((seq, LANE), lambda b, h, i: (b, h)),
        ],
        out_specs=pl.BlockSpec((tq, LANE), lambda b, h, i: (b * nq + i, h)),
        out_shape=jax.ShapeDtypeStruct((n, B_HEADS * LANE), BF16),
        scratch_shapes=[pltpu.VMEM((seq, 2 * LANE), BF16)],
        compiler_params=_cparams(("arbitrary", "arbitrary", "arbitrary")),
        name="mla_attn",
    )(q, k, v)


def _dilated_kernel(*refs, seq):
    qkv = refs[:3 * A_NG]
    posc_ref = refs[3 * A_NG]
    posr = refs[3 * A_NG + 1: 3 * A_NG + 1 + A_NG]
    sl_ref = refs[3 * A_NG + 1 + A_NG]
    o_ref = refs[3 * A_NG + 2 + A_NG]
    qf, kf, vf, og, lg = refs[3 * A_NG + 3 + A_NG:]
    scale = HEAD_DIM ** -0.5

    for g, (win, dil) in enumerate(A_GROUPS):
        radius = win // (2 * dil)
        length = seq // dil
        kw = min(2 * QBLK, length)
        q_in, k_in, v_in = qkv[3 * g], qkv[3 * g + 1], qkv[3 * g + 2]
        slope2 = sl_ref[0, g:g + 1, 0:1] * LOG2E
        if dil > 1:
            qf[...] = q_in[0].astype(F32)
            kf[...] = k_in[0].astype(F32)
            vf[...] = v_in[0].astype(F32)

        def rows(start, size, dil=dil):
            if dil == 1:
                return pl.ds(start, size)
            return pl.ds(start, size, stride=dil)

        def load(ref_in, ref_f32, rws, dil=dil):
            if dil == 1:
                return ref_in[0, rws, :]
            return ref_f32[rws, :].astype(BF16)

        units = []
        for r in range(dil):
            for ib in range(length // QBLK):
                i0 = ib * QBLK
                ws = min(max(i0 - radius, 0), length - kw)
                units.append((r, i0, ws, rows(r + dil * i0, QBLK), rows(r + dil * ws, kw)))
        qs = [(load(q_in, qf, u[3]).astype(F32) * (scale * LOG2E)).astype(BF16) for u in units]
        ks = [load(k_in, kf, u[4]) for u in units]
        ss = [lax.dot_general(q, k, (((1,), (1,)), ((), ())), preferred_element_type=F32)
              for q, k in zip(qs, ks)]
        biased = []
        for (r, i0, ws, q_rows, _), s in zip(units, ss):
            pq = posc_ref[0, q_rows, :]
            pk = posr[g][0, r:r + 1, ws:ws + kw]
            dist = jnp.abs(pq - pk).astype(F32)
            ii = i0 + lax.broadcasted_iota(I32, (QBLK, kw), 0)
            jj = ws + lax.broadcasted_iota(I32, (QBLK, kw), 1)
            biased.append(jnp.where(jnp.abs(ii - jj) <= radius, s - slope2 * dist, NEG))
        ms = [jnp.max(s, axis=-1, keepdims=True) for s in biased]
        ps = [jnp.exp2(s - m) for s, m in zip(biased, ms)]
        dens = [jnp.sum(p, axis=-1, keepdims=True) for p in ps]
        vs = [load(v_in, vf, u[4]) for u in units]
        outs = [jnp.dot(p.astype(BF16), v, preferred_element_type=F32) / den
                for p, v, den in zip(ps, vs, dens)]
        for u, o, m, den in zip(units, outs, ms, dens):
            og[g, u[3], :] = o
            lg[g, u[3], :] = jnp.broadcast_to(m * (1.0 / LOG2E) + jnp.log(den), (QBLK, HEAD_DIM))

    chunk = 256

    def merge(ci, carry):
        rs = pl.ds(pl.multiple_of(ci * chunk, chunk), chunk)
        l0, l1, l2 = lg[0, rs, :], lg[1, rs, :], lg[2, rs, :]
        m = jnp.maximum(jnp.maximum(l0, l1), l2)
        w0, w1, w2 = jnp.exp(l0 - m), jnp.exp(l1 - m), jnp.exp(l2 - m)
        o = (w0 * og[0, rs, :] + w1 * og[1, rs, :] + w2 * og[2, rs, :]) / (w0 + w1 + w2)
        o_ref[0, rs, :] = o.astype(o_ref.dtype)
        return carry

    lax.fori_loop(0, seq // chunk, merge, 0)


def _dilated_attn(proj3, posc, posr, slopes):
    bsz, seq, _ = proj3.shape
    in_specs = []
    args = []
    for g in range(A_NG):
        for base in (COL_QA, COL_KA, COL_VA):
            in_specs.append(pl.BlockSpec(
                (1, seq, LANE), lambda b, h, base=base, g=g: (b, 0, base + A_HPG * g + h)))
            args.append(proj3)
    in_specs.append(pl.BlockSpec((1, seq, 1), lambda b, h: (b, 0, 0)))
    args.append(posc)
    for g, (_, dil) in enumerate(A_GROUPS):
        in_specs.append(pl.BlockSpec((1, dil, seq // dil), lambda b, h: (b, 0, 0)))
        args.append(posr[g])
    in_specs.append(pl.BlockSpec((1, 8, LANE), lambda b, h: (h, 0, 0)))
    args.append(slopes)
    return pl.pallas_call(
        functools.partial(_dilated_kernel, seq=seq),
        grid=(bsz, A_HPG),
        in_specs=in_specs,
        out_specs=pl.BlockSpec((1, seq, LANE), lambda b, h: (b, 0, h)),
        out_shape=jax.ShapeDtypeStruct((bsz, seq, A_OUT), BF16),
        scratch_shapes=[pltpu.VMEM((seq, HEAD_DIM), F32)] * 3
        + [pltpu.VMEM((A_NG, seq, HEAD_DIM), F32)] * 2,
        compiler_params=_cparams(("arbitrary", "arbitrary")),
        name="dilated_attn",
    )(*args)


def _merge_out_kernel(ya_ref, yb_ref, ga_ref, gb_ref, x_ref, wa_ref, wb_ref, wo_ref, gt_ref,
                      g2_ref, sc_ref, sh_ref, wr_ref, br_ref, xo_ref, h_ref, lg_ref):
    a = jnp.dot(ya_ref[...], wa_ref[...], preferred_element_type=F32)
    b = jnp.dot(yb_ref[...], wb_ref[...], preferred_element_type=F32)
    ga = jax.nn.sigmoid(ga_ref[...].astype(F32))
    gb = jax.nn.sigmoid(gb_ref[...].astype(F32))
    merged = (ga * a + gb * b).astype(BF16)
    xn = x_ref[...] + (1.0 + gt_ref[0]) * jnp.dot(merged, wo_ref[...], preferred_element_type=F32)
    xo_ref[...] = xn
    h = _rms(xn, g2_ref[...]) * (1.0 + sc_ref[0]) + sh_ref[0]
    h_ref[...] = h
    h_hi = h.astype(BF16)
    h_lo = (h - h_hi.astype(F32)).astype(BF16)
    t = jnp.dot(h_hi, wr_ref[...], preferred_element_type=F32)
    u = jnp.dot(h_lo, wr_ref[:, :LANE], preferred_element_type=F32)
    lg_ref[...] = t[:, :LANE] + t[:, LANE:] + u + br_ref[...]


def _merge_out(ya, yb, proj, x2, wa, wb, wo, gt, g2, sc, sh, wr, br, seq):
    n, d = x2.shape
    tm = 256
    per_b = seq // tm
    const = lambda i: (0, 0)
    bidx = lambda i: (i // per_b, 0, 0)
    return pl.pallas_call(
        _merge_out_kernel,
        grid=(n // tm,),
        in_specs=[
            pl.BlockSpec((tm, A_OUT), lambda i: (i, 0)),
            pl.BlockSpec((tm, B_OUT), lambda i: (i, 0)),
            pl.BlockSpec((tm, d), lambda i: (i, COL_GA * LANE // d)),
            pl.BlockSpec((tm, d), lambda i: (i, COL_GB * LANE // d)),
            pl.BlockSpec((tm, d), lambda i: (i, 0)),
            pl.BlockSpec(wa.shape, const),
            pl.BlockSpec(wb.shape, const),
            pl.BlockSpec(wo.shape, const),
            pl.BlockSpec((1, 1, d), bidx),
            pl.BlockSpec((1, d), const),
            pl.BlockSpec((1, 1, d), bidx),
            pl.BlockSpec((1, 1, d), bidx),
            pl.BlockSpec(wr.shape, const),
            pl.BlockSpec((1, LANE), const),
        ],
        out_specs=[pl.BlockSpec((tm, d), lambda i: (i, 0)),
                   pl.BlockSpec((tm, d), lambda i: (i, 0)),
                   pl.BlockSpec((tm, LANE), lambda i: (i, 0))],
        out_shape=[jax.ShapeDtypeStruct((n, d), F32),
                   jax.ShapeDtypeStruct((n, d), F32),
                   jax.ShapeDtypeStruct((n, LANE), F32)],
        compiler_params=_cparams(("arbitrary",)),
        name="merge_out",
    )(ya, yb, proj, proj, x2, wa, wb, wo, gt, g2, sc, sh, wr, br)


def _route_kernel(lg_ref, rt_ref, rw_ref, cnt_ref, carry_ref):
    @pl.when(pl.program_id(0) == 0)
    def _():
        carry_ref[...] = jnp.zeros_like(carry_ref)

    lg = lg_ref[...]
    tm = lg.shape[0]
    lane = lax.broadcasted_iota(I32, lg.shape, 1).astype(F32)
    big = float(2 * LANE)
    is_grp = lane < N_GROUPS
    gl = jnp.where(is_grp, lg, NEG)
    gm = jnp.max(gl, axis=-1, keepdims=True)
    ge = jnp.where(is_grp, jnp.exp(gl - gm), 0.0)
    pg = ge / jnp.sum(ge, axis=-1, keepdims=True)
    p_top = jnp.max(pg, axis=-1, keepdims=True)
    g_idx = jnp.min(jnp.where(is_grp & (pg == p_top), lane, big), axis=-1, keepdims=True)

    lo = N_GROUPS + EPG * g_idx
    in_grp = (lane >= lo) & (lane < lo + EPG)
    el = jnp.where(in_grp, lg, NEG)
    em = jnp.max(el, axis=-1, keepdims=True)
    ee = jnp.where(in_grp, jnp.exp(el - em), 0.0)
    pin = ee / jnp.sum(ee, axis=-1, keepdims=True)
    v1 = jnp.max(pin, axis=-1, keepdims=True)
    i1 = jnp.min(jnp.where(in_grp & (pin == v1), lane, big), axis=-1, keepdims=True)
    rest = in_grp & (lane != i1)
    p2 = jnp.where(rest, pin, -1.0)
    v2 = jnp.max(p2, axis=-1, keepdims=True)
    i2 = jnp.min(jnp.where(rest & (p2 == v2), lane, big), axis=-1, keepdims=True)
    vs = v1 + v2
    w1 = p_top * v1 / vs
    w2 = p_top * v2 / vs

    oh1 = lane == i1
    oh2 = lane == i2
    mm = (oh1 | oh2).astype(BF16)
    tri = (lax.broadcasted_iota(I32, (tm, tm), 0) > lax.broadcasted_iota(I32, (tm, tm), 1))
    cnt = jnp.dot(tri.astype(BF16), mm, preferred_element_type=F32) + carry_ref[...]
    r1 = jnp.sum(jnp.where(oh1, cnt, 0.0), axis=-1, keepdims=True)
    r2 = jnp.sum(jnp.where(oh2, cnt, 0.0), axis=-1, keepdims=True)
    carry_ref[...] += jnp.sum(mm.astype(F32), axis=0, keepdims=True)
    cnt_ref[...] = carry_ref[...]

    e1 = i1 - N_GROUPS
    e2 = i2 - N_GROUPS
    packed = jnp.where(lane == 0, e1, jnp.where(lane == 1, e2, jnp.where(
        lane == 2, r1, jnp.where(lane == 3, r2, jnp.where(lane == 4, w1, w2)))))
    rt_ref[...] = packed.T[:8, :]
    rw_ref[...] = jnp.where(lane == 0, w1, w2)


def _route(logits):
    n = logits.shape[0]
    tm = 512
    return pl.pallas_call(
        _route_kernel,
        grid=(n // tm,),
        in_specs=[pl.BlockSpec((tm, LANE), lambda i: (i, 0))],
        out_specs=[pl.BlockSpec((8, tm), lambda i: (0, i)),
                   pl.BlockSpec((tm, LANE), lambda i: (i, 0)),
                   pl.BlockSpec((1, LANE), lambda i: (0, 0))],
        out_shape=[jax.ShapeDtypeStruct((8, n), F32),
                   jax.ShapeDtypeStruct((n, LANE), F32),
                   jax.ShapeDtypeStruct((1, LANE), F32)],
        scratch_shapes=[pltpu.VMEM((1, LANE), F32)],
        compiler_params=_cparams(("arbitrary",)),
        name="route",
    )(logits)


def _dispatch_kernel(dest_ref, h_ref, xs_in_ref, xs_ref, sem):
    del xs_in_ref
    tm = h_ref.shape[0]

    def copy(j, k):
        return pltpu.make_async_copy(h_ref.at[pl.ds(j, 1)],
                                     xs_ref.at[pl.ds(dest_ref[0, 0, k * tm + j], 1)], sem)

    def start(j, c):
        copy(j, 0).start()
        copy(j, 1).start()
        return c

    def wait(j, c):
        copy(j, 0).wait()
        copy(j, 1).wait()
        return c

    lax.fori_loop(0, tm, start, 0)
    lax.fori_loop(0, tm, wait, 0)


def _dispatch(dest3, h, xs0):
    n, d = h.shape
    tm = dest3.shape[2] // TOP_K
    return pl.pallas_call(
        _dispatch_kernel,
        grid=(n // tm,),
        in_specs=[pl.BlockSpec((1, 1, TOP_K * tm), lambda i: (i, 0, 0), memory_space=pltpu.SMEM),
                  pl.BlockSpec((tm, d), lambda i: (i, 0)),
                  pl.BlockSpec(memory_space=pl.ANY)],
        out_specs=pl.BlockSpec(memory_space=pl.ANY),
        out_shape=jax.ShapeDtypeStruct(xs0.shape, xs0.dtype),
        scratch_shapes=[pltpu.SemaphoreType.DMA(())],
        input_output_aliases={2: 0},
        compiler_params=_cparams(("arbitrary",)),
        name="dispatch",
    )(dest3, h, xs0)


def _expert_kernel(be_ref, nu_ref, x_ref, wgu_ref, wd_ref, o_ref, wgu_s, wd_s):
    i = pl.program_id(0)

    @pl.when(i < nu_ref[0])
    def _():
        e = be_ref[i]
        prev = be_ref[jnp.maximum(i - 1, 0)]

        @pl.when((i == 0) | (e != prev))
        def _():
            wgu_s[...] = wgu_ref[0, 0].astype(BF16)
            wd_s[...] = wd_ref[0, 0].astype(BF16)

        gu = jnp.dot(x_ref[...].astype(BF16), wgu_s[...], preferred_element_type=F32)
        gate = gu[:, :D_EXPERT]
        up = gu[:, D_EXPERT:]
        act = (gate * jax.nn.sigmoid(gate) * up).astype(BF16)
        o_ref[...] = jnp.dot(act, wd_s[...], preferred_element_type=F32)

    @pl.when(i >= nu_ref[0])
    def _():
        o_ref[...] = jnp.zeros_like(o_ref)


def _experts(block_e, n_used, xs, w_gu, w_down, layer):
    n_slots, d = xs.shape
    nb = n_slots // EXPERT_TILE

    def blk(i, be, nu):
        return jnp.minimum(i, nu[0] - 1)

    grid_spec = pltpu.PrefetchScalarGridSpec(
        num_scalar_prefetch=2,
        grid=(nb,),
        in_specs=[
            pl.BlockSpec((EXPERT_TILE, d), lambda i, be, nu: (blk(i, be, nu), 0)),
            pl.BlockSpec((1, 1, d, 2 * D_EXPERT),
                         lambda i, be, nu: (layer, be[blk(i, be, nu)], 0, 0)),
            pl.BlockSpec((1, 1, D_EXPERT, d),
                         lambda i, be, nu: (layer, be[blk(i, be, nu)], 0, 0)),
        ],
        out_specs=pl.BlockSpec((EXPERT_TILE, d), lambda i, be, nu: (i, 0)),
        scratch_shapes=[pltpu.VMEM((d, 2 * D_EXPERT), BF16), pltpu.VMEM((D_EXPERT, d), BF16)],
    )
    return pl.pallas_call(
        _expert_kernel,
        grid_spec=grid_spec,
        out_shape=jax.ShapeDtypeStruct((n_slots, d), F32),
        compiler_params=_cparams(("arbitrary",)),
        name="experts",
    )(block_e, n_used, xs, w_gu, w_down)


def _combine_kernel(dest_ref, ys_ref, x_ref, rw_ref, gt_ref, fg_ref, o_ref, y0, y1, sem, *, final):
    tm = x_ref.shape[0]

    def copy(j, k):
        dst = y0 if k == 0 else y1
        return pltpu.make_async_copy(ys_ref.at[pl.ds(dest_ref[0, 0, k * tm + j], 1)],
                                     dst.at[pl.ds(j, 1)], sem)

    def start(j, c):
        copy(j, 0).start()
        copy(j, 1).start()
        return c

    def wait(j, c):
        copy(j, 0).wait()
        copy(j, 1).wait()
        return c

    lax.fori_loop(0, tm, start, 0)
    lax.fori_loop(0, tm, wait, 0)
    rw = rw_ref[...]
    y = rw[:, 0:1] * y0[...] + rw[:, 1:2] * y1[...]
    xn = x_ref[...] + (1.0 + gt_ref[0]) * y
    if final:
        xn = _rms(xn, fg_ref[...])
    o_ref[...] = xn


def _combine(dest3, ys, x2, rw, gt, fg, seq, final):
    n, d = x2.shape
    tm = dest3.shape[2] // TOP_K
    per_b = seq // tm
    return pl.pallas_call(
        functools.partial(_combine_kernel, final=final),
        grid=(n // tm,),
        in_specs=[pl.BlockSpec((1, 1, TOP_K * tm), lambda i: (i, 0, 0), memory_space=pltpu.SMEM),
                  pl.BlockSpec(memory_space=pl.ANY),
                  pl.BlockSpec((tm, d), lambda i: (i, 0)),
                  pl.BlockSpec((tm, LANE), lambda i: (i, 0)),
                  pl.BlockSpec((1, 1, d), lambda i: (i // per_b, 0, 0)),
                  pl.BlockSpec((1, d), lambda i: (0, 0))],
        out_specs=pl.BlockSpec((tm, d), lambda i: (i, 0)),
        out_shape=jax.ShapeDtypeStruct((n, d), F32),
        scratch_shapes=[pltpu.VMEM((tm, d), F32), pltpu.VMEM((tm, d), F32),
                        pltpu.SemaphoreType.DMA(())],
        compiler_params=_cparams(("arbitrary",)),
        name="combine",
    )(dest3, ys, x2, rw, gt, fg)


def _rope_cols(w3):
    half = B_ROPE // 2
    x1, x2 = w3[..., :half], w3[..., half:]
    z = jnp.zeros(w3.shape[:-1] + (LANE - B_ROPE,), w3.dtype)
    a = jnp.concatenate([x1, x2, z], axis=-1)
    b = jnp.concatenate([x2, x1, z], axis=-1)
    return a, b


def _layout_w_in(w):
    d = w.shape[0]
    o = 0
    qa = w[:, o:o + A_WIDTH]; o += A_WIDTH
    ka = w[:, o:o + A_WIDTH]; o += A_WIDTH
    va = w[:, o:o + A_WIDTH]; o += A_WIDTH
    cq = w[:, o:o + Q_LORA]; o += Q_LORA
    ckv = w[:, o:o + KV_LORA]; o += KV_LORA
    kr = w[:, o:o + B_ROPE]; o += B_ROPE
    gates = w[:, o:]
    kra, krb = _rope_cols(kr.reshape(d, 1, B_ROPE))
    return jnp.concatenate([gates, qa, ka, va, cq, ckv, kra.reshape(d, LANE),
                            krb.reshape(d, LANE)], axis=1).astype(BF16)


def _layout_w_uq(w):
    k = w.shape[0]
    w3 = w.reshape(k, B_HEADS, B_QK)
    nope = w3[:, :, :B_NOPE].reshape(k, B_HEADS * LANE)
    a, b = _rope_cols(w3[:, :, B_NOPE:])
    return jnp.concatenate([nope, a.reshape(k, -1), b.reshape(k, -1)], axis=1).astype(BF16)


def _layout_w_ukv(w):
    k = w.shape[0]
    w3 = w.reshape(k, B_HEADS, B_NOPE + B_V)
    return jnp.concatenate([w3[:, :, :B_NOPE].reshape(k, -1),
                            w3[:, :, B_NOPE:].reshape(k, -1)], axis=1).astype(BF16)


def kernel(x, c, positions, ln1_g, ln2_g, w_ada, b_ada, w_in, q_norm_g, w_uq, kv_norm_g, w_ukv,
           w_a_up, w_b_up, w_o, w_grp, b_grp, w_exp, b_exp, w_gu, w_down, final_g):
    bsz, seq, d = x.shape
    depth = w_in.shape[0]
    n = bsz * seq
    n_blocks = n * TOP_K // EXPERT_TILE + N_EXPERTS
    n_slots = n_blocks * EXPERT_TILE
    tile_tok = 256

    mod = _ada(c, w_ada, b_ada).reshape(depth, bsz, N_MOD, 1, d)

    pos_col = positions.reshape(n, 1)
    ctab, stab = _rope_tables(pos_col)
    posc = positions.reshape(bsz, seq, 1)
    posr = [positions.reshape(bsz, seq // dil, dil).transpose(0, 2, 1) for _, dil in A_GROUPS]
    slopes = 2.0 ** (-ALIBI_MAX_BIAS * jnp.arange(1, A_HEADS + 1, dtype=F32) / A_HEADS)
    slopes = slopes.reshape(A_NG, A_HPG).T
    slopes = jnp.pad(slopes, ((0, 0), (0, 8 - A_NG)))
    slopes = jnp.broadcast_to(slopes[:, :, None], (A_HPG, 8, LANE))

    x2 = x.reshape(n, d)
    for l in range(depth):
        sh1, sc1, gt1, sh2, sc2, gt2 = (mod[l, :, m] for m in range(N_MOD))

        proj = _norm_proj(x2, ln1_g[l].reshape(1, d), sc1, sh1, _layout_w_in(w_in[l]), seq)
        q = _mla_q(proj, q_norm_g[l].reshape(1, Q_LORA), _layout_w_uq(w_uq[l]), ctab, stab)
        k, v = _mla_kv(proj, kv_norm_g[l].reshape(1, KV_LORA), _layout_w_ukv(w_ukv[l]), ctab, stab)
        yb = _mla_attn(q, k, v, bsz, seq)
        ya = _dilated_attn(proj.reshape(bsz, seq, PROJ_COLS), posc, posr, slopes).reshape(n, A_OUT)

        w_r = jnp.concatenate([w_grp[l], w_exp[l],
                               jnp.zeros((d, LANE - N_GROUPS - N_EXPERTS), F32)], axis=1)
        w_r_hi = w_r.astype(BF16)
        w_r = jnp.concatenate([w_r_hi, (w_r - w_r_hi.astype(F32)).astype(BF16)], axis=1)
        b_r = jnp.concatenate([b_grp[l], b_exp[l],
                               jnp.zeros((LANE - N_GROUPS - N_EXPERTS,), F32)]).reshape(1, LANE)
        x2, h2, logits = _merge_out(ya, yb, proj, x2, w_a_up[l].astype(BF16),
                                    w_b_up[l].astype(BF16), w_o[l].astype(BF16), gt1,
                                    ln2_g[l].reshape(1, d), sc2, sh2, w_r, b_r, seq)

        rt, rw, cnt = _route(logits)
        counts = cnt[0, N_GROUPS:N_GROUPS + N_EXPERTS].astype(I32)
        padded = (counts + EXPERT_TILE - 1) // EXPERT_TILE * EXPERT_TILE
        pends = jnp.cumsum(padded)
        pstarts = pends - padded
        eids = rt[:TOP_K].astype(I32)
        onehot = eids[:, :, None] == jnp.arange(N_EXPERTS, dtype=I32)
        dest = jnp.sum(jnp.where(onehot, pstarts, 0), axis=-1) + rt[TOP_K:2 * TOP_K].astype(I32)
        dest3 = dest.reshape(TOP_K, n // tile_tok, tile_tok).transpose(1, 0, 2).reshape(
            n // tile_tok, 1, TOP_K * tile_tok)
        n_used = (pends[-1:] // EXPERT_TILE).astype(I32)
        blk_start = jnp.arange(n_blocks, dtype=I32) * EXPERT_TILE
        block_e = jnp.minimum(jnp.sum((pends[None, :] <= blk_start[:, None]).astype(I32), axis=1),
                              N_EXPERTS - 1)

        xs = _dispatch(dest3, h2, jnp.zeros((n_slots, d), F32))
        ys = _experts(block_e, n_used, xs, w_gu, w_down, l)
        x2 = _combine(dest3, ys, x2, rw, gt2, final_g.reshape(1, d), seq, final=(l == depth - 1))
    return x2.reshape(bsz, seq, d)
```

```python
import functools

import jax
import jax.numpy as jnp
from jax import lax
from jax.experimental import pallas as pl
from jax.experimental.pallas import tpu as pltpu

F32 = jnp.float32
BF16 = jnp.bfloat16
I32 = jnp.int32
HIGHEST = lax.Precision.HIGHEST

HEAD_DIM = 128
A_GROUPS = ((128, 1), (512, 4), (2048, 16))
A_HPG = 4
A_NG = len(A_GROUPS)
A_HEADS = A_HPG * A_NG
A_WIDTH = A_HEADS * HEAD_DIM
A_OUT = A_HPG * HEAD_DIM
ALIBI_MAX_BIAS = 8.0
B_HEADS = 8
B_NOPE = 128
B_ROPE = 64
B_V = 128
B_QK = B_NOPE + B_ROPE
Q_LORA = 512
KV_LORA = 512
B_OUT = B_HEADS * B_V
ROPE_THETA = 10000.0
N_GROUPS = 8
EPG = 8
N_EXPERTS = N_GROUPS * EPG
TOP_K = 2
D_EXPERT = 512
N_MOD = 6
EPS = 1e-6
NEG = -1e30
LOG2E = 1.4426950408889634

LANE = 128
VMEM_LIMIT = 56 * 1024 * 1024
QBLK = 128
EXPERT_TILE = 256

COL_GA = 0
COL_GB = 16
COL_QA = 32
COL_KA = 44
COL_VA = 56
COL_CQ = 68
COL_CKV = 72
COL_KR = 76
N_COLBLK = 78
PROJ_COLS = N_COLBLK * LANE


def _cparams(sem):
    return pltpu.CompilerParams(dimension_semantics=sem, vmem_limit_bytes=VMEM_LIMIT)


def _rms(x, g):
    return x * lax.rsqrt(jnp.mean(x * x, axis=-1, keepdims=True) + EPS) * g


def _ada_kernel(c_ref, w_ref, b_ref, o_ref):
    c = c_ref[...]
    cs = c * jax.nn.sigmoid(c)
    o_ref[0] = jnp.dot(cs, w_ref[0], precision=HIGHEST, preferred_element_type=F32) + b_ref[0]


def _ada(c, w_ada, b_ada):
    nl, d, nm = w_ada.shape
    bsz = c.shape[0]
    tn = 1536
    return pl.pallas_call(
        _ada_kernel,
        grid=(nl, nm // tn),
        in_specs=[
            pl.BlockSpec((bsz, d), lambda l, j: (0, 0)),
            pl.BlockSpec((1, d, tn), lambda l, j: (l, 0, j)),
            pl.BlockSpec((1, 1, tn), lambda l, j: (l, 0, j)),
        ],
        out_specs=pl.BlockSpec((1, bsz, tn), lambda l, j: (l, 0, j)),
        out_shape=jax.ShapeDtypeStruct((nl, bsz, nm), F32),
        compiler_params=_cparams(("arbitrary", "arbitrary")),
        name="ada",
    )(c, w_ada, b_ada.reshape(nl, 1, nm))


def _rope_table_kernel(pos_ref, inv_ref, c_ref, s_ref):
    ang = pos_ref[...].astype(F32) * inv_ref[...]
    lane = lax.broadcasted_iota(I32, ang.shape, 1)
    half = B_ROPE // 2
    cs = jnp.cos(ang)
    sn = jnp.sin(ang)
    c_ref[...] = jnp.where(lane < B_ROPE, cs, 0.0)
    s_ref[...] = jnp.where(lane < half, -sn, jnp.where(lane < B_ROPE, sn, 0.0))


def _rope_tables(pos_col):
    n = pos_col.shape[0]
    half = B_ROPE // 2
    inv = ROPE_THETA ** (-jnp.arange(half, dtype=F32) / half)
    inv_row = jnp.concatenate([inv, inv, jnp.zeros((LANE - B_ROPE,), F32)]).reshape(1, LANE)
    tm = 2048
    return pl.pallas_call(
        _rope_table_kernel,
        grid=(n // tm,),
        in_specs=[pl.BlockSpec((tm, 1), lambda i: (i, 0)),
                  pl.BlockSpec((1, LANE), lambda i: (0, 0))],
        out_specs=[pl.BlockSpec((tm, LANE), lambda i: (i, 0)),
                   pl.BlockSpec((tm, LANE), lambda i: (i, 0))],
        out_shape=[jax.ShapeDtypeStruct((n, LANE), F32)] * 2,
        compiler_params=_cparams(("arbitrary",)),
        name="rope_tables",
    )(pos_col, inv_row)


def _norm_proj_kernel(x_ref, g_ref, sc_ref, sh_ref, w_ref, o_ref, h_ref):
    @pl.when(pl.program_id(1) == 0)
    def _():
        h = _rms(x_ref[...], g_ref[...]) * (1.0 + sc_ref[0]) + sh_ref[0]
        h_ref[...] = h.astype(BF16)

    o_ref[...] = jnp.dot(h_ref[...], w_ref[...], preferred_element_type=F32).astype(o_ref.dtype)


def _norm_proj(x2, g, sc, sh, w, seq):
    n, d = x2.shape
    ncol = w.shape[1]
    tm = 1024
    tn = 13 * LANE
    per_b = seq // tm
    return pl.pallas_call(
        _norm_proj_kernel,
        grid=(n // tm, ncol // tn),
        in_specs=[
            pl.BlockSpec((tm, d), lambda i, j: (i, 0)),
            pl.BlockSpec((1, d), lambda i, j: (0, 0)),
            pl.BlockSpec((1, 1, d), lambda i, j: (i // per_b, 0, 0)),
            pl.BlockSpec((1, 1, d), lambda i, j: (i // per_b, 0, 0)),
            pl.BlockSpec((d, tn), lambda i, j: (0, j)),
        ],
        out_specs=pl.BlockSpec((tm, tn), lambda i, j: (i, j)),
        out_shape=jax.ShapeDtypeStruct((n, ncol), BF16),
        scratch_shapes=[pltpu.VMEM((tm, d), BF16)],
        compiler_params=_cparams(("arbitrary", "arbitrary")),
        name="norm_proj",
    )(x2, g, sc, sh, w)


def _mla_q_kernel(cq_ref, g_ref, w_ref, c_ref, s_ref, q_ref):
    h = _rms(cq_ref[...].astype(F32), g_ref[...]).astype(BF16)
    r = jnp.dot(h, w_ref[...], preferred_element_type=F32)
    scale = B_QK ** -0.5 * LOG2E
    c = c_ref[...]
    s = s_ref[...]
    nw = B_HEADS * LANE
    for hd in range(B_HEADS):
        nope = r[:, hd * LANE:(hd + 1) * LANE]
        ra = r[:, nw + hd * LANE: nw + (hd + 1) * LANE]
        rb = r[:, 2 * nw + hd * LANE: 2 * nw + (hd + 1) * LANE]
        q_ref[:, 2 * hd * LANE:(2 * hd + 1) * LANE] = (nope * scale).astype(BF16)
        q_ref[:, (2 * hd + 1) * LANE:(2 * hd + 2) * LANE] = ((ra * c + rb * s) * scale).astype(BF16)


def _mla_q(proj, g, w, ctab, stab):
    n = proj.shape[0]
    tm = 512
    return pl.pallas_call(
        _mla_q_kernel,
        grid=(n // tm,),
        in_specs=[
            pl.BlockSpec((tm, Q_LORA), lambda i: (i, COL_CQ * LANE // Q_LORA)),
            pl.BlockSpec((1, Q_LORA), lambda i: (0, 0)),
            pl.BlockSpec(w.shape, lambda i: (0, 0)),
            pl.BlockSpec((tm, LANE), lambda i: (i, 0)),
            pl.BlockSpec((tm, LANE), lambda i: (i, 0)),
        ],
        out_specs=pl.BlockSpec((tm, 2 * B_HEADS * LANE), lambda i: (i, 0)),
        out_shape=jax.ShapeDtypeStruct((n, 2 * B_HEADS * LANE), BF16),
        compiler_params=_cparams(("arbitrary",)),
        name="mla_q",
    )(proj, g, w, ctab, stab)


def _mla_kv_kernel(ckv_ref, kr_ref, g_ref, w_ref, c_ref, s_ref, k_ref, v_ref):
    h = _rms(ckv_ref[...].astype(F32), g_ref[...]).astype(BF16)
    r = jnp.dot(h, w_ref[...], preferred_element_type=F32)
    kr = kr_ref[...].astype(F32)
    krope = (kr[:, :LANE] * c_ref[...] + kr[:, LANE:] * s_ref[...]).astype(BF16)
    nw = B_HEADS * LANE
    for hd in range(B_HEADS):
        k_ref[:, 2 * hd * LANE:(2 * hd + 1) * LANE] = r[:, hd * LANE:(hd + 1) * LANE].astype(BF16)
        k_ref[:, (2 * hd + 1) * LANE:(2 * hd + 2) * LANE] = krope
    v_ref[...] = r[:, nw:].astype(BF16)


def _mla_kv(proj, g, w, ctab, stab):
    n = proj.shape[0]
    tm = 512
    return pl.pallas_call(
        _mla_kv_kernel,
        grid=(n // tm,),
        in_specs=[
            pl.BlockSpec((tm, KV_LORA), lambda i: (i, COL_CKV * LANE // KV_LORA)),
            pl.BlockSpec((tm, 2 * LANE), lambda i: (i, COL_KR // 2)),
            pl.BlockSpec((1, KV_LORA), lambda i: (0, 0)),
            pl.BlockSpec(w.shape, lambda i: (0, 0)),
            pl.BlockSpec((tm, LANE), lambda i: (i, 0)),
            pl.BlockSpec((tm, LANE), lambda i: (i, 0)),
        ],
        out_specs=[pl.BlockSpec((tm, 2 * B_HEADS * LANE), lambda i: (i, 0)),
                   pl.BlockSpec((tm, B_HEADS * LANE), lambda i: (i, 0))],
        out_shape=[jax.ShapeDtypeStruct((n, 2 * B_HEADS * LANE), BF16),
                   jax.ShapeDtypeStruct((n, B_HEADS * LANE), BF16)],
        compiler_params=_cparams(("arbitrary",)),
        name="mla_kv",
    )(proj, proj, g, w, ctab, stab)


MLA_SUB = 512


def _mla_attn_kernel(q_ref, k_ref, v_ref, o_ref, vx_ref):
    @pl.when(pl.program_id(2) == 0)
    def _():
        vx_ref[:, :LANE] = v_ref[...]
        vx_ref[:, LANE:] = jnp.ones((vx_ref.shape[0], LANE), BF16)

    for j in range(q_ref.shape[0] // MLA_SUB):
        rows = slice(j * MLA_SUB, (j + 1) * MLA_SUB)
        s = lax.dot_general(q_ref[rows, :], k_ref[...], (((1,), (1,)), ((), ())),
                            preferred_element_type=F32)
        m = jnp.max(s, axis=-1, keepdims=True)
        p = jnp.exp2(s - m).astype(BF16)
        ox = jnp.dot(p, vx_ref[...], preferred_element_type=F32)
        o_ref[rows, :] = (ox[:, :LANE] / ox[:, LANE:LANE + 1]).astype(o_ref.dtype)


def _mla_attn(q, k, v, bsz, seq):
    n = q.shape[0]
    tq = 2048
    nq = seq // tq
    return pl.pallas_call(
        _mla_attn_kernel,
        grid=(bsz, B_HEADS, nq),
        in_specs=[
            pl.BlockSpec((tq, 2 * LANE), lambda b, h, i: (b * nq + i, h)),
            pl.BlockSpec((seq, 2 * LANE), lambda b, h, i: (b, h)),
            pl.BlockSpec((seq, LANE), lambda b, h, i: (b, h)),
        ],
        out_specs=pl.BlockSpec((tq, LANE), lambda b, h, i: (b * nq + i, h)),
        out_shape=jax.ShapeDtypeStruct((n, B_HEADS * LANE), BF16),
        scratch_shapes=[pltpu.VMEM((seq, 2 * LANE), BF16)],
        compiler_params=_cparams(("arbitrary", "arbitrary", "arbitrary")),
        name="mla_attn",
    )(q, k, v)


def _dilated_kernel(*refs, seq):
    qkv = refs[:3 * A_NG]
    posc_ref = refs[3 * A_NG]
    posr = refs[3 * A_NG + 1: 3 * A_NG + 1 + A_NG]
    sl_ref = refs[3 * A_NG + 1 + A_NG]
    o_ref = refs[3 * A_NG + 2 + A_NG]
    qf, kf, vf, og, lg = refs[3 * A_NG + 3 + A_NG:]
    scale = HEAD_DIM ** -0.5

    for g, (win, dil) in enumerate(A_GROUPS):
        radius = win // (2 * dil)
        length = seq // dil
        kw = min(2 * QBLK, length)
        q_in, k_in, v_in = qkv[3 * g], qkv[3 * g + 1], qkv[3 * g + 2]
        slope2 = sl_ref[0, g:g + 1, 0:1] * LOG2E
        if dil > 1:
            qf[...] = q_in[0].astype(F32)
            kf[...] = k_in[0].astype(F32)
            vf[...] = v_in[0].astype(F32)

        def rows(start, size, dil=dil):
            if dil == 1:
                return pl.ds(start, size)
            return pl.ds(start, size, stride=dil)

        def load(ref_in, ref_f32, rws, dil=dil):
            if dil == 1:
                return ref_in[0, rws, :]
            return ref_f32[rws, :].astype(BF16)

        units = []
        for r in range(dil):
            for ib in range(length // QBLK):
                i0 = ib * QBLK
                ws = min(max(i0 - radius, 0), length - kw)
                units.append((r, i0, ws, rows(r + dil * i0, QBLK), rows(r + dil * ws, kw)))
        qs = [(load(q_in, qf, u[3]).astype(F32) * (scale * LOG2E)).astype(BF16) for u in units]
        ks = [load(k_in, kf, u[4]) for u in units]
        ss = [lax.dot_general(q, k, (((1,), (1,)), ((), ())), preferred_element_type=F32)
              for q, k in zip(qs, ks)]
        biased = []
        for (r, i0, ws, q_rows, _), s in zip(units, ss):
            pq = posc_ref[0, q_rows, :]
            pk = posr[g][0, r:r + 1, ws:ws + kw]
            dist = jnp.abs(pq - pk).astype(F32)
            ii = i0 + lax.broadcasted_iota(I32, (QBLK, kw), 0)
            jj = ws + lax.broadcasted_iota(I32, (QBLK, kw), 1)
            biased.append(jnp.where(jnp.abs(ii - jj) <= radius, s - slope2 * dist, NEG))
        ms = [jnp.max(s, axis=-1, keepdims=True) for s in biased]
        ps = [jnp.exp2(s - m) for s, m in zip(biased, ms)]
        dens = [jnp.sum(p, axis=-1, keepdims=True) for p in ps]
        vs = [load(v_in, vf, u[4]) for u in units]
        outs = [jnp.dot(p.astype(BF16), v, preferred_element_type=F32) / den
                for p, v, den in zip(ps, vs, dens)]
        for u, o, m, den in zip(units, outs, ms, dens):
            og[g, u[3], :] = o
            lg[g, u[3], :] = jnp.broadcast_to(m * (1.0 / LOG2E) + jnp.log(den), (QBLK, HEAD_DIM))

    chunk = 256

    def merge(ci, carry):
        rs = pl.ds(pl.multiple_of(ci * chunk, chunk), chunk)
        l0, l1, l2 = lg[0, rs, :], lg[1, rs, :], lg[2, rs, :]
        m = jnp.maximum(jnp.maximum(l0, l1), l2)
        w0, w1, w2 = jnp.exp(l0 - m), jnp.exp(l1 - m), jnp.exp(l2 - m)
        o = (w0 * og[0, rs, :] + w1 * og[1, rs, :] + w2 * og[2, rs, :]) / (w0 + w1 + w2)
        o_ref[0, rs, :] = o.astype(o_ref.dtype)
        return carry

    lax.fori_loop(0, seq // chunk, merge, 0)


def _dilated_attn(proj3, posc, posr, slopes):
    bsz, seq, _ = proj3.shape
    in_specs = []
    args = []
    for g in range(A_NG):
        for base in (COL_QA, COL_KA, COL_VA):
            in_specs.append(pl.BlockSpec(
                (1, seq, LANE), lambda b, h, base=base, g=g: (b, 0, base + A_HPG * g + h)))
            args.append(proj3)
    in_specs.append(pl.BlockSpec((1, seq, 1), lambda b, h: (b, 0, 0)))
    args.append(posc)
    for g, (_, dil) in enumerate(A_GROUPS):
        in_specs.append(pl.BlockSpec((1, dil, seq // dil), lambda b, h: (b, 0, 0)))
        args.append(posr[g])
    in_specs.append(pl.BlockSpec((1, 8, LANE), lambda b, h: (h, 0, 0)))
    args.append(slopes)
    return pl.pallas_call(
        functools.partial(_dilated_kernel, seq=seq),
        grid=(bsz, A_HPG),
        in_specs=in_specs,
        out_specs=pl.BlockSpec((1, seq, LANE), lambda b, h: (b, 0, h)),
        out_shape=jax.ShapeDtypeStruct((bsz, seq, A_OUT), BF16),
        scratch_shapes=[pltpu.VMEM((seq, HEAD_DIM), F32)] * 3
        + [pltpu.VMEM((A_NG, seq, HEAD_DIM), F32)] * 2,
        compiler_params=_cparams(("arbitrary", "arbitrary")),
        name="dilated_attn",
    )(*args)


def _merge_out_kernel(ya_ref, yb_ref, ga_ref, gb_ref, x_ref, wa_ref, wb_ref, wo_ref, gt_ref,
                      g2_ref, sc_ref, sh_ref, wr_ref, br_ref, xo_ref, h_ref, lg_ref):
    a = jnp.dot(ya_ref[...], wa_ref[...], preferred_element_type=F32)
    b = jnp.dot(yb_ref[...], wb_ref[...], preferred_element_type=F32)
    ga = jax.nn.sigmoid(ga_ref[...].astype(F32))
    gb = jax.nn.sigmoid(gb_ref[...].astype(F32))
    merged = (ga * a + gb * b).astype(BF16)
    xn = x_ref[...] + (1.0 + gt_ref[0]) * jnp.dot(merged, wo_ref[...], preferred_element_type=F32)
    xo_ref[...] = xn
    h = _rms(xn, g2_ref[...]) * (1.0 + sc_ref[0]) + sh_ref[0]
    h_ref[...] = h
    h_hi = h.astype(BF16)
    h_lo = (h - h_hi.astype(F32)).astype(BF16)
    t = jnp.dot(h_hi, wr_ref[...], preferred_element_type=F32)
    u = jnp.dot(h_lo, wr_ref[:, :LANE], preferred_element_type=F32)
    lg_ref[...] = t[:, :LANE] + t[:, LANE:] + u + br_ref[...]


def _merge_out(ya, yb, proj, x2, wa, wb, wo, gt, g2, sc, sh, wr, br, seq):
    n, d = x2.shape
    tm = 256
    per_b = seq // tm
    const = lambda i: (0, 0)
    bidx = lambda i: (i // per_b, 0, 0)
    return pl.pallas_call(
        _merge_out_kernel,
        grid=(n // tm,),
        in_specs=[
            pl.BlockSpec((tm, A_OUT), lambda i: (i, 0)),
            pl.BlockSpec((tm, B_OUT), lambda i: (i, 0)),
            pl.BlockSpec((tm, d), lambda i: (i, COL_GA * LANE // d)),
            pl.BlockSpec((tm, d), lambda i: (i, COL_GB * LANE // d)),
            pl.BlockSpec((tm, d), lambda i: (i, 0)),
            pl.BlockSpec(wa.shape, const),
            pl.BlockSpec(wb.shape, const),
            pl.BlockSpec(wo.shape, const),
            pl.BlockSpec((1, 1, d), bidx),
            pl.BlockSpec((1, d), const),
            pl.BlockSpec((1, 1, d), bidx),
            pl.BlockSpec((1, 1, d), bidx),
            pl.BlockSpec(wr.shape, const),
            pl.BlockSpec((1, LANE), const),
        ],
        out_specs=[pl.BlockSpec((tm, d), lambda i: (i, 0)),
                   pl.BlockSpec((tm, d), lambda i: (i, 0)),
                   pl.BlockSpec((tm, LANE), lambda i: (i, 0))],
        out_shape=[jax.ShapeDtypeStruct((n, d), F32),
                   jax.ShapeDtypeStruct((n, d), F32),
                   jax.ShapeDtypeStruct((n, LANE), F32)],
        compiler_params=_cparams(("arbitrary",)),
        name="merge_out",
    )(ya, yb, proj, proj, x2, wa, wb, wo, gt, g2, sc, sh, wr, br)


def _route_kernel(lg_ref, rt_ref, rw_ref, cnt_ref, carry_ref):
    @pl.when(pl.program_id(0) == 0)
    def _():
        carry_ref[...] = jnp.zeros_like(carry_ref)

    lg = lg_ref[...]
    tm = lg.shape[0]
    lane = lax.broadcasted_iota(I32, lg.shape, 1).astype(F32)
    big = float(2 * LANE)
    is_grp = lane < N_GROUPS
    gl = jnp.where(is_grp, lg, NEG)
    gm = jnp.max(gl, axis=-1, keepdims=True)
    ge = jnp.where(is_grp, jnp.exp(gl - gm), 0.0)
    pg = ge / jnp.sum(ge, axis=-1, keepdims=True)
    p_top = jnp.max(pg, axis=-1, keepdims=True)
    g_idx = jnp.min(jnp.where(is_grp & (pg == p_top), lane, big), axis=-1, keepdims=True)

    lo = N_GROUPS + EPG * g_idx
    in_grp = (lane >= lo) & (lane < lo + EPG)
    el = jnp.where(in_grp, lg, NEG)
    em = jnp.max(el, axis=-1, keepdims=True)
    ee = jnp.where(in_grp, jnp.exp(el - em), 0.0)
    pin = ee / jnp.sum(ee, axis=-1, keepdims=True)
    v1 = jnp.max(pin, axis=-1, keepdims=True)
    i1 = jnp.min(jnp.where(in_grp & (pin == v1), lane, big), axis=-1, keepdims=True)
    rest = in_grp & (lane != i1)
    p2 = jnp.where(rest, pin, -1.0)
    v2 = jnp.max(p2, axis=-1, keepdims=True)
    i2 = jnp.min(jnp.where(rest & (p2 == v2), lane, big), axis=-1, keepdims=True)
    vs = v1 + v2
    w1 = p_top * v1 / vs
    w2 = p_top * v2 / vs

    oh1 = lane == i1
    oh2 = lane == i2
    mm = (oh1 | oh2).astype(BF16)
    tri = (lax.broadcasted_iota(I32, (tm, tm), 0) > lax.broadcasted_iota(I32, (tm, tm), 1))
    cnt = jnp.dot(tri.astype(BF16), mm, preferred_element_type=F32) + carry_ref[...]
    r1 = jnp.sum(jnp.where(oh1, cnt, 0.0), axis=-1, keepdims=True)
    r2 = jnp.sum(jnp.where(oh2, cnt, 0.0), axis=-1, keepdims=True)
    carry_ref[...] += jnp.sum(mm.astype(F32), axis=0, keepdims=True)
    cnt_ref[...] = carry_ref[...]

    e1 = i1 - N_GROUPS
    e2 = i2 - N_GROUPS
    packed = jnp.where(lane == 0, e1, jnp.where(lane == 1, e2, jnp.where(
        lane == 2, r1, jnp.where(lane == 3, r2, jnp.where(lane == 4, w1, w2)))))
    rt_ref[...] = packed.T[:8, :]
    rw_ref[...] = jnp.where(lane == 0, w1, w2)


def _route(logits):
    n = logits.shape[0]
    tm = 512
    return pl.pallas_call(
        _route_kernel,
        grid=(n // tm,),
        in_specs=[pl.BlockSpec((tm, LANE), lambda i: (i, 0))],
        out_specs=[pl.BlockSpec((8, tm), lambda i: (0, i)),
                   pl.BlockSpec((tm, LANE), lambda i: (i, 0)),
                   pl.BlockSpec((1, LANE), lambda i: (0, 0))],
        out_shape=[jax.ShapeDtypeStruct((8, n), F32),
                   jax.ShapeDtypeStruct((n, LANE), F32),
                   jax.ShapeDtypeStruct((1, LANE), F32)],
        scratch_shapes=[pltpu.VMEM((1, LANE), F32)],
        compiler_params=_cparams(("arbitrary",)),
        name="route",
    )(logits)


def _start_rows(src_ref, dst_ref, idx_ref, base, count, sem):
    for j in range(count):
        pltpu.make_async_copy(src_ref.at[pl.ds(idx_ref[0, 0, base + j], 1)],
                              dst_ref.at[pl.ds(j, 1)], sem).start()


def _wait_rows(src_ref, dst_ref, sem):
    pltpu.make_async_copy(src_ref.at[pl.ds(0, dst_ref.shape[0])], dst_ref, sem).wait()


def _expert_kernel(be_ref, nu_ref, tok0_ref, tokn_ref, h_ref, wgu_ref, wd_ref, o_ref,
                   xbuf0, xbuf1, wgu_s, wd_s, sem):
    i = pl.program_id(0)
    n_used = nu_ref[0]
    tb = xbuf0.shape[0]

    @pl.when(i == 0)
    def _():
        _start_rows(h_ref, xbuf0, tok0_ref, 0, tb, sem.at[0])

    @pl.when(i < n_used)
    def _():
        e = be_ref[i]
        prev = be_ref[jnp.maximum(i - 1, 0)]

        @pl.when((i == 0) | (e != prev))
        def _():
            wgu_s[...] = wgu_ref[0, 0].astype(BF16)
            wd_s[...] = wd_ref[0, 0].astype(BF16)

        def block(cur_buf, cur_sem, nxt_buf, nxt_sem):
            _wait_rows(h_ref, cur_buf, cur_sem)
            _start_rows(h_ref, nxt_buf, tokn_ref, 0, tb, nxt_sem)
            gu = jnp.dot(cur_buf[...].astype(BF16), wgu_s[...], preferred_element_type=F32)
            gate = gu[:, :D_EXPERT]
            up = gu[:, D_EXPERT:]
            act = (gate * jax.nn.sigmoid(gate) * up).astype(BF16)
            o_ref[...] = jnp.dot(act, wd_s[...], preferred_element_type=F32)

            @pl.when(i + 1 >= n_used)
            def _():
                _wait_rows(h_ref, nxt_buf, nxt_sem)

        @pl.when(i % 2 == 0)
        def _():
            block(xbuf0, sem.at[0], xbuf1, sem.at[1])

        @pl.when(i % 2 == 1)
        def _():
            block(xbuf1, sem.at[1], xbuf0, sem.at[0])

    @pl.when(i >= n_used)
    def _():
        o_ref[...] = jnp.zeros_like(o_ref)


def _experts(block_e, n_used, slot_tok3, h, w_gu, w_down, layer):
    d = h.shape[1]
    nb = slot_tok3.shape[0]

    def blk(i, nu):
        return jnp.minimum(i, nu[0] - 1)

    grid_spec = pltpu.PrefetchScalarGridSpec(
        num_scalar_prefetch=2,
        grid=(nb,),
        in_specs=[
            pl.BlockSpec((1, 1, EXPERT_TILE), lambda i, be, nu: (0, 0, 0),
                         memory_space=pltpu.SMEM),
            pl.BlockSpec((1, 1, EXPERT_TILE), lambda i, be, nu: (blk(i + 1, nu), 0, 0),
                         memory_space=pltpu.SMEM),
            pl.BlockSpec(memory_space=pl.ANY),
            pl.BlockSpec((1, 1, d, 2 * D_EXPERT),
                         lambda i, be, nu: (layer, be[blk(i, nu)], 0, 0)),
            pl.BlockSpec((1, 1, D_EXPERT, d),
                         lambda i, be, nu: (layer, be[blk(i, nu)], 0, 0)),
        ],
        out_specs=pl.BlockSpec((EXPERT_TILE, d), lambda i, be, nu: (i, 0)),
        scratch_shapes=[pltpu.VMEM((EXPERT_TILE, d), F32), pltpu.VMEM((EXPERT_TILE, d), F32),
                        pltpu.VMEM((d, 2 * D_EXPERT), BF16), pltpu.VMEM((D_EXPERT, d), BF16),
                        pltpu.SemaphoreType.DMA((2,))],
    )
    return pl.pallas_call(
        _expert_kernel,
        grid_spec=grid_spec,
        out_shape=jax.ShapeDtypeStruct((nb * EXPERT_TILE, d), F32),
        compiler_params=_cparams(("arbitrary",)),
        name="experts",
    )(block_e, n_used, slot_tok3, slot_tok3, h, w_gu, w_down)


def _combine_kernel(dest0_ref, destn_ref, ys_ref, x_ref, rw_ref, gt_ref, fg_ref, o_ref, ybuf, sem,
                    *, final):
    i = pl.program_id(0)
    tm = x_ref.shape[0]

    def start(dest_ref, slot):
        for k in range(TOP_K):
            _start_rows(ys_ref, ybuf.at[slot, k], dest_ref, k * tm, tm, sem.at[slot, k])

    @pl.when(i == 0)
    def _():
        start(dest0_ref, 0)

    @pl.when(i + 1 < pl.num_programs(0))
    def _():
        start(destn_ref, (i + 1) % 2)

    cur = i % 2
    for k in range(TOP_K):
        _wait_rows(ys_ref, ybuf.at[cur, k], sem.at[cur, k])
    rw = rw_ref[...]
    y = rw[:, 0:1] * ybuf[cur, 0] + rw[:, 1:2] * ybuf[cur, 1]
    xn = x_ref[...] + (1.0 + gt_ref[0]) * y
    if final:
        xn = _rms(xn, fg_ref[...])
    o_ref[...] = xn


def _combine(dest3, ys, x2, rw, gt, fg, seq, final):
    n, d = x2.shape
    tm = dest3.shape[2] // TOP_K
    per_b = seq // tm
    nt = n // tm
    return pl.pallas_call(
        functools.partial(_combine_kernel, final=final),
        grid=(nt,),
        in_specs=[pl.BlockSpec((1, 1, TOP_K * tm), lambda i: (0, 0, 0), memory_space=pltpu.SMEM),
                  pl.BlockSpec((1, 1, TOP_K * tm), lambda i: (jnp.minimum(i + 1, nt - 1), 0, 0),
                               memory_space=pltpu.SMEM),
                  pl.BlockSpec(memory_space=pl.ANY),
                  pl.BlockSpec((tm, d), lambda i: (i, 0)),
                  pl.BlockSpec((tm, LANE), lambda i: (i, 0)),
                  pl.BlockSpec((1, 1, d), lambda i: (i // per_b, 0, 0)),
                  pl.BlockSpec((1, d), lambda i: (0, 0))],
        out_specs=pl.BlockSpec((tm, d), lambda i: (i, 0)),
        out_shape=jax.ShapeDtypeStruct((n, d), F32),
        scratch_shapes=[pltpu.VMEM((2, TOP_K, tm, d), F32),
                        pltpu.SemaphoreType.DMA((2, TOP_K))],
        compiler_params=_cparams(("arbitrary",)),
        name="combine",
    )(dest3, dest3, ys, x2, rw, gt, fg)


def _rope_cols(w3):
    half = B_ROPE // 2
    x1, x2 = w3[..., :half], w3[..., half:]
    z = jnp.zeros(w3.shape[:-1] + (LANE - B_ROPE,), w3.dtype)
    a = jnp.concatenate([x1, x2, z], axis=-1)
    b = jnp.concatenate([x2, x1, z], axis=-1)
    return a, b


def _layout_w_in(w):
    d = w.shape[0]
    o = 0
    qa = w[:, o:o + A_WIDTH]; o += A_WIDTH
    ka = w[:, o:o + A_WIDTH]; o += A_WIDTH
    va = w[:, o:o + A_WIDTH]; o += A_WIDTH
    cq = w[:, o:o + Q_LORA]; o += Q_LORA
    ckv = w[:, o:o + KV_LORA]; o += KV_LORA
    kr = w[:, o:o + B_ROPE]; o += B_ROPE
    gates = w[:, o:]
    kra, krb = _rope_cols(kr.reshape(d, 1, B_ROPE))
    return jnp.concatenate([gates, qa, ka, va, cq, ckv, kra.reshape(d, LANE),
                            krb.reshape(d, LANE)], axis=1).astype(BF16)


def _layout_w_uq(w):
    k = w.shape[0]
    w3 = w.reshape(k, B_HEADS, B_QK)
    nope = w3[:, :, :B_NOPE].reshape(k, B_HEADS * LANE)
    a, b = _rope_cols(w3[:, :, B_NOPE:])
    return jnp.concatenate([nope, a.reshape(k, -1), b.reshape(k, -1)], axis=1).astype(BF16)


def _layout_w_ukv(w):
    k = w.shape[0]
    w3 = w.reshape(k, B_HEADS, B_NOPE + B_V)
    return jnp.concatenate([w3[:, :, :B_NOPE].reshape(k, -1),
                            w3[:, :, B_NOPE:].reshape(k, -1)], axis=1).astype(BF16)


def kernel(x, c, positions, ln1_g, ln2_g, w_ada, b_ada, w_in, q_norm_g, w_uq, kv_norm_g, w_ukv,
           w_a_up, w_b_up, w_o, w_grp, b_grp, w_exp, b_exp, w_gu, w_down, final_g):
    bsz, seq, d = x.shape
    depth = w_in.shape[0]
    n = bsz * seq
    n_blocks = n * TOP_K // EXPERT_TILE + N_EXPERTS
    n_slots = n_blocks * EXPERT_TILE
    tile_tok = 256

    mod = _ada(c, w_ada, b_ada).reshape(depth, bsz, N_MOD, 1, d)

    pos_col = positions.reshape(n, 1)
    ctab, stab = _rope_tables(pos_col)
    posc = positions.reshape(bsz, seq, 1)
    posr = [positions.reshape(bsz, seq // dil, dil).transpose(0, 2, 1) for _, dil in A_GROUPS]
    slopes = 2.0 ** (-ALIBI_MAX_BIAS * jnp.arange(1, A_HEADS + 1, dtype=F32) / A_HEADS)
    slopes = slopes.reshape(A_NG, A_HPG).T
    slopes = jnp.pad(slopes, ((0, 0), (0, 8 - A_NG)))
    slopes = jnp.broadcast_to(slopes[:, :, None], (A_HPG, 8, LANE))

    x2 = x.reshape(n, d)
    for l in range(depth):
        sh1, sc1, gt1, sh2, sc2, gt2 = (mod[l, :, m] for m in range(N_MOD))

        proj = _norm_proj(x2, ln1_g[l].reshape(1, d), sc1, sh1, _layout_w_in(w_in[l]), seq)
        q = _mla_q(proj, q_norm_g[l].reshape(1, Q_LORA), _layout_w_uq(w_uq[l]), ctab, stab)
        k, v = _mla_kv(proj, kv_norm_g[l].reshape(1, KV_LORA), _layout_w_ukv(w_ukv[l]), ctab, stab)
        yb = _mla_attn(q, k, v, bsz, seq)
        ya = _dilated_attn(proj.reshape(bsz, seq, PROJ_COLS), posc, posr, slopes).reshape(n, A_OUT)

        w_r = jnp.concatenate([w_grp[l], w_exp[l],
                               jnp.zeros((d, LANE - N_GROUPS - N_EXPERTS), F32)], axis=1)
        w_r_hi = w_r.astype(BF16)
        w_r = jnp.concatenate([w_r_hi, (w_r - w_r_hi.astype(F32)).astype(BF16)], axis=1)
        b_r = jnp.concatenate([b_grp[l], b_exp[l],
                               jnp.zeros((LANE - N_GROUPS - N_EXPERTS,), F32)]).reshape(1, LANE)
        x2, h2, logits = _merge_out(ya, yb, proj, x2, w_a_up[l].astype(BF16),
                                    w_b_up[l].astype(BF16), w_o[l].astype(BF16), gt1,
                                    ln2_g[l].reshape(1, d), sc2, sh2, w_r, b_r, seq)

        rt, rw, cnt = _route(logits)
        counts = cnt[0, N_GROUPS:N_GROUPS + N_EXPERTS].astype(I32)
        padded = (counts + EXPERT_TILE - 1) // EXPERT_TILE * EXPERT_TILE
        pends = jnp.cumsum(padded)
        pstarts = pends - padded
        eids = rt[:TOP_K].astype(I32)
        onehot = eids[:, :, None] == jnp.arange(N_EXPERTS, dtype=I32)
        dest = jnp.sum(jnp.where(onehot, pstarts, 0), axis=-1) + rt[TOP_K:2 * TOP_K].astype(I32)
        dest3 = dest.reshape(TOP_K, n // tile_tok, tile_tok).transpose(1, 0, 2).reshape(
            n // tile_tok, 1, TOP_K * tile_tok)
        n_used = (pends[-1:] // EXPERT_TILE).astype(I32)
        blk_start = jnp.arange(n_blocks, dtype=I32) * EXPERT_TILE
        block_e = jnp.minimum(jnp.sum((pends[None, :] <= blk_start[:, None]).astype(I32), axis=1),
                              N_EXPERTS - 1)

        tok = jnp.broadcast_to(jnp.arange(n, dtype=I32), (TOP_K, n))
        slot_tok = jnp.zeros((n_slots,), I32).at[dest.reshape(-1)].set(
            tok.reshape(-1), unique_indices=True)
        ys = _experts(block_e, n_used, slot_tok.reshape(n_blocks, 1, EXPERT_TILE), h2,
                      w_gu, w_down, l)
        x2 = _combine(dest3, ys, x2, rw, gt2, final_g.reshape(1, d), seq, final=(l == depth - 1))
    return x2.reshape(bsz, seq, d)
```

```python
import functools

import jax
import jax.numpy as jnp
from jax import lax
from jax.experimental import pallas as pl
from jax.experimental.pallas import tpu as pltpu

F32 = jnp.float32
BF16 = jnp.bfloat16
I32 = jnp.int32
HIGHEST = lax.Precision.HIGHEST

HEAD_DIM = 128
A_GROUPS = ((128, 1), (512, 4), (2048, 16))
A_HPG = 4
A_NG = len(A_GROUPS)
A_HEADS = A_HPG * A_NG
A_WIDTH = A_HEADS * HEAD_DIM
A_OUT = A_HPG * HEAD_DIM
ALIBI_MAX_BIAS = 8.0
B_HEADS = 8
B_NOPE = 128
B_ROPE = 64
B_V = 128
B_QK = B_NOPE + B_ROPE
Q_LORA = 512
KV_LORA = 512
B_OUT = B_HEADS * B_V
ROPE_THETA = 10000.0
N_GROUPS = 8
EPG = 8
N_EXPERTS = N_GROUPS * EPG
TOP_K = 2
D_EXPERT = 512
N_MOD = 6
EPS = 1e-6
NEG = -1e30
LOG2E = 1.4426950408889634

LANE = 128
VMEM_LIMIT = 56 * 1024 * 1024
QBLK = 128
EXPERT_TILE = 256

COL_GA = 0
COL_GB = 16
COL_QA = 32
COL_KA = 44
COL_VA = 56
COL_CQ = 68
COL_CKV = 72
COL_KR = 76
N_COLBLK = 78
PROJ_COLS = N_COLBLK * LANE


def _cparams(sem):
    return pltpu.CompilerParams(dimension_semantics=sem, vmem_limit_bytes=VMEM_LIMIT)


def _rms(x, g):
    return x * lax.rsqrt(jnp.mean(x * x, axis=-1, keepdims=True) + EPS) * g


def _ada_kernel(c_ref, w_ref, b_ref, o_ref):
    c = c_ref[...]
    cs = c * jax.nn.sigmoid(c)
    o_ref[0] = jnp.dot(cs, w_ref[0], precision=HIGHEST, preferred_element_type=F32) + b_ref[0]


def _ada(c, w_ada, b_ada):
    nl, d, nm = w_ada.shape
    bsz = c.shape[0]
    tn = 1536
    return pl.pallas_call(
        _ada_kernel,
        grid=(nl, nm // tn),
        in_specs=[
            pl.BlockSpec((bsz, d), lambda l, j: (0, 0)),
            pl.BlockSpec((1, d, tn), lambda l, j: (l, 0, j)),
            pl.BlockSpec((1, 1, tn), lambda l, j: (l, 0, j)),
        ],
        out_specs=pl.BlockSpec((1, bsz, tn), lambda l, j: (l, 0, j)),
        out_shape=jax.ShapeDtypeStruct((nl, bsz, nm), F32),
        compiler_params=_cparams(("arbitrary", "arbitrary")),
        name="ada",
    )(c, w_ada, b_ada.reshape(nl, 1, nm))


def _rope_table_kernel(pos_ref, inv_ref, c_ref, s_ref):
    ang = pos_ref[...].astype(F32) * inv_ref[...]
    lane = lax.broadcasted_iota(I32, ang.shape, 1)
    half = B_ROPE // 2
    cs = jnp.cos(ang)
    sn = jnp.sin(ang)
    c_ref[...] = jnp.where(lane < B_ROPE, cs, 0.0)
    s_ref[...] = jnp.where(lane < half, -sn, jnp.where(lane < B_ROPE, sn, 0.0))


def _rope_tables(pos_col):
    n = pos_col.shape[0]
    half = B_ROPE // 2
    inv = ROPE_THETA ** (-jnp.arange(half, dtype=F32) / half)
    inv_row = jnp.concatenate([inv, inv, jnp.zeros((LANE - B_ROPE,), F32)]).reshape(1, LANE)
    tm = 2048
    return pl.pallas_call(
        _rope_table_kernel,
        grid=(n // tm,),
        in_specs=[pl.BlockSpec((tm, 1), lambda i: (i, 0)),
                  pl.BlockSpec((1, LANE), lambda i: (0, 0))],
        out_specs=[pl.BlockSpec((tm, LANE), lambda i: (i, 0)),
                   pl.BlockSpec((tm, LANE), lambda i: (i, 0))],
        out_shape=[jax.ShapeDtypeStruct((n, LANE), F32)] * 2,
        compiler_params=_cparams(("arbitrary",)),
        name="rope_tables",
    )(pos_col, inv_row)


def _norm_proj_kernel(x_ref, g_ref, sc_ref, sh_ref, w_ref, o_ref, h_ref):
    @pl.when(pl.program_id(1) == 0)
    def _():
        h = _rms(x_ref[...], g_ref[...]) * (1.0 + sc_ref[0]) + sh_ref[0]
        h_ref[...] = h.astype(BF16)

    o_ref[...] = jnp.dot(h_ref[...], w_ref[...], preferred_element_type=F32).astype(o_ref.dtype)


def _norm_proj(x2, g, sc, sh, w, seq):
    n, d = x2.shape
    ncol = w.shape[1]
    tm = 1024
    tn = 13 * LANE
    per_b = seq // tm
    return pl.pallas_call(
        _norm_proj_kernel,
        grid=(n // tm, ncol // tn),
        in_specs=[
            pl.BlockSpec((tm, d), lambda i, j: (i, 0)),
            pl.BlockSpec((1, d), lambda i, j: (0, 0)),
            pl.BlockSpec((1, 1, d), lambda i, j: (i // per_b, 0, 0)),
            pl.BlockSpec((1, 1, d), lambda i, j: (i // per_b, 0, 0)),
            pl.BlockSpec((d, tn), lambda i, j: (0, j)),
        ],
        out_specs=pl.BlockSpec((tm, tn), lambda i, j: (i, j)),
        out_shape=jax.ShapeDtypeStruct((n, ncol), BF16),
        scratch_shapes=[pltpu.VMEM((tm, d), BF16)],
        compiler_params=_cparams(("arbitrary", "arbitrary")),
        name="norm_proj",
    )(x2, g, sc, sh, w)


def _mla_q_kernel(cq_ref, g_ref, w_ref, c_ref, s_ref, q_ref):
    h = _rms(cq_ref[...].astype(F32), g_ref[...]).astype(BF16)
    r = jnp.dot(h, w_ref[...], preferred_element_type=F32)
    scale = B_QK ** -0.5 * LOG2E
    c = c_ref[...]
    s = s_ref[...]
    nw = B_HEADS * LANE
    for hd in range(B_HEADS):
        nope = r[:, hd * LANE:(hd + 1) * LANE]
        ra = r[:, nw + hd * LANE: nw + (hd + 1) * LANE]
        rb = r[:, 2 * nw + hd * LANE: 2 * nw + (hd + 1) * LANE]
        q_ref[:, 2 * hd * LANE:(2 * hd + 1) * LANE] = (nope * scale).astype(BF16)
        q_ref[:, (2 * hd + 1) * LANE:(2 * hd + 2) * LANE] = ((ra * c + rb * s) * scale).astype(BF16)


def _mla_q(proj, g, w, ctab, stab):
    n = proj.shape[0]
    tm = 512
    return pl.pallas_call(
        _mla_q_kernel,
        grid=(n // tm,),
        in_specs=[
            pl.BlockSpec((tm, Q_LORA), lambda i: (i, COL_CQ * LANE // Q_LORA)),
            pl.BlockSpec((1, Q_LORA), lambda i: (0, 0)),
            pl.BlockSpec(w.shape, lambda i: (0, 0)),
            pl.BlockSpec((tm, LANE), lambda i: (i, 0)),
            pl.BlockSpec((tm, LANE), lambda i: (i, 0)),
        ],
        out_specs=pl.BlockSpec((tm, 2 * B_HEADS * LANE), lambda i: (i, 0)),
        out_shape=jax.ShapeDtypeStruct((n, 2 * B_HEADS * LANE), BF16),
        compiler_params=_cparams(("arbitrary",)),
        name="mla_q",
    )(proj, g, w, ctab, stab)


def _mla_kv_kernel(ckv_ref, kr_ref, g_ref, w_ref, c_ref, s_ref, k_ref, v_ref):
    h = _rms(ckv_ref[...].astype(F32), g_ref[...]).astype(BF16)
    r = jnp.dot(h, w_ref[...], preferred_element_type=F32)
    kr = kr_ref[...].astype(F32)
    krope = (kr[:, :LANE] * c_ref[...] + kr[:, LANE:] * s_ref[...]).astype(BF16)
    nw = B_HEADS * LANE
    for hd in range(B_HEADS):
        k_ref[:, 2 * hd * LANE:(2 * hd + 1) * LANE] = r[:, hd * LANE:(hd + 1) * LANE].astype(BF16)
        k_ref[:, (2 * hd + 1) * LANE:(2 * hd + 2) * LANE] = krope
    v_ref[...] = r[:, nw:].astype(BF16)


def _mla_kv(proj, g, w, ctab, stab):
    n = proj.shape[0]
    tm = 512
    return pl.pallas_call(
        _mla_kv_kernel,
        grid=(n // tm,),
        in_specs=[
            pl.BlockSpec((tm, KV_LORA), lambda i: (i, COL_CKV * LANE // KV_LORA)),
            pl.BlockSpec((tm, 2 * LANE), lambda i: (i, COL_KR // 2)),
            pl.BlockSpec((1, KV_LORA), lambda i: (0, 0)),
            pl.BlockSpec(w.shape, lambda i: (0, 0)),
            pl.BlockSpec((tm, LANE), lambda i: (i, 0)),
            pl.BlockSpec((tm, LANE), lambda i: (i, 0)),
        ],
        out_specs=[pl.BlockSpec((tm, 2 * B_HEADS * LANE), lambda i: (i, 0)),
                   pl.BlockSpec((tm, B_HEADS * LANE), lambda i: (i, 0))],
        out_shape=[jax.ShapeDtypeStruct((n, 2 * B_HEADS * LANE), BF16),
                   jax.ShapeDtypeStruct((n, B_HEADS * LANE), BF16)],
        compiler_params=_cparams(("arbitrary",)),
        name="mla_kv",
    )(proj, proj, g, w, ctab, stab)


MLA_SUB = 512


def _mla_attn_kernel(q_ref, k_ref, v_ref, o_ref, vx_ref):
    @pl.when(pl.program_id(2) == 0)
    def _():
        vx_ref[:, :LANE] = v_ref[...]
        vx_ref[:, LANE:] = jnp.ones((vx_ref.shape[0], LANE), BF16)

    for j in range(q_ref.shape[0] // MLA_SUB):
        rows = slice(j * MLA_SUB, (j + 1) * MLA_SUB)
        s = lax.dot_general(q_ref[rows, :], k_ref[...], (((1,), (1,)), ((), ())),
                            preferred_element_type=F32)
        m = jnp.max(s, axis=-1, keepdims=True)
        p = jnp.exp2(s - m).astype(BF16)
        ox = jnp.dot(p, vx_ref[...], preferred_element_type=F32)
        o_ref[rows, :] = (ox[:, :LANE] / ox[:, LANE:LANE + 1]).astype(o_ref.dtype)


def _mla_attn(q, k, v, bsz, seq):
    n = q.shape[0]
    tq = 2048
    nq = seq // tq
    return pl.pallas_call(
        _mla_attn_kernel,
        grid=(bsz, B_HEADS, nq),
        in_specs=[
            pl.BlockSpec((tq, 2 * LANE), lambda b, h, i: (b * nq + i, h)),
            pl.BlockSpec((seq, 2 * LANE), lambda b, h, i: (b, h)),
            pl.BlockSpec((seq, LANE), lambda b, h, i: (b, h)),
        ],
        out_specs=pl.BlockSpec((tq, LANE), lambda b, h, i: (b * nq + i, h)),
        out_shape=jax.ShapeDtypeStruct((n, B_HEADS * LANE), BF16),
        scratch_shapes=[pltpu.VMEM((seq, 2 * LANE), BF16)],
        compiler_params=_cparams(("arbitrary", "arbitrary", "arbitrary")),
        name="mla_attn",
    )(q, k, v)


def _dilated_kernel(*refs, seq):
    qkv = refs[:3 * A_NG]
    posc_ref = refs[3 * A_NG]
    posr = refs[3 * A_NG + 1: 3 * A_NG + 1 + A_NG]
    sl_ref = refs[3 * A_NG + 1 + A_NG]
    o_ref = refs[3 * A_NG + 2 + A_NG]
    qf, kf, vf, og, lg = refs[3 * A_NG + 3 + A_NG:]
    scale = HEAD_DIM ** -0.5

    for g, (win, dil) in enumerate(A_GROUPS):
        radius = win // (2 * dil)
        length = seq // dil
        kw = min(2 * QBLK, length)
        q_in, k_in, v_in = qkv[3 * g], qkv[3 * g + 1], qkv[3 * g + 2]
        slope2 = sl_ref[0, g:g + 1, 0:1] * LOG2E
        if dil > 1:
            qf[...] = q_in[0].astype(F32)
            kf[...] = k_in[0].astype(F32)
            vf[...] = v_in[0].astype(F32)

        def rows(start, size, dil=dil):
            if dil == 1:
                return pl.ds(start, size)
            return pl.ds(start, size, stride=dil)

        def load(ref_in, ref_f32, rws, dil=dil):
            if dil == 1:
                return ref_in[0, rws, :]
            return ref_f32[rws, :].astype(BF16)

        units = []
        for r in range(dil):
            for ib in range(length // QBLK):
                i0 = ib * QBLK
                ws = min(max(i0 - radius, 0), length - kw)
                units.append((r, i0, ws, rows(r + dil * i0, QBLK), rows(r + dil * ws, kw)))
        qs = [(load(q_in, qf, u[3]).astype(F32) * (scale * LOG2E)).astype(BF16) for u in units]
        ks = [load(k_in, kf, u[4]) for u in units]
        ss = [lax.dot_general(q, k, (((1,), (1,)), ((), ())), preferred_element_type=F32)
              for q, k in zip(qs, ks)]
        biased = []
        for (r, i0, ws, q_rows, _), s in zip(units, ss):
            pq = posc_ref[0, q_rows, :]
            pk = posr[g][0, r:r + 1, ws:ws + kw]
            dist = jnp.abs(pq - pk).astype(F32)
            ii = i0 + lax.broadcasted_iota(I32, (QBLK, kw), 0)
            jj = ws + lax.broadcasted_iota(I32, (QBLK, kw), 1)
            biased.append(jnp.where(jnp.abs(ii - jj) <= radius, s - slope2 * dist, NEG))
        ms = [jnp.max(s, axis=-1, keepdims=True) for s in biased]
        ps = [jnp.exp2(s - m) for s, m in zip(biased, ms)]
        dens = [jnp.sum(p, axis=-1, keepdims=True) for p in ps]
        vs = [load(v_in, vf, u[4]) for u in units]
        outs = [jnp.dot(p.astype(BF16), v, preferred_element_type=F32) / den
                for p, v, den in zip(ps, vs, dens)]
        for u, o, m, den in zip(units, outs, ms, dens):
            og[g, u[3], :] = o
            lg[g, u[3], :] = jnp.broadcast_to(m * (1.0 / LOG2E) + jnp.log(den), (QBLK, HEAD_DIM))

    chunk = 256

    def merge(ci, carry):
        rs = pl.ds(pl.multiple_of(ci * chunk, chunk), chunk)
        l0, l1, l2 = lg[0, rs, :], lg[1, rs, :], lg[2, rs, :]
        m = jnp.maximum(jnp.maximum(l0, l1), l2)
        w0, w1, w2 = jnp.exp(l0 - m), jnp.exp(l1 - m), jnp.exp(l2 - m)
        o = (w0 * og[0, rs, :] + w1 * og[1, rs, :] + w2 * og[2, rs, :]) / (w0 + w1 + w2)
        o_ref[0, rs, :] = o.astype(o_ref.dtype)
        return carry

    lax.fori_loop(0, seq // chunk, merge, 0)


def _dilated_attn(proj3, posc, posr, slopes):
    bsz, seq, _ = proj3.shape
    in_specs = []
    args = []
    for g in range(A_NG):
        for base in (COL_QA, COL_KA, COL_VA):
            in_specs.append(pl.BlockSpec(
                (1, seq, LANE), lambda b, h, base=base, g=g: (b, 0, base + A_HPG * g + h)))
            args.append(proj3)
    in_specs.append(pl.BlockSpec((1, seq, 1), lambda b, h: (b, 0, 0)))
    args.append(posc)
    for g, (_, dil) in enumerate(A_GROUPS):
        in_specs.append(pl.BlockSpec((1, dil, seq // dil), lambda b, h: (b, 0, 0)))
        args.append(posr[g])
    in_specs.append(pl.BlockSpec((1, 8, LANE), lambda b, h: (h, 0, 0)))
    args.append(slopes)
    return pl.pallas_call(
        functools.partial(_dilated_kernel, seq=seq),
        grid=(bsz, A_HPG),
        in_specs=in_specs,
        out_specs=pl.BlockSpec((1, seq, LANE), lambda b, h: (b, 0, h)),
        out_shape=jax.ShapeDtypeStruct((bsz, seq, A_OUT), BF16),
        scratch_shapes=[pltpu.VMEM((seq, HEAD_DIM), F32)] * 3
        + [pltpu.VMEM((A_NG, seq, HEAD_DIM), F32)] * 2,
        compiler_params=_cparams(("arbitrary", "arbitrary")),
        name="dilated_attn",
    )(*args)


def _merge_out_kernel(ya_ref, yb_ref, ga_ref, gb_ref, x_ref, wa_ref, wb_ref, wo_ref, gt_ref,
                      g2_ref, sc_ref, sh_ref, wr_ref, br_ref, xo_ref, h_ref, lg_ref):
    a = jnp.dot(ya_ref[...], wa_ref[...], preferred_element_type=F32)
    b = jnp.dot(yb_ref[...], wb_ref[...], preferred_element_type=F32)
    ga = jax.nn.sigmoid(ga_ref[...].astype(F32))
    gb = jax.nn.sigmoid(gb_ref[...].astype(F32))
    merged = (ga * a + gb * b).astype(BF16)
    xn = x_ref[...] + (1.0 + gt_ref[0]) * jnp.dot(merged, wo_ref[...], preferred_element_type=F32)
    xo_ref[...] = xn
    h = _rms(xn, g2_ref[...]) * (1.0 + sc_ref[0]) + sh_ref[0]
    h_ref[...] = h
    h_hi = h.astype(BF16)
    h_lo = (h - h_hi.astype(F32)).astype(BF16)
    t = jnp.dot(h_hi, wr_ref[...], preferred_element_type=F32)
    u = jnp.dot(h_lo, wr_ref[:, :LANE], preferred_element_type=F32)
    lg_ref[...] = t[:, :LANE] + t[:, LANE:] + u + br_ref[...]


def _merge_out(ya, yb, proj, x2, wa, wb, wo, gt, g2, sc, sh, wr, br, seq):
    n, d = x2.shape
    tm = 256
    per_b = seq // tm
    const = lambda i: (0, 0)
    bidx = lambda i: (i // per_b, 0, 0)
    return pl.pallas_call(
        _merge_out_kernel,
        grid=(n // tm,),
        in_specs=[
            pl.BlockSpec((tm, A_OUT), lambda i: (i, 0)),
            pl.BlockSpec((tm, B_OUT), lambda i: (i, 0)),
            pl.BlockSpec((tm, d), lambda i: (i, COL_GA * LANE // d)),
            pl.BlockSpec((tm, d), lambda i: (i, COL_GB * LANE // d)),
            pl.BlockSpec((tm, d), lambda i: (i, 0)),
            pl.BlockSpec(wa.shape, const),
            pl.BlockSpec(wb.shape, const),
            pl.BlockSpec(wo.shape, const),
            pl.BlockSpec((1, 1, d), bidx),
            pl.BlockSpec((1, d), const),
            pl.BlockSpec((1, 1, d), bidx),
            pl.BlockSpec((1, 1, d), bidx),
            pl.BlockSpec(wr.shape, const),
            pl.BlockSpec((1, LANE), const),
        ],
        out_specs=[pl.BlockSpec((tm, d), lambda i: (i, 0)),
                   pl.BlockSpec((tm, d), lambda i: (i, 0)),
                   pl.BlockSpec((tm, LANE), lambda i: (i, 0))],
        out_shape=[jax.ShapeDtypeStruct((n, d), F32),
                   jax.ShapeDtypeStruct((n, d), F32),
                   jax.ShapeDtypeStruct((n, LANE), F32)],
        compiler_params=_cparams(("arbitrary",)),
        name="merge_out",
    )(ya, yb, proj, proj, x2, wa, wb, wo, gt, g2, sc, sh, wr, br)


def _route_kernel(lg_ref, rt_ref, rw_ref, cnt_ref, carry_ref):
    @pl.when(pl.program_id(0) == 0)
    def _():
        carry_ref[...] = jnp.zeros_like(carry_ref)

    lg = lg_ref[...]
    tm = lg.shape[0]
    lane = lax.broadcasted_iota(I32, lg.shape, 1).astype(F32)
    big = float(2 * LANE)
    is_grp = lane < N_GROUPS
    gl = jnp.where(is_grp, lg, NEG)
    gm = jnp.max(gl, axis=-1, keepdims=True)
    ge = jnp.where(is_grp, jnp.exp(gl - gm), 0.0)
    pg = ge / jnp.sum(ge, axis=-1, keepdims=True)
    p_top = jnp.max(pg, axis=-1, keepdims=True)
    g_idx = jnp.min(jnp.where(is_grp & (pg == p_top), lane, big), axis=-1, keepdims=True)

    lo = N_GROUPS + EPG * g_idx
    in_grp = (lane >= lo) & (lane < lo + EPG)
    el = jnp.where(in_grp, lg, NEG)
    em = jnp.max(el, axis=-1, keepdims=True)
    ee = jnp.where(in_grp, jnp.exp(el - em), 0.0)
    pin = ee / jnp.sum(ee, axis=-1, keepdims=True)
    v1 = jnp.max(pin, axis=-1, keepdims=True)
    i1 = jnp.min(jnp.where(in_grp & (pin == v1), lane, big), axis=-1, keepdims=True)
    rest = in_grp & (lane != i1)
    p2 = jnp.where(rest, pin, -1.0)
    v2 = jnp.max(p2, axis=-1, keepdims=True)
    i2 = jnp.min(jnp.where(rest & (p2 == v2), lane, big), axis=-1, keepdims=True)
    vs = v1 + v2
    w1 = p_top * v1 / vs
    w2 = p_top * v2 / vs

    oh1 = lane == i1
    oh2 = lane == i2
    mm = (oh1 | oh2).astype(BF16)
    tri = (lax.broadcasted_iota(I32, (tm, tm), 0) > lax.broadcasted_iota(I32, (tm, tm), 1))
    cnt = jnp.dot(tri.astype(BF16), mm, preferred_element_type=F32) + carry_ref[...]
    r1 = jnp.sum(jnp.where(oh1, cnt, 0.0), axis=-1, keepdims=True)
    r2 = jnp.sum(jnp.where(oh2, cnt, 0.0), axis=-1, keepdims=True)
    carry_ref[...] += jnp.sum(mm.astype(F32), axis=0, keepdims=True)
    cnt_ref[...] = carry_ref[...]

    e1 = i1 - N_GROUPS
    e2 = i2 - N_GROUPS
    packed = jnp.where(lane == 0, e1, jnp.where(lane == 1, e2, jnp.where(
        lane == 2, r1, jnp.where(lane == 3, r2, jnp.where(lane == 4, w1, w2)))))
    rt_ref[...] = packed.T[:8, :]
    rw_ref[...] = jnp.where(lane == 0, w1, w2)


def _route(logits):
    n = logits.shape[0]
    tm = 512
    return pl.pallas_call(
        _route_kernel,
        grid=(n // tm,),
        in_specs=[pl.BlockSpec((tm, LANE), lambda i: (i, 0))],
        out_specs=[pl.BlockSpec((8, tm), lambda i: (0, i)),
                   pl.BlockSpec((tm, LANE), lambda i: (i, 0)),
                   pl.BlockSpec((1, LANE), lambda i: (0, 0))],
        out_shape=[jax.ShapeDtypeStruct((8, n), F32),
                   jax.ShapeDtypeStruct((n, LANE), F32),
                   jax.ShapeDtypeStruct((1, LANE), F32)],
        scratch_shapes=[pltpu.VMEM((1, LANE), F32)],
        compiler_params=_cparams(("arbitrary",)),
        name="route",
    )(logits)


def _start_rows(src_ref, dst_ref, idx_ref, base, count, sem):
    for j in range(count):
        pltpu.make_async_copy(src_ref.at[pl.ds(idx_ref[0, 0, base + j], 1)],
                              dst_ref.at[pl.ds(j, 1)], sem).start()


def _wait_rows(src_ref, dst_ref, sem):
    pltpu.make_async_copy(src_ref.at[pl.ds(0, dst_ref.shape[0])], dst_ref, sem).wait()


def _expert_kernel(be_ref, nx_ref, nu_ref, tok0_ref, tokn_ref, h_ref, wgu_ref, wd_ref, o_ref,
                   xbuf0, xbuf1, wgu_f, wd_f, wgu_s, wd_s, sem, wsem, *, layer):
    i = pl.program_id(0)
    n_used = nu_ref[0]
    tb = xbuf0.shape[0]

    def weight_copies(e):
        return (pltpu.make_async_copy(wgu_ref.at[layer, e], wgu_f, wsem.at[0]),
                pltpu.make_async_copy(wd_ref.at[layer, e], wd_f, wsem.at[1]))

    @pl.when(i == 0)
    def _():
        for cp in weight_copies(be_ref[0]):
            cp.start(priority=1)
        _start_rows(h_ref, xbuf0, tok0_ref, 0, tb, sem.at[0])

    @pl.when(i < n_used)
    def _():
        e = be_ref[i]
        prev = be_ref[jnp.maximum(i - 1, 0)]

        @pl.when((i == 0) | (e != prev))
        def _():
            for cp in weight_copies(e):
                cp.wait()
            wgu_s[...] = wgu_f[...].astype(BF16)
            wd_s[...] = wd_f[...].astype(BF16)

            @pl.when(nx_ref[i] >= 0)
            def _():
                for cp in weight_copies(nx_ref[i]):
                    cp.start(priority=1)

        def block(cur_buf, cur_sem, nxt_buf, nxt_sem):
            _wait_rows(h_ref, cur_buf, cur_sem)
            _start_rows(h_ref, nxt_buf, tokn_ref, 0, tb, nxt_sem)
            gu = jnp.dot(cur_buf[...].astype(BF16), wgu_s[...], preferred_element_type=F32)
            gate = gu[:, :D_EXPERT]
            up = gu[:, D_EXPERT:]
            act = (gate * jax.nn.sigmoid(gate) * up).astype(BF16)
            o_ref[...] = jnp.dot(act, wd_s[...], preferred_element_type=F32)

            @pl.when(i + 1 >= n_used)
            def _():
                _wait_rows(h_ref, nxt_buf, nxt_sem)

        @pl.when(i % 2 == 0)
        def _():
            block(xbuf0, sem.at[0], xbuf1, sem.at[1])

        @pl.when(i % 2 == 1)
        def _():
            block(xbuf1, sem.at[1], xbuf0, sem.at[0])

    @pl.when(i >= n_used)
    def _():
        o_ref[...] = jnp.zeros_like(o_ref)


def _experts(block_e, next_e, n_used, slot_tok3, h, w_gu, w_down, layer):
    d = h.shape[1]
    nb = slot_tok3.shape[0]

    def nxt_blk(i, be, nx, nu):
        return (jnp.minimum(i + 1, nu[0] - 1), 0, 0)

    grid_spec = pltpu.PrefetchScalarGridSpec(
        num_scalar_prefetch=3,
        grid=(nb,),
        in_specs=[
            pl.BlockSpec((1, 1, EXPERT_TILE), lambda i, be, nx, nu: (0, 0, 0),
                         memory_space=pltpu.SMEM),
            pl.BlockSpec((1, 1, EXPERT_TILE), nxt_blk, memory_space=pltpu.SMEM),
            pl.BlockSpec(memory_space=pl.ANY),
            pl.BlockSpec(memory_space=pl.ANY),
            pl.BlockSpec(memory_space=pl.ANY),
        ],
        out_specs=pl.BlockSpec((EXPERT_TILE, d), lambda i, be, nx, nu: (i, 0)),
        scratch_shapes=[pltpu.VMEM((EXPERT_TILE, d), F32), pltpu.VMEM((EXPERT_TILE, d), F32),
                        pltpu.VMEM((d, 2 * D_EXPERT), F32), pltpu.VMEM((D_EXPERT, d), F32),
                        pltpu.VMEM((d, 2 * D_EXPERT), BF16), pltpu.VMEM((D_EXPERT, d), BF16),
                        pltpu.SemaphoreType.DMA((2,)), pltpu.SemaphoreType.DMA((2,))],
    )
    return pl.pallas_call(
        functools.partial(_expert_kernel, layer=layer),
        grid_spec=grid_spec,
        out_shape=jax.ShapeDtypeStruct((nb * EXPERT_TILE, d), F32),
        compiler_params=_cparams(("arbitrary",)),
        name="experts",
    )(block_e, next_e, n_used, slot_tok3, slot_tok3, h, w_gu, w_down)


def _combine_kernel(dest0_ref, destn_ref, ys_ref, x_ref, rw_ref, gt_ref, fg_ref, o_ref, ybuf, sem,
                    *, final):
    i = pl.program_id(0)
    tm = x_ref.shape[0]

    def start(dest_ref, slot):
        for k in range(TOP_K):
            _start_rows(ys_ref, ybuf.at[slot, k], dest_ref, k * tm, tm, sem.at[slot, k])

    @pl.when(i == 0)
    def _():
        start(dest0_ref, 0)

    @pl.when(i + 1 < pl.num_programs(0))
    def _():
        start(destn_ref, (i + 1) % 2)

    cur = i % 2
    for k in range(TOP_K):
        _wait_rows(ys_ref, ybuf.at[cur, k], sem.at[cur, k])
    rw = rw_ref[...]
    y = rw[:, 0:1] * ybuf[cur, 0] + rw[:, 1:2] * ybuf[cur, 1]
    xn = x_ref[...] + (1.0 + gt_ref[0]) * y
    if final:
        xn = _rms(xn, fg_ref[...])
    o_ref[...] = xn


def _combine(dest3, ys, x2, rw, gt, fg, seq, final):
    n, d = x2.shape
    tm = dest3.shape[2] // TOP_K
    per_b = seq // tm
    nt = n // tm
    return pl.pallas_call(
        functools.partial(_combine_kernel, final=final),
        grid=(nt,),
        in_specs=[pl.BlockSpec((1, 1, TOP_K * tm), lambda i: (0, 0, 0), memory_space=pltpu.SMEM),
                  pl.BlockSpec((1, 1, TOP_K * tm), lambda i: (jnp.minimum(i + 1, nt - 1), 0, 0),
                               memory_space=pltpu.SMEM),
                  pl.BlockSpec(memory_space=pl.ANY),
                  pl.BlockSpec((tm, d), lambda i: (i, 0)),
                  pl.BlockSpec((tm, LANE), lambda i: (i, 0)),
                  pl.BlockSpec((1, 1, d), lambda i: (i // per_b, 0, 0)),
                  pl.BlockSpec((1, d), lambda i: (0, 0))],
        out_specs=pl.BlockSpec((tm, d), lambda i: (i, 0)),
        out_shape=jax.ShapeDtypeStruct((n, d), F32),
        scratch_shapes=[pltpu.VMEM((2, TOP_K, tm, d), F32),
                        pltpu.SemaphoreType.DMA((2, TOP_K))],
        compiler_params=_cparams(("arbitrary",)),
        name="combine",
    )(dest3, dest3, ys, x2, rw, gt, fg)


def _rope_cols(w3):
    half = B_ROPE // 2
    x1, x2 = w3[..., :half], w3[..., half:]
    z = jnp.zeros(w3.shape[:-1] + (LANE - B_ROPE,), w3.dtype)
    a = jnp.concatenate([x1, x2, z], axis=-1)
    b = jnp.concatenate([x2, x1, z], axis=-1)
    return a, b


def _layout_w_in(w):
    d = w.shape[0]
    o = 0
    qa = w[:, o:o + A_WIDTH]; o += A_WIDTH
    ka = w[:, o:o + A_WIDTH]; o += A_WIDTH
    va = w[:, o:o + A_WIDTH]; o += A_WIDTH
    cq = w[:, o:o + Q_LORA]; o += Q_LORA
    ckv = w[:, o:o + KV_LORA]; o += KV_LORA
    kr = w[:, o:o + B_ROPE]; o += B_ROPE
    gates = w[:, o:]
    kra, krb = _rope_cols(kr.reshape(d, 1, B_ROPE))
    return jnp.concatenate([gates, qa, ka, va, cq, ckv, kra.reshape(d, LANE),
                            krb.reshape(d, LANE)], axis=1).astype(BF16)


def _layout_w_uq(w):
    k = w.shape[0]
    w3 = w.reshape(k, B_HEADS, B_QK)
    nope = w3[:, :, :B_NOPE].reshape(k, B_HEADS * LANE)
    a, b = _rope_cols(w3[:, :, B_NOPE:])
    return jnp.concatenate([nope, a.reshape(k, -1), b.reshape(k, -1)], axis=1).astype(BF16)


def _layout_w_ukv(w):
    k = w.shape[0]
    w3 = w.reshape(k, B_HEADS, B_NOPE + B_V)
    return jnp.concatenate([w3[:, :, :B_NOPE].reshape(k, -1),
                            w3[:, :, B_NOPE:].reshape(k, -1)], axis=1).astype(BF16)


def kernel(x, c, positions, ln1_g, ln2_g, w_ada, b_ada, w_in, q_norm_g, w_uq, kv_norm_g, w_ukv,
           w_a_up, w_b_up, w_o, w_grp, b_grp, w_exp, b_exp, w_gu, w_down, final_g):
    bsz, seq, d = x.shape
    depth = w_in.shape[0]
    n = bsz * seq
    n_blocks = n * TOP_K // EXPERT_TILE + N_EXPERTS
    n_slots = n_blocks * EXPERT_TILE
    tile_tok = 256

    mod = _ada(c, w_ada, b_ada).reshape(depth, bsz, N_MOD, 1, d)

    pos_col = positions.reshape(n, 1)
    ctab, stab = _rope_tables(pos_col)
    posc = positions.reshape(bsz, seq, 1)
    posr = [positions.reshape(bsz, seq // dil, dil).transpose(0, 2, 1) for _, dil in A_GROUPS]
    slopes = 2.0 ** (-ALIBI_MAX_BIAS * jnp.arange(1, A_HEADS + 1, dtype=F32) / A_HEADS)
    slopes = slopes.reshape(A_NG, A_HPG).T
    slopes = jnp.pad(slopes, ((0, 0), (0, 8 - A_NG)))
    slopes = jnp.broadcast_to(slopes[:, :, None], (A_HPG, 8, LANE))

    x2 = x.reshape(n, d)
    for l in range(depth):
        sh1, sc1, gt1, sh2, sc2, gt2 = (mod[l, :, m] for m in range(N_MOD))

        proj = _norm_proj(x2, ln1_g[l].reshape(1, d), sc1, sh1, _layout_w_in(w_in[l]), seq)
        q = _mla_q(proj, q_norm_g[l].reshape(1, Q_LORA), _layout_w_uq(w_uq[l]), ctab, stab)
        k, v = _mla_kv(proj, kv_norm_g[l].reshape(1, KV_LORA), _layout_w_ukv(w_ukv[l]), ctab, stab)
        yb = _mla_attn(q, k, v, bsz, seq)
        ya = _dilated_attn(proj.reshape(bsz, seq, PROJ_COLS), posc, posr, slopes).reshape(n, A_OUT)

        w_r = jnp.concatenate([w_grp[l], w_exp[l],
                               jnp.zeros((d, LANE - N_GROUPS - N_EXPERTS), F32)], axis=1)
        w_r_hi = w_r.astype(BF16)
        w_r = jnp.concatenate([w_r_hi, (w_r - w_r_hi.astype(F32)).astype(BF16)], axis=1)
        b_r = jnp.concatenate([b_grp[l], b_exp[l],
                               jnp.zeros((LANE - N_GROUPS - N_EXPERTS,), F32)]).reshape(1, LANE)
        x2, h2, logits = _merge_out(ya, yb, proj, x2, w_a_up[l].astype(BF16),
                                    w_b_up[l].astype(BF16), w_o[l].astype(BF16), gt1,
                                    ln2_g[l].reshape(1, d), sc2, sh2, w_r, b_r, seq)

        rt, rw, cnt = _route(logits)
        counts = cnt[0, N_GROUPS:N_GROUPS + N_EXPERTS].astype(I32)
        padded = (counts + EXPERT_TILE - 1) // EXPERT_TILE * EXPERT_TILE
        pends = jnp.cumsum(padded)
        pstarts = pends - padded
        eids = rt[:TOP_K].astype(I32)
        onehot = eids[:, :, None] == jnp.arange(N_EXPERTS, dtype=I32)
        dest = jnp.sum(jnp.where(onehot, pstarts, 0), axis=-1) + rt[TOP_K:2 * TOP_K].astype(I32)
        dest3 = dest.reshape(TOP_K, n // tile_tok, tile_tok).transpose(1, 0, 2).reshape(
            n // tile_tok, 1, TOP_K * tile_tok)
        n_used = (pends[-1:] // EXPERT_TILE).astype(I32)
        blk_start = jnp.arange(n_blocks, dtype=I32) * EXPERT_TILE
        block_e = jnp.minimum(jnp.sum((pends[None, :] <= blk_start[:, None]).astype(I32), axis=1),
                              N_EXPERTS - 1)

        tok = jnp.broadcast_to(jnp.arange(n, dtype=I32), (TOP_K, n))
        slot_tok = jnp.zeros((n_slots,), I32).at[dest.reshape(-1)].set(
            tok.reshape(-1), unique_indices=True)
        eidx = jnp.arange(N_EXPERTS, dtype=I32)
        nxt_used = lax.cummin(jnp.where(counts > 0, eidx, N_EXPERTS), reverse=True)
        nxt_after = jnp.concatenate([nxt_used[1:], jnp.full((1,), N_EXPERTS, I32)])
        nxt_after = jnp.where(nxt_after >= N_EXPERTS, -1, nxt_after)
        next_e = jnp.sum(jnp.where(block_e[:, None] == eidx[None, :], nxt_after[None, :], 0),
                         axis=1).astype(I32)
        ys = _experts(block_e, next_e, n_used, slot_tok.reshape(n_blocks, 1, EXPERT_TILE), h2,
                      w_gu, w_down, l)
        x2 = _combine(dest3, ys, x2, rw, gt2, final_g.reshape(1, d), seq, final=(l == depth - 1))
    return x2.reshape(bsz, seq, d)
```

```python
import functools

import jax
import jax.numpy as jnp
from jax import lax
from jax.experimental import pallas as pl
from jax.experimental.pallas import tpu as pltpu

F32 = jnp.float32
BF16 = jnp.bfloat16
I32 = jnp.int32
U32 = jnp.uint32
HIGHEST = lax.Precision.HIGHEST

HEAD_DIM = 128
A_GROUPS = ((128, 1), (512, 4), (2048, 16))
A_HPG = 4
A_NG = len(A_GROUPS)
A_HEADS = A_HPG * A_NG
A_WIDTH = A_HEADS * HEAD_DIM
A_OUT = A_HPG * HEAD_DIM
ALIBI_MAX_BIAS = 8.0
B_HEADS = 8
B_NOPE = 128
B_ROPE = 64
B_V = 128
B_QK = B_NOPE + B_ROPE
Q_LORA = 512
KV_LORA = 512
B_OUT = B_HEADS * B_V
ROPE_THETA = 10000.0
N_GROUPS = 8
EPG = 8
N_EXPERTS = N_GROUPS * EPG
TOP_K = 2
D_EXPERT = 512
N_MOD = 6
EPS = 1e-6
NEG = -1e30
LOG2E = 1.4426950408889634

LANE = 128
VMEM_LIMIT = 56 * 1024 * 1024
QBLK = 128
EXPERT_TILE = 256

COL_GA = 0
COL_GB = 16
COL_QA = 32
COL_KA = 44
COL_VA = 56
COL_CQ = 68
COL_CKV = 72
COL_KR = 76
N_COLBLK = 78
PROJ_COLS = N_COLBLK * LANE


def _cparams(sem):
    return pltpu.CompilerParams(dimension_semantics=sem, vmem_limit_bytes=VMEM_LIMIT)


def _rms(x, g):
    return x * lax.rsqrt(jnp.mean(x * x, axis=-1, keepdims=True) + EPS) * g


def _ada_kernel(c_ref, w_ref, b_ref, o_ref):
    c = c_ref[...]
    cs = c * jax.nn.sigmoid(c)
    o_ref[0] = jnp.dot(cs, w_ref[0], precision=HIGHEST, preferred_element_type=F32) + b_ref[0]


def _ada(c, w_ada, b_ada):
    nl, d, nm = w_ada.shape
    bsz = c.shape[0]
    tn = 1536
    return pl.pallas_call(
        _ada_kernel,
        grid=(nl, nm // tn),
        in_specs=[
            pl.BlockSpec((bsz, d), lambda l, j: (0, 0)),
            pl.BlockSpec((1, d, tn), lambda l, j: (l, 0, j)),
            pl.BlockSpec((1, 1, tn), lambda l, j: (l, 0, j)),
        ],
        out_specs=pl.BlockSpec((1, bsz, tn), lambda l, j: (l, 0, j)),
        out_shape=jax.ShapeDtypeStruct((nl, bsz, nm), F32),
        compiler_params=_cparams(("arbitrary", "arbitrary")),
        name="ada",
    )(c, w_ada, b_ada.reshape(nl, 1, nm))


def _rope_table_kernel(pos_ref, inv_ref, c_ref, s_ref):
    ang = pos_ref[...].astype(F32) * inv_ref[...]
    lane = lax.broadcasted_iota(I32, ang.shape, 1)
    half = B_ROPE // 2
    cs = jnp.cos(ang)
    sn = jnp.sin(ang)
    c_ref[...] = jnp.where(lane < B_ROPE, cs, 0.0)
    s_ref[...] = jnp.where(lane < half, -sn, jnp.where(lane < B_ROPE, sn, 0.0))


def _rope_tables(pos_col):
    n = pos_col.shape[0]
    half = B_ROPE // 2
    inv = ROPE_THETA ** (-jnp.arange(half, dtype=F32) / half)
    inv_row = jnp.concatenate([inv, inv, jnp.zeros((LANE - B_ROPE,), F32)]).reshape(1, LANE)
    tm = 2048
    return pl.pallas_call(
        _rope_table_kernel,
        grid=(n // tm,),
        in_specs=[pl.BlockSpec((tm, 1), lambda i: (i, 0)),
                  pl.BlockSpec((1, LANE), lambda i: (0, 0))],
        out_specs=[pl.BlockSpec((tm, LANE), lambda i: (i, 0)),
                   pl.BlockSpec((tm, LANE), lambda i: (i, 0))],
        out_shape=[jax.ShapeDtypeStruct((n, LANE), F32)] * 2,
        compiler_params=_cparams(("arbitrary",)),
        name="rope_tables",
    )(pos_col, inv_row)


def _norm_proj_kernel(x_ref, g_ref, sc_ref, sh_ref, w_ref, o_ref, h_ref):
    @pl.when(pl.program_id(1) == 0)
    def _():
        h = _rms(x_ref[...], g_ref[...]) * (1.0 + sc_ref[0]) + sh_ref[0]
        h_ref[...] = h.astype(BF16)

    o_ref[...] = jnp.dot(h_ref[...], w_ref[...], preferred_element_type=F32).astype(o_ref.dtype)


def _norm_proj(x2, g, sc, sh, w, seq):
    n, d = x2.shape
    ncol = w.shape[1]
    tm = 1024
    tn = 13 * LANE
    per_b = seq // tm
    return pl.pallas_call(
        _norm_proj_kernel,
        grid=(n // tm, ncol // tn),
        in_specs=[
            pl.BlockSpec((tm, d), lambda i, j: (i, 0)),
            pl.BlockSpec((1, d), lambda i, j: (0, 0)),
            pl.BlockSpec((1, 1, d), lambda i, j: (i // per_b, 0, 0)),
            pl.BlockSpec((1, 1, d), lambda i, j: (i // per_b, 0, 0)),
            pl.BlockSpec((d, tn), lambda i, j: (0, j)),
        ],
        out_specs=pl.BlockSpec((tm, tn), lambda i, j: (i, j)),
        out_shape=jax.ShapeDtypeStruct((n, ncol), BF16),
        scratch_shapes=[pltpu.VMEM((tm, d), BF16)],
        compiler_params=_cparams(("arbitrary", "arbitrary")),
        name="norm_proj",
    )(x2, g, sc, sh, w)


def _mla_q_kernel(cq_ref, g_ref, w_ref, c_ref, s_ref, q_ref):
    h = _rms(cq_ref[...].astype(F32), g_ref[...]).astype(BF16)
    r = jnp.dot(h, w_ref[...], preferred_element_type=F32)
    scale = B_QK ** -0.5 * LOG2E
    c = c_ref[...]
    s = s_ref[...]
    nw = B_HEADS * LANE
    for hd in range(B_HEADS):
        nope = r[:, hd * LANE:(hd + 1) * LANE]
        ra = r[:, nw + hd * LANE: nw + (hd + 1) * LANE]
        rb = r[:, 2 * nw + hd * LANE: 2 * nw + (hd + 1) * LANE]
        q_ref[:, 2 * hd * LANE:(2 * hd + 1) * LANE] = (nope * scale).astype(BF16)
        q_ref[:, (2 * hd + 1) * LANE:(2 * hd + 2) * LANE] = ((ra * c + rb * s) * scale).astype(BF16)


def _mla_q(proj, g, w, ctab, stab):
    n = proj.shape[0]
    tm = 512
    return pl.pallas_call(
        _mla_q_kernel,
        grid=(n // tm,),
        in_specs=[
            pl.BlockSpec((tm, Q_LORA), lambda i: (i, COL_CQ * LANE // Q_LORA)),
            pl.BlockSpec((1, Q_LORA), lambda i: (0, 0)),
            pl.BlockSpec(w.shape, lambda i: (0, 0)),
            pl.BlockSpec((tm, LANE), lambda i: (i, 0)),
            pl.BlockSpec((tm, LANE), lambda i: (i, 0)),
        ],
        out_specs=pl.BlockSpec((tm, 2 * B_HEADS * LANE), lambda i: (i, 0)),
        out_shape=jax.ShapeDtypeStruct((n, 2 * B_HEADS * LANE), BF16),
        compiler_params=_cparams(("arbitrary",)),
        name="mla_q",
    )(proj, g, w, ctab, stab)


def _mla_kv_kernel(ckv_ref, kr_ref, g_ref, w_ref, c_ref, s_ref, k_ref, v_ref):
    h = _rms(ckv_ref[...].astype(F32), g_ref[...]).astype(BF16)
    r = jnp.dot(h, w_ref[...], preferred_element_type=F32)
    kr = kr_ref[...].astype(F32)
    krope = (kr[:, :LANE] * c_ref[...] + kr[:, LANE:] * s_ref[...]).astype(BF16)
    nw = B_HEADS * LANE
    for hd in range(B_HEADS):
        k_ref[:, 2 * hd * LANE:(2 * hd + 1) * LANE] = r[:, hd * LANE:(hd + 1) * LANE].astype(BF16)
        k_ref[:, (2 * hd + 1) * LANE:(2 * hd + 2) * LANE] = krope
    v_ref[...] = r[:, nw:].astype(BF16)


def _mla_kv(proj, g, w, ctab, stab):
    n = proj.shape[0]
    tm = 512
    return pl.pallas_call(
        _mla_kv_kernel,
        grid=(n // tm,),
        in_specs=[
            pl.BlockSpec((tm, KV_LORA), lambda i: (i, COL_CKV * LANE // KV_LORA)),
            pl.BlockSpec((tm, 2 * LANE), lambda i: (i, COL_KR // 2)),
            pl.BlockSpec((1, KV_LORA), lambda i: (0, 0)),
            pl.BlockSpec(w.shape, lambda i: (0, 0)),
            pl.BlockSpec((tm, LANE), lambda i: (i, 0)),
            pl.BlockSpec((tm, LANE), lambda i: (i, 0)),
        ],
        out_specs=[pl.BlockSpec((tm, 2 * B_HEADS * LANE), lambda i: (i, 0)),
                   pl.BlockSpec((tm, B_HEADS * LANE), lambda i: (i, 0))],
        out_shape=[jax.ShapeDtypeStruct((n, 2 * B_HEADS * LANE), BF16),
                   jax.ShapeDtypeStruct((n, B_HEADS * LANE), BF16)],
        compiler_params=_cparams(("arbitrary",)),
        name="mla_kv",
    )(proj, proj, g, w, ctab, stab)


MLA_SUB = 512


def _mla_attn_kernel(q_ref, k_ref, v_ref, o_ref, vx_ref):
    @pl.when(pl.program_id(2) == 0)
    def _():
        vx_ref[:, :LANE] = v_ref[...]
        vx_ref[:, LANE:] = jnp.ones((vx_ref.shape[0], LANE), BF16)

    for j in range(q_ref.shape[0] // MLA_SUB):
        rows = slice(j * MLA_SUB, (j + 1) * MLA_SUB)
        s = lax.dot_general(q_ref[rows, :], k_ref[...], (((1,), (1,)), ((), ())),
                            preferred_element_type=F32)
        m = jnp.max(s, axis=-1, keepdims=True)
        p = jnp.exp2(s - m).astype(BF16)
        ox = jnp.dot(p, vx_ref[...], preferred_element_type=F32)
        o_ref[rows, :] = (ox[:, :LANE] / ox[:, LANE:LANE + 1]).astype(o_ref.dtype)


def _mla_attn(q, k, v, bsz, seq):
    n = q.shape[0]
    tq = 2048
    nq = seq // tq
    return pl.pallas_call(
        _mla_attn_kernel,
        grid=(bsz, B_HEADS, nq),
        in_specs=[
            pl.BlockSpec((tq, 2 * LANE), lambda b, h, i: (b * nq + i, h)),
            pl.BlockSpec((seq, 2 * LANE), lambda b, h, i: (b, h)),
            pl.BlockSpec((seq, LANE), lambda b, h, i: (b, h)),
        ],
        out_specs=pl.BlockSpec((tq, LANE), lambda b, h, i: (b * nq + i, h)),
        out_shape=jax.ShapeDtypeStruct((n, B_HEADS * LANE), BF16),
        scratch_shapes=[pltpu.VMEM((seq, 2 * LANE), BF16)],
        compiler_params=_cparams(("arbitrary", "arbitrary", "arbitrary")),
        name="mla_attn",
    )(q, k, v)


def _dilated_kernel(*refs, seq):
    qkv = refs[:3 * A_NG]
    posc_ref = refs[3 * A_NG]
    posr = refs[3 * A_NG + 1: 3 * A_NG + 1 + A_NG]
    sl_ref = refs[3 * A_NG + 1 + A_NG]
    o_ref = refs[3 * A_NG + 2 + A_NG]
    qf, kf, vf, og, lg = refs[3 * A_NG + 3 + A_NG:]
    scale = HEAD_DIM ** -0.5

    for g, (win, dil) in enumerate(A_GROUPS):
        radius = win // (2 * dil)
        length = seq // dil
        kw = min(2 * QBLK, length)
        q_in, k_in, v_in = qkv[3 * g], qkv[3 * g + 1], qkv[3 * g + 2]
        slope2 = sl_ref[0, g:g + 1, 0:1] * LOG2E
        if dil > 1:
            qf[...] = q_in[0].astype(F32)
            kf[...] = k_in[0].astype(F32)
            vf[...] = v_in[0].astype(F32)

        def rows(start, size, dil=dil):
            if dil == 1:
                return pl.ds(start, size)
            return pl.ds(start, size, stride=dil)

        def load(ref_in, ref_f32, rws, dil=dil):
            if dil == 1:
                return ref_in[0, rws, :]
            return ref_f32[rws, :].astype(BF16)

        units = []
        for r in range(dil):
            for ib in range(length // QBLK):
                i0 = ib * QBLK
                ws = min(max(i0 - radius, 0), length - kw)
                units.append((r, i0, ws, rows(r + dil * i0, QBLK), rows(r + dil * ws, kw)))
        qs = [(load(q_in, qf, u[3]).astype(F32) * (scale * LOG2E)).astype(BF16) for u in units]
        ks = [load(k_in, kf, u[4]) for u in units]
        ss = [lax.dot_general(q, k, (((1,), (1,)), ((), ())), preferred_element_type=F32)
              for q, k in zip(qs, ks)]
        biased = []
        for (r, i0, ws, q_rows, _), s in zip(units, ss):
            pq = posc_ref[0, q_rows, :]
            pk = posr[g][0, r:r + 1, ws:ws + kw]
            dist = jnp.abs(pq - pk).astype(F32)
            ii = i0 + lax.broadcasted_iota(I32, (QBLK, kw), 0)
            jj = ws + lax.broadcasted_iota(I32, (QBLK, kw), 1)
            biased.append(jnp.where(jnp.abs(ii - jj) <= radius, s - slope2 * dist, NEG))
        ms = [jnp.max(s, axis=-1, keepdims=True) for s in biased]
        ps = [jnp.exp2(s - m) for s, m in zip(biased, ms)]
        dens = [jnp.sum(p, axis=-1, keepdims=True) for p in ps]
        vs = [load(v_in, vf, u[4]) for u in units]
        outs = [jnp.dot(p.astype(BF16), v, preferred_element_type=F32) / den
                for p, v, den in zip(ps, vs, dens)]
        for u, o, m, den in zip(units, outs, ms, dens):
            og[g, u[3], :] = o
            lg[g, u[3], :] = jnp.broadcast_to(m * (1.0 / LOG2E) + jnp.log(den), (QBLK, HEAD_DIM))

    chunk = 256

    def merge(ci, carry):
        rs = pl.ds(pl.multiple_of(ci * chunk, chunk), chunk)
        l0, l1, l2 = lg[0, rs, :], lg[1, rs, :], lg[2, rs, :]
        m = jnp.maximum(jnp.maximum(l0, l1), l2)
        w0, w1, w2 = jnp.exp(l0 - m), jnp.exp(l1 - m), jnp.exp(l2 - m)
        o = (w0 * og[0, rs, :] + w1 * og[1, rs, :] + w2 * og[2, rs, :]) / (w0 + w1 + w2)
        o_ref[0, rs, :] = o.astype(o_ref.dtype)
        return carry

    lax.fori_loop(0, seq // chunk, merge, 0)


def _dilated_attn(proj3, posc, posr, slopes):
    bsz, seq, _ = proj3.shape
    in_specs = []
    args = []
    for g in range(A_NG):
        for base in (COL_QA, COL_KA, COL_VA):
            in_specs.append(pl.BlockSpec(
                (1, seq, LANE), lambda b, h, base=base, g=g: (b, 0, base + A_HPG * g + h)))
            args.append(proj3)
    in_specs.append(pl.BlockSpec((1, seq, 1), lambda b, h: (b, 0, 0)))
    args.append(posc)
    for g, (_, dil) in enumerate(A_GROUPS):
        in_specs.append(pl.BlockSpec((1, dil, seq // dil), lambda b, h: (b, 0, 0)))
        args.append(posr[g])
    in_specs.append(pl.BlockSpec((1, 8, LANE), lambda b, h: (h, 0, 0)))
    args.append(slopes)
    return pl.pallas_call(
        functools.partial(_dilated_kernel, seq=seq),
        grid=(bsz, A_HPG),
        in_specs=in_specs,
        out_specs=pl.BlockSpec((1, seq, LANE), lambda b, h: (b, 0, h)),
        out_shape=jax.ShapeDtypeStruct((bsz, seq, A_OUT), BF16),
        scratch_shapes=[pltpu.VMEM((seq, HEAD_DIM), F32)] * 3
        + [pltpu.VMEM((A_NG, seq, HEAD_DIM), F32)] * 2,
        compiler_params=_cparams(("arbitrary", "arbitrary")),
        name="dilated_attn",
    )(*args)


def _merge_out_kernel(ya_ref, yb_ref, ga_ref, gb_ref, x_ref, wa_ref, wb_ref, wo_ref, gt_ref,
                      g2_ref, sc_ref, sh_ref, wr_ref, br_ref, xo_ref, h_ref, lg_ref):
    a = jnp.dot(ya_ref[...], wa_ref[...], preferred_element_type=F32)
    b = jnp.dot(yb_ref[...], wb_ref[...], preferred_element_type=F32)
    ga = jax.nn.sigmoid(ga_ref[...].astype(F32))
    gb = jax.nn.sigmoid(gb_ref[...].astype(F32))
    merged = (ga * a + gb * b).astype(BF16)
    xn = x_ref[...] + (1.0 + gt_ref[0]) * jnp.dot(merged, wo_ref[...], preferred_element_type=F32)
    xo_ref[...] = xn
    h = _rms(xn, g2_ref[...]) * (1.0 + sc_ref[0]) + sh_ref[0]
    _pack_store(h, h_ref)
    h_hi = h.astype(BF16)
    h_lo = (h - h_hi.astype(F32)).astype(BF16)
    t = jnp.dot(h_hi, wr_ref[...], preferred_element_type=F32)
    u = jnp.dot(h_lo, wr_ref[:, :LANE], preferred_element_type=F32)
    lg_ref[...] = t[:, :LANE] + t[:, LANE:] + u + br_ref[...]


def _merge_out(ya, yb, proj, x2, wa, wb, wo, gt, g2, sc, sh, wr, br, seq):
    n, d = x2.shape
    tm = 256
    per_b = seq // tm
    const = lambda i: (0, 0)
    bidx = lambda i: (i // per_b, 0, 0)
    return pl.pallas_call(
        _merge_out_kernel,
        grid=(n // tm,),
        in_specs=[
            pl.BlockSpec((tm, A_OUT), lambda i: (i, 0)),
            pl.BlockSpec((tm, B_OUT), lambda i: (i, 0)),
            pl.BlockSpec((tm, d), lambda i: (i, COL_GA * LANE // d)),
            pl.BlockSpec((tm, d), lambda i: (i, COL_GB * LANE // d)),
            pl.BlockSpec((tm, d), lambda i: (i, 0)),
            pl.BlockSpec(wa.shape, const),
            pl.BlockSpec(wb.shape, const),
            pl.BlockSpec(wo.shape, const),
            pl.BlockSpec((1, 1, d), bidx),
            pl.BlockSpec((1, d), const),
            pl.BlockSpec((1, 1, d), bidx),
            pl.BlockSpec((1, 1, d), bidx),
            pl.BlockSpec(wr.shape, const),
            pl.BlockSpec((1, LANE), const),
        ],
        out_specs=[pl.BlockSpec((tm, d), lambda i: (i, 0)),
                   pl.BlockSpec((tm * PACK_ROWS, LANE), lambda i: (i, 0)),
                   pl.BlockSpec((tm, LANE), lambda i: (i, 0))],
        out_shape=[jax.ShapeDtypeStruct((n, d), F32),
                   jax.ShapeDtypeStruct((n * PACK_ROWS, LANE), U32),
                   jax.ShapeDtypeStruct((n, LANE), F32)],
        compiler_params=_cparams(("arbitrary",)),
        name="merge_out",
    )(ya, yb, proj, proj, x2, wa, wb, wo, gt, g2, sc, sh, wr, br)


def _route_kernel(lg_ref, rt_ref, rw_ref, cnt_ref, carry_ref):
    @pl.when(pl.program_id(0) == 0)
    def _():
        carry_ref[...] = jnp.zeros_like(carry_ref)

    lg = lg_ref[...]
    tm = lg.shape[0]
    lane = lax.broadcasted_iota(I32, lg.shape, 1).astype(F32)
    big = float(2 * LANE)
    is_grp = lane < N_GROUPS
    gl = jnp.where(is_grp, lg, NEG)
    gm = jnp.max(gl, axis=-1, keepdims=True)
    ge = jnp.where(is_grp, jnp.exp(gl - gm), 0.0)
    pg = ge / jnp.sum(ge, axis=-1, keepdims=True)
    p_top = jnp.max(pg, axis=-1, keepdims=True)
    g_idx = jnp.min(jnp.where(is_grp & (pg == p_top), lane, big), axis=-1, keepdims=True)

    lo = N_GROUPS + EPG * g_idx
    in_grp = (lane >= lo) & (lane < lo + EPG)
    el = jnp.where(in_grp, lg, NEG)
    em = jnp.max(el, axis=-1, keepdims=True)
    ee = jnp.where(in_grp, jnp.exp(el - em), 0.0)
    pin = ee / jnp.sum(ee, axis=-1, keepdims=True)
    v1 = jnp.max(pin, axis=-1, keepdims=True)
    i1 = jnp.min(jnp.where(in_grp & (pin == v1), lane, big), axis=-1, keepdims=True)
    rest = in_grp & (lane != i1)
    p2 = jnp.where(rest, pin, -1.0)
    v2 = jnp.max(p2, axis=-1, keepdims=True)
    i2 = jnp.min(jnp.where(rest & (p2 == v2), lane, big), axis=-1, keepdims=True)
    vs = v1 + v2
    w1 = p_top * v1 / vs
    w2 = p_top * v2 / vs

    oh1 = lane == i1
    oh2 = lane == i2
    mm = (oh1 | oh2).astype(BF16)
    tri = (lax.broadcasted_iota(I32, (tm, tm), 0) > lax.broadcasted_iota(I32, (tm, tm), 1))
    cnt = jnp.dot(tri.astype(BF16), mm, preferred_element_type=F32) + carry_ref[...]
    r1 = jnp.sum(jnp.where(oh1, cnt, 0.0), axis=-1, keepdims=True)
    r2 = jnp.sum(jnp.where(oh2, cnt, 0.0), axis=-1, keepdims=True)
    carry_ref[...] += jnp.sum(mm.astype(F32), axis=0, keepdims=True)
    cnt_ref[...] = carry_ref[...]

    e1 = i1 - N_GROUPS
    e2 = i2 - N_GROUPS
    packed = jnp.where(lane == 0, e1, jnp.where(lane == 1, e2, jnp.where(
        lane == 2, r1, jnp.where(lane == 3, r2, jnp.where(lane == 4, w1, w2)))))
    rt_ref[...] = packed.T[:8, :]
    rw_ref[...] = jnp.where(lane == 0, w1, w2)


def _route(logits):
    n = logits.shape[0]
    tm = 512
    return pl.pallas_call(
        _route_kernel,
        grid=(n // tm,),
        in_specs=[pl.BlockSpec((tm, LANE), lambda i: (i, 0))],
        out_specs=[pl.BlockSpec((8, tm), lambda i: (0, i)),
                   pl.BlockSpec((tm, LANE), lambda i: (i, 0)),
                   pl.BlockSpec((1, LANE), lambda i: (0, 0))],
        out_shape=[jax.ShapeDtypeStruct((8, n), F32),
                   jax.ShapeDtypeStruct((n, LANE), F32),
                   jax.ShapeDtypeStruct((1, LANE), F32)],
        scratch_shapes=[pltpu.VMEM((1, LANE), F32)],
        compiler_params=_cparams(("arbitrary",)),
        name="route",
    )(logits)


PACK_ROWS = 8
HI_MASK = 0xFFFF0000


def _pack_store(val, ref):
    rows = val.shape[0]
    for c in range(PACK_ROWS):
        lo = val[:, 2 * c * LANE:(2 * c + 1) * LANE].astype(BF16).astype(F32)
        hi = val[:, (2 * c + 1) * LANE:(2 * c + 2) * LANE].astype(BF16).astype(F32)
        word = (pltpu.bitcast(hi, U32) & jnp.uint32(HI_MASK)) | (pltpu.bitcast(lo, U32) >> 16)
        ref[pl.ds(c, rows, stride=PACK_ROWS), :] = word


def _unpack_load(load_chunk):
    pieces = []
    for c in range(PACK_ROWS):
        w = load_chunk(c)
        pieces.append(pltpu.bitcast(w << 16, F32))
        pieces.append(pltpu.bitcast(w & jnp.uint32(HI_MASK), F32))
    return pieces


def _start_rows(src_ref, dst_ref, idx_ref, base, count, sem):
    for j in range(count):
        src_row = pl.multiple_of(idx_ref[0, 0, base + j] * PACK_ROWS, PACK_ROWS)
        pltpu.make_async_copy(src_ref.at[pl.ds(src_row, PACK_ROWS)],
                              dst_ref.at[pl.ds(j * PACK_ROWS, PACK_ROWS)], sem).start()


def _wait_rows(src_ref, dst_ref, sem):
    pltpu.make_async_copy(src_ref.at[pl.ds(0, dst_ref.shape[0])], dst_ref, sem).wait()


def _expert_kernel(be_ref, nx_ref, nu_ref, tok0_ref, tokn_ref, h_ref, wgu_ref, wd_ref, o_ref,
                   xbuf0, xbuf1, wgu_f, wd_f, wgu_s, wd_s, sem, wsem, *, layer):
    i = pl.program_id(0)
    n_used = nu_ref[0]
    tb = xbuf0.shape[0] // PACK_ROWS

    def weight_copies(e):
        return (pltpu.make_async_copy(wgu_ref.at[layer, e], wgu_f, wsem.at[0]),
                pltpu.make_async_copy(wd_ref.at[layer, e], wd_f, wsem.at[1]))

    @pl.when(i == 0)
    def _():
        for cp in weight_copies(be_ref[0]):
            cp.start(priority=1)
        _start_rows(h_ref, xbuf0, tok0_ref, 0, tb, sem.at[0])

    @pl.when(i < n_used)
    def _():
        e = be_ref[i]
        prev = be_ref[jnp.maximum(i - 1, 0)]

        @pl.when((i == 0) | (e != prev))
        def _():
            for cp in weight_copies(e):
                cp.wait()
            wgu_s[...] = wgu_f[...].astype(BF16)
            wd_s[...] = wd_f[...].astype(BF16)

            @pl.when(nx_ref[i] >= 0)
            def _():
                for cp in weight_copies(nx_ref[i]):
                    cp.start(priority=1)

        def block(cur_buf, cur_sem, nxt_buf, nxt_sem):
            _wait_rows(h_ref, cur_buf, cur_sem)
            _start_rows(h_ref, nxt_buf, tokn_ref, 0, tb, nxt_sem)
            pieces = _unpack_load(lambda c: cur_buf[pl.ds(c, tb, stride=PACK_ROWS), :])
            x = jnp.concatenate([p.astype(BF16) for p in pieces], axis=-1)
            gu = jnp.dot(x, wgu_s[...], preferred_element_type=F32)
            gate = gu[:, :D_EXPERT]
            up = gu[:, D_EXPERT:]
            act = (gate * jax.nn.sigmoid(gate) * up).astype(BF16)
            _pack_store(jnp.dot(act, wd_s[...], preferred_element_type=F32), o_ref)

            @pl.when(i + 1 >= n_used)
            def _():
                _wait_rows(h_ref, nxt_buf, nxt_sem)

        @pl.when(i % 2 == 0)
        def _():
            block(xbuf0, sem.at[0], xbuf1, sem.at[1])

        @pl.when(i % 2 == 1)
        def _():
            block(xbuf1, sem.at[1], xbuf0, sem.at[0])

    @pl.when(i >= n_used)
    def _():
        o_ref[...] = jnp.zeros_like(o_ref)


def _experts(block_e, next_e, n_used, slot_tok3, h_packed, w_gu, w_down, layer):
    d = w_gu.shape[2]
    nb = slot_tok3.shape[0]
    tile_rows = EXPERT_TILE * PACK_ROWS

    def nxt_blk(i, be, nx, nu):
        return (jnp.minimum(i + 1, nu[0] - 1), 0, 0)

    grid_spec = pltpu.PrefetchScalarGridSpec(
        num_scalar_prefetch=3,
        grid=(nb,),
        in_specs=[
            pl.BlockSpec((1, 1, EXPERT_TILE), lambda i, be, nx, nu: (0, 0, 0),
                         memory_space=pltpu.SMEM),
            pl.BlockSpec((1, 1, EXPERT_TILE), nxt_blk, memory_space=pltpu.SMEM),
            pl.BlockSpec(memory_space=pl.ANY),
            pl.BlockSpec(memory_space=pl.ANY),
            pl.BlockSpec(memory_space=pl.ANY),
        ],
        out_specs=pl.BlockSpec((tile_rows, LANE), lambda i, be, nx, nu: (i, 0)),
        scratch_shapes=[pltpu.VMEM((tile_rows, LANE), U32), pltpu.VMEM((tile_rows, LANE), U32),
                        pltpu.VMEM((d, 2 * D_EXPERT), F32), pltpu.VMEM((D_EXPERT, d), F32),
                        pltpu.VMEM((d, 2 * D_EXPERT), BF16), pltpu.VMEM((D_EXPERT, d), BF16),
                        pltpu.SemaphoreType.DMA((2,)), pltpu.SemaphoreType.DMA((2,))],
    )
    return pl.pallas_call(
        functools.partial(_expert_kernel, layer=layer),
        grid_spec=grid_spec,
        out_shape=jax.ShapeDtypeStruct((nb * tile_rows, LANE), U32),
        compiler_params=_cparams(("arbitrary",)),
        name="experts",
    )(block_e, next_e, n_used, slot_tok3, slot_tok3, h_packed, w_gu, w_down)


def _combine_kernel(dest0_ref, destn_ref, ys_ref, x_ref, rw_ref, gt_ref, fg_ref, o_ref, ybuf, sem,
                    *, final):
    i = pl.program_id(0)
    tm = x_ref.shape[0]

    def start(dest_ref, slot):
        for k in range(TOP_K):
            _start_rows(ys_ref, ybuf.at[slot, k], dest_ref, k * tm, tm, sem.at[slot, k])

    @pl.when(i == 0)
    def _():
        start(dest0_ref, 0)

    @pl.when(i + 1 < pl.num_programs(0))
    def _():
        start(destn_ref, (i + 1) % 2)

    cur = i % 2
    for k in range(TOP_K):
        _wait_rows(ys_ref, ybuf.at[cur, k], sem.at[cur, k])
    rw = rw_ref[...]
    y0 = _unpack_load(lambda c: ybuf[cur, 0, pl.ds(c, tm, stride=PACK_ROWS), :])
    y1 = _unpack_load(lambda c: ybuf[cur, 1, pl.ds(c, tm, stride=PACK_ROWS), :])
    y = jnp.concatenate([rw[:, 0:1] * a + rw[:, 1:2] * b for a, b in zip(y0, y1)], axis=-1)
    xn = x_ref[...] + (1.0 + gt_ref[0]) * y
    if final:
        xn = _rms(xn, fg_ref[...])
    o_ref[...] = xn


def _combine(dest3, ys, x2, rw, gt, fg, seq, final):
    n, d = x2.shape
    tm = dest3.shape[2] // TOP_K
    per_b = seq // tm
    nt = n // tm
    return pl.pallas_call(
        functools.partial(_combine_kernel, final=final),
        grid=(nt,),
        in_specs=[pl.BlockSpec((1, 1, TOP_K * tm), lambda i: (0, 0, 0), memory_space=pltpu.SMEM),
                  pl.BlockSpec((1, 1, TOP_K * tm), lambda i: (jnp.minimum(i + 1, nt - 1), 0, 0),
                               memory_space=pltpu.SMEM),
                  pl.BlockSpec(memory_space=pl.ANY),
                  pl.BlockSpec((tm, d), lambda i: (i, 0)),
                  pl.BlockSpec((tm, LANE), lambda i: (i, 0)),
                  pl.BlockSpec((1, 1, d), lambda i: (i // per_b, 0, 0)),
                  pl.BlockSpec((1, d), lambda i: (0, 0))],
        out_specs=pl.BlockSpec((tm, d), lambda i: (i, 0)),
        out_shape=jax.ShapeDtypeStruct((n, d), F32),
        scratch_shapes=[pltpu.VMEM((2, TOP_K, tm * PACK_ROWS, LANE), U32),
                        pltpu.SemaphoreType.DMA((2, TOP_K))],
        compiler_params=_cparams(("arbitrary",)),
        name="combine",
    )(dest3, dest3, ys, x2, rw, gt, fg)


def _rope_cols(w3):
    half = B_ROPE // 2
    x1, x2 = w3[..., :half], w3[..., half:]
    z = jnp.zeros(w3.shape[:-1] + (LANE - B_ROPE,), w3.dtype)
    a = jnp.concatenate([x1, x2, z], axis=-1)
    b = jnp.concatenate([x2, x1, z], axis=-1)
    return a, b


def _layout_w_in(w):
    d = w.shape[0]
    o = 0
    qa = w[:, o:o + A_WIDTH]; o += A_WIDTH
    ka = w[:, o:o + A_WIDTH]; o += A_WIDTH
    va = w[:, o:o + A_WIDTH]; o += A_WIDTH
    cq = w[:, o:o + Q_LORA]; o += Q_LORA
    ckv = w[:, o:o + KV_LORA]; o += KV_LORA
    kr = w[:, o:o + B_ROPE]; o += B_ROPE
    gates = w[:, o:]
    kra, krb = _rope_cols(kr.reshape(d, 1, B_ROPE))
    return jnp.concatenate([gates, qa, ka, va, cq, ckv, kra.reshape(d, LANE),
                            krb.reshape(d, LANE)], axis=1).astype(BF16)


def _layout_w_uq(w):
    k = w.shape[0]
    w3 = w.reshape(k, B_HEADS, B_QK)
    nope = w3[:, :, :B_NOPE].reshape(k, B_HEADS * LANE)
    a, b = _rope_cols(w3[:, :, B_NOPE:])
    return jnp.concatenate([nope, a.reshape(k, -1), b.reshape(k, -1)], axis=1).astype(BF16)


def _layout_w_ukv(w):
    k = w.shape[0]
    w3 = w.reshape(k, B_HEADS, B_NOPE + B_V)
    return jnp.concatenate([w3[:, :, :B_NOPE].reshape(k, -1),
                            w3[:, :, B_NOPE:].reshape(k, -1)], axis=1).astype(BF16)


def kernel(x, c, positions, ln1_g, ln2_g, w_ada, b_ada, w_in, q_norm_g, w_uq, kv_norm_g, w_ukv,
           w_a_up, w_b_up, w_o, w_grp, b_grp, w_exp, b_exp, w_gu, w_down, final_g):
    bsz, seq, d = x.shape
    depth = w_in.shape[0]
    n = bsz * seq
    n_blocks = n * TOP_K // EXPERT_TILE + N_EXPERTS
    n_slots = n_blocks * EXPERT_TILE
    tile_tok = 256

    mod = _ada(c, w_ada, b_ada).reshape(depth, bsz, N_MOD, 1, d)

    pos_col = positions.reshape(n, 1)
    ctab, stab = _rope_tables(pos_col)
    posc = positions.reshape(bsz, seq, 1)
    posr = [positions.reshape(bsz, seq // dil, dil).transpose(0, 2, 1) for _, dil in A_GROUPS]
    slopes = 2.0 ** (-ALIBI_MAX_BIAS * jnp.arange(1, A_HEADS + 1, dtype=F32) / A_HEADS)
    slopes = slopes.reshape(A_NG, A_HPG).T
    slopes = jnp.pad(slopes, ((0, 0), (0, 8 - A_NG)))
    slopes = jnp.broadcast_to(slopes[:, :, None], (A_HPG, 8, LANE))

    x2 = x.reshape(n, d)
    for l in range(depth):
        sh1, sc1, gt1, sh2, sc2, gt2 = (mod[l, :, m] for m in range(N_MOD))

        proj = _norm_proj(x2, ln1_g[l].reshape(1, d), sc1, sh1, _layout_w_in(w_in[l]), seq)
        q = _mla_q(proj, q_norm_g[l].reshape(1, Q_LORA), _layout_w_uq(w_uq[l]), ctab, stab)
        k, v = _mla_kv(proj, kv_norm_g[l].reshape(1, KV_LORA), _layout_w_ukv(w_ukv[l]), ctab, stab)
        yb = _mla_attn(q, k, v, bsz, seq)
        ya = _dilated_attn(proj.reshape(bsz, seq, PROJ_COLS), posc, posr, slopes).reshape(n, A_OUT)

        w_r = jnp.concatenate([w_grp[l], w_exp[l],
                               jnp.zeros((d, LANE - N_GROUPS - N_EXPERTS), F32)], axis=1)
        w_r_hi = w_r.astype(BF16)
        w_r = jnp.concatenate([w_r_hi, (w_r - w_r_hi.astype(F32)).astype(BF16)], axis=1)
        b_r = jnp.concatenate([b_grp[l], b_exp[l],
                               jnp.zeros((LANE - N_GROUPS - N_EXPERTS,), F32)]).reshape(1, LANE)
        x2, h2, logits = _merge_out(ya, yb, proj, x2, w_a_up[l].astype(BF16),
                                    w_b_up[l].astype(BF16), w_o[l].astype(BF16), gt1,
                                    ln2_g[l].reshape(1, d), sc2, sh2, w_r, b_r, seq)

        rt, rw, cnt = _route(logits)
        counts = cnt[0, N_GROUPS:N_GROUPS + N_EXPERTS].astype(I32)
        padded = (counts + EXPERT_TILE - 1) // EXPERT_TILE * EXPERT_TILE
        pends = jnp.cumsum(padded)
        pstarts = pends - padded
        eids = rt[:TOP_K].astype(I32)
        onehot = eids[:, :, None] == jnp.arange(N_EXPERTS, dtype=I32)
        dest = jnp.sum(jnp.where(onehot, pstarts, 0), axis=-1) + rt[TOP_K:2 * TOP_K].astype(I32)
        dest3 = dest.reshape(TOP_K, n // tile_tok, tile_tok).transpose(1, 0, 2).reshape(
            n // tile_tok, 1, TOP_K * tile_tok)
        n_used = (pends[-1:] // EXPERT_TILE).astype(I32)
        blk_start = jnp.arange(n_blocks, dtype=I32) * EXPERT_TILE
        block_e = jnp.minimum(jnp.sum((pends[None, :] <= blk_start[:, None]).astype(I32), axis=1),
                              N_EXPERTS - 1)

        tok = jnp.broadcast_to(jnp.arange(n, dtype=I32), (TOP_K, n))
        slot_tok = jnp.zeros((n_slots,), I32).at[dest.reshape(-1)].set(
            tok.reshape(-1), unique_indices=True)
        eidx = jnp.arange(N_EXPERTS, dtype=I32)
        nxt_used = lax.cummin(jnp.where(counts > 0, eidx, N_EXPERTS), reverse=True)
        nxt_after = jnp.concatenate([nxt_used[1:], jnp.full((1,), N_EXPERTS, I32)])
        nxt_after = jnp.where(nxt_after >= N_EXPERTS, -1, nxt_after)
        next_e = jnp.sum(jnp.where(block_e[:, None] == eidx[None, :], nxt_after[None, :], 0),
                         axis=1).astype(I32)
        ys = _experts(block_e, next_e, n_used, slot_tok.reshape(n_blocks, 1, EXPERT_TILE), h2,
                      w_gu, w_down, l)
        x2 = _combine(dest3, ys, x2, rw, gt2, final_g.reshape(1, d), seq, final=(l == depth - 1))
    return x2.reshape(bsz, seq, d)
```

```python
import functools

import jax
import jax.numpy as jnp
from jax import lax
from jax.experimental import pallas as pl
from jax.experimental.pallas import tpu as pltpu

F32 = jnp.float32
BF16 = jnp.bfloat16
I32 = jnp.int32
U32 = jnp.uint32
HIGHEST = lax.Precision.HIGHEST

HEAD_DIM = 128
A_GROUPS = ((128, 1), (512, 4), (2048, 16))
A_HPG = 4
A_NG = len(A_GROUPS)
A_HEADS = A_HPG * A_NG
A_WIDTH = A_HEADS * HEAD_DIM
A_OUT = A_HPG * HEAD_DIM
ALIBI_MAX_BIAS = 8.0
B_HEADS = 8
B_NOPE = 128
B_ROPE = 64
B_V = 128
B_QK = B_NOPE + B_ROPE
Q_LORA = 512
KV_LORA = 512
B_OUT = B_HEADS * B_V
ROPE_THETA = 10000.0
N_GROUPS = 8
EPG = 8
N_EXPERTS = N_GROUPS * EPG
TOP_K = 2
D_EXPERT = 512
N_MOD = 6
EPS = 1e-6
NEG = -1e30
LOG2E = 1.4426950408889634

LANE = 128
VMEM_LIMIT = 56 * 1024 * 1024
QBLK = 128
EXPERT_TILE = 512

COL_GA = 0
COL_GB = 16
COL_QA = 32
COL_KA = 44
COL_VA = 56
COL_CQ = 68
COL_CKV = 72
COL_KR = 76
N_COLBLK = 78
PROJ_COLS = N_COLBLK * LANE


def _cparams(sem):
    return pltpu.CompilerParams(dimension_semantics=sem, vmem_limit_bytes=VMEM_LIMIT)


def _rms(x, g):
    return x * lax.rsqrt(jnp.mean(x * x, axis=-1, keepdims=True) + EPS) * g


def _ada_kernel(c_ref, w_ref, b_ref, o_ref):
    c = c_ref[...]
    cs = c * jax.nn.sigmoid(c)
    o_ref[0] = jnp.dot(cs, w_ref[0], precision=HIGHEST, preferred_element_type=F32) + b_ref[0]


def _ada(c, w_ada, b_ada):
    nl, d, nm = w_ada.shape
    bsz = c.shape[0]
    tn = 1536
    return pl.pallas_call(
        _ada_kernel,
        grid=(nl, nm // tn),
        in_specs=[
            pl.BlockSpec((bsz, d), lambda l, j: (0, 0)),
            pl.BlockSpec((1, d, tn), lambda l, j: (l, 0, j)),
            pl.BlockSpec((1, 1, tn), lambda l, j: (l, 0, j)),
        ],
        out_specs=pl.BlockSpec((1, bsz, tn), lambda l, j: (l, 0, j)),
        out_shape=jax.ShapeDtypeStruct((nl, bsz, nm), F32),
        compiler_params=_cparams(("arbitrary", "arbitrary")),
        name="ada",
    )(c, w_ada, b_ada.reshape(nl, 1, nm))


def _rope_table_kernel(pos_ref, inv_ref, c_ref, s_ref):
    ang = pos_ref[...].astype(F32) * inv_ref[...]
    lane = lax.broadcasted_iota(I32, ang.shape, 1)
    half = B_ROPE // 2
    cs = jnp.cos(ang)
    sn = jnp.sin(ang)
    c_ref[...] = jnp.where(lane < B_ROPE, cs, 0.0)
    s_ref[...] = jnp.where(lane < half, -sn, jnp.where(lane < B_ROPE, sn, 0.0))


def _rope_tables(pos_col):
    n = pos_col.shape[0]
    half = B_ROPE // 2
    inv = ROPE_THETA ** (-jnp.arange(half, dtype=F32) / half)
    inv_row = jnp.concatenate([inv, inv, jnp.zeros((LANE - B_ROPE,), F32)]).reshape(1, LANE)
    tm = 2048
    return pl.pallas_call(
        _rope_table_kernel,
        grid=(n // tm,),
        in_specs=[pl.BlockSpec((tm, 1), lambda i: (i, 0)),
                  pl.BlockSpec((1, LANE), lambda i: (0, 0))],
        out_specs=[pl.BlockSpec((tm, LANE), lambda i: (i, 0)),
                   pl.BlockSpec((tm, LANE), lambda i: (i, 0))],
        out_shape=[jax.ShapeDtypeStruct((n, LANE), F32)] * 2,
        compiler_params=_cparams(("arbitrary",)),
        name="rope_tables",
    )(pos_col, inv_row)


def _norm_proj_kernel(x_ref, g_ref, sc_ref, sh_ref, w_ref, o_ref, h_ref):
    @pl.when(pl.program_id(1) == 0)
    def _():
        h = _rms(x_ref[...], g_ref[...]) * (1.0 + sc_ref[0]) + sh_ref[0]
        h_ref[...] = h.astype(BF16)

    o_ref[...] = jnp.dot(h_ref[...], w_ref[...], preferred_element_type=F32).astype(o_ref.dtype)


def _norm_proj(x2, g, sc, sh, w, seq):
    n, d = x2.shape
    ncol = w.shape[1]
    tm = 1024
    tn = 13 * LANE
    per_b = seq // tm
    return pl.pallas_call(
        _norm_proj_kernel,
        grid=(n // tm, ncol // tn),
        in_specs=[
            pl.BlockSpec((tm, d), lambda i, j: (i, 0)),
            pl.BlockSpec((1, d), lambda i, j: (0, 0)),
            pl.BlockSpec((1, 1, d), lambda i, j: (i // per_b, 0, 0)),
            pl.BlockSpec((1, 1, d), lambda i, j: (i // per_b, 0, 0)),
            pl.BlockSpec((d, tn), lambda i, j: (0, j)),
        ],
        out_specs=pl.BlockSpec((tm, tn), lambda i, j: (i, j)),
        out_shape=jax.ShapeDtypeStruct((n, ncol), BF16),
        scratch_shapes=[pltpu.VMEM((tm, d), BF16)],
        compiler_params=_cparams(("arbitrary", "arbitrary")),
        name="norm_proj",
    )(x2, g, sc, sh, w)


def _mla_q_kernel(cq_ref, g_ref, w_ref, c_ref, s_ref, q_ref):
    h = _rms(cq_ref[...].astype(F32), g_ref[...]).astype(BF16)
    r = jnp.dot(h, w_ref[...], preferred_element_type=F32)
    scale = B_QK ** -0.5 * LOG2E
    c = c_ref[...]
    s = s_ref[...]
    nw = B_HEADS * LANE
    for hd in range(B_HEADS):
        nope = r[:, hd * LANE:(hd + 1) * LANE]
        ra = r[:, nw + hd * LANE: nw + (hd + 1) * LANE]
        rb = r[:, 2 * nw + hd * LANE: 2 * nw + (hd + 1) * LANE]
        q_ref[:, 2 * hd * LANE:(2 * hd + 1) * LANE] = (nope * scale).astype(BF16)
        q_ref[:, (2 * hd + 1) * LANE:(2 * hd + 2) * LANE] = ((ra * c + rb * s) * scale).astype(BF16)


def _mla_q(proj, g, w, ctab, stab):
    n = proj.shape[0]
    tm = 512
    return pl.pallas_call(
        _mla_q_kernel,
        grid=(n // tm,),
        in_specs=[
            pl.BlockSpec((tm, Q_LORA), lambda i: (i, COL_CQ * LANE // Q_LORA)),
            pl.BlockSpec((1, Q_LORA), lambda i: (0, 0)),
            pl.BlockSpec(w.shape, lambda i: (0, 0)),
            pl.BlockSpec((tm, LANE), lambda i: (i, 0)),
            pl.BlockSpec((tm, LANE), lambda i: (i, 0)),
        ],
        out_specs=pl.BlockSpec((tm, 2 * B_HEADS * LANE), lambda i: (i, 0)),
        out_shape=jax.ShapeDtypeStruct((n, 2 * B_HEADS * LANE), BF16),
        compiler_params=_cparams(("arbitrary",)),
        name="mla_q",
    )(proj, g, w, ctab, stab)


def _mla_kv_kernel(ckv_ref, kr_ref, g_ref, w_ref, c_ref, s_ref, k_ref, v_ref):
    h = _rms(ckv_ref[...].astype(F32), g_ref[...]).astype(BF16)
    r = jnp.dot(h, w_ref[...], preferred_element_type=F32)
    kr = kr_ref[...].astype(F32)
    krope = (kr[:, :LANE] * c_ref[...] + kr[:, LANE:] * s_ref[...]).astype(BF16)
    nw = B_HEADS * LANE
    for hd in range(B_HEADS):
        k_ref[:, 2 * hd * LANE:(2 * hd + 1) * LANE] = r[:, hd * LANE:(hd + 1) * LANE].astype(BF16)
        k_ref[:, (2 * hd + 1) * LANE:(2 * hd + 2) * LANE] = krope
    v_ref[...] = r[:, nw:].astype(BF16)


def _mla_kv(proj, g, w, ctab, stab):
    n = proj.shape[0]
    tm = 512
    return pl.pallas_call(
        _mla_kv_kernel,
        grid=(n // tm,),
        in_specs=[
            pl.BlockSpec((tm, KV_LORA), lambda i: (i, COL_CKV * LANE // KV_LORA)),
            pl.BlockSpec((tm, 2 * LANE), lambda i: (i, COL_KR // 2)),
            pl.BlockSpec((1, KV_LORA), lambda i: (0, 0)),
            pl.BlockSpec(w.shape, lambda i: (0, 0)),
            pl.BlockSpec((tm, LANE), lambda i: (i, 0)),
            pl.BlockSpec((tm, LANE), lambda i: (i, 0)),
        ],
        out_specs=[pl.BlockSpec((tm, 2 * B_HEADS * LANE), lambda i: (i, 0)),
                   pl.BlockSpec((tm, B_HEADS * LANE), lambda i: (i, 0))],
        out_shape=[jax.ShapeDtypeStruct((n, 2 * B_HEADS * LANE), BF16),
                   jax.ShapeDtypeStruct((n, B_HEADS * LANE), BF16)],
        compiler_params=_cparams(("arbitrary",)),
        name="mla_kv",
    )(proj, proj, g, w, ctab, stab)


MLA_SUB = 512


def _mla_attn_kernel(q_ref, k_ref, v_ref, o_ref, vx_ref):
    @pl.when(pl.program_id(2) == 0)
    def _():
        vx_ref[:, :LANE] = v_ref[...]
        vx_ref[:, LANE:] = jnp.ones((vx_ref.shape[0], LANE), BF16)

    for j in range(q_ref.shape[0] // MLA_SUB):
        rows = slice(j * MLA_SUB, (j + 1) * MLA_SUB)
        s = lax.dot_general(q_ref[rows, :], k_ref[...], (((1,), (1,)), ((), ())),
                            preferred_element_type=F32)
        m = jnp.max(s, axis=-1, keepdims=True)
        p = jnp.exp2(s - m).astype(BF16)
        ox = jnp.dot(p, vx_ref[...], preferred_element_type=F32)
        o_ref[rows, :] = (ox[:, :LANE] / ox[:, LANE:LANE + 1]).astype(o_ref.dtype)


def _mla_attn(q, k, v, bsz, seq):
    n = q.shape[0]
    tq = 2048
    nq = seq // tq
    return pl.pallas_call(
        _mla_attn_kernel,
        grid=(bsz, B_HEADS, nq),
        in_specs=[
            pl.BlockSpec((tq, 2 * LANE), lambda b, h, i: (b * nq + i, h)),
            pl.BlockSpec((seq, 2 * LANE), lambda b, h, i: (b, h)),
            pl.BlockSpec((seq, LANE), lambda b, h, i: (b, h)),
        ],
        out_specs=pl.BlockSpec((tq, LANE), lambda b, h, i: (b * nq + i, h)),
        out_shape=jax.ShapeDtypeStruct((n, B_HEADS * LANE), BF16),
        scratch_shapes=[pltpu.VMEM((seq, 2 * LANE), BF16)],
        compiler_params=_cparams(("arbitrary", "arbitrary", "arbitrary")),
        name="mla_attn",
    )(q, k, v)


def _dilated_kernel(*refs, seq):
    qkv = refs[:3 * A_NG]
    posc_ref = refs[3 * A_NG]
    posr = refs[3 * A_NG + 1: 3 * A_NG + 1 + A_NG]
    sl_ref = refs[3 * A_NG + 1 + A_NG]
    o_ref = refs[3 * A_NG + 2 + A_NG]
    qf, kf, vf, og, lg = refs[3 * A_NG + 3 + A_NG:]
    scale = HEAD_DIM ** -0.5

    for g, (win, dil) in enumerate(A_GROUPS):
        radius = win // (2 * dil)
        length = seq // dil
        kw = min(2 * QBLK, length)
        q_in, k_in, v_in = qkv[3 * g], qkv[3 * g + 1], qkv[3 * g + 2]
        slope2 = sl_ref[0, g:g + 1, 0:1] * LOG2E
        if dil > 1:
            qf[...] = q_in[0].astype(F32)
            kf[...] = k_in[0].astype(F32)
            vf[...] = v_in[0].astype(F32)

        def rows(start, size, dil=dil):
            if dil == 1:
                return pl.ds(start, size)
            return pl.ds(start, size, stride=dil)

        def load(ref_in, ref_f32, rws, dil=dil):
            if dil == 1:
                return ref_in[0, rws, :]
            return ref_f32[rws, :].astype(BF16)

        units = []
        for r in range(dil):
            for ib in range(length // QBLK):
                i0 = ib * QBLK
                ws = min(max(i0 - radius, 0), length - kw)
                units.append((r, i0, ws, rows(r + dil * i0, QBLK), rows(r + dil * ws, kw)))
        qs = [(load(q_in, qf, u[3]).astype(F32) * (scale * LOG2E)).astype(BF16) for u in units]
        ks = [load(k_in, kf, u[4]) for u in units]
        ss = [lax.dot_general(q, k, (((1,), (1,)), ((), ())), preferred_element_type=F32)
              for q, k in zip(qs, ks)]
        biased = []
        for (r, i0, ws, q_rows, _), s in zip(units, ss):
            pq = posc_ref[0, q_rows, :]
            pk = posr[g][0, r:r + 1, ws:ws + kw]
            dist = jnp.abs(pq - pk).astype(F32)
            ii = i0 + lax.broadcasted_iota(I32, (QBLK, kw), 0)
            jj = ws + lax.broadcasted_iota(I32, (QBLK, kw), 1)
            biased.append(jnp.where(jnp.abs(ii - jj) <= radius, s - slope2 * dist, NEG))
        ms = [jnp.max(s, axis=-1, keepdims=True) for s in biased]
        ps = [jnp.exp2(s - m) for s, m in zip(biased, ms)]
        dens = [jnp.sum(p, axis=-1, keepdims=True) for p in ps]
        vs = [load(v_in, vf, u[4]) for u in units]
        outs = [jnp.dot(p.astype(BF16), v, preferred_element_type=F32) / den
                for p, v, den in zip(ps, vs, dens)]
        for u, o, m, den in zip(units, outs, ms, dens):
            og[g, u[3], :] = o
            lg[g, u[3], :] = jnp.broadcast_to(m * (1.0 / LOG2E) + jnp.log(den), (QBLK, HEAD_DIM))

    chunk = 256

    def merge(ci, carry):
        rs = pl.ds(pl.multiple_of(ci * chunk, chunk), chunk)
        l0, l1, l2 = lg[0, rs, :], lg[1, rs, :], lg[2, rs, :]
        m = jnp.maximum(jnp.maximum(l0, l1), l2)
        w0, w1, w2 = jnp.exp(l0 - m), jnp.exp(l1 - m), jnp.exp(l2 - m)
        o = (w0 * og[0, rs, :] + w1 * og[1, rs, :] + w2 * og[2, rs, :]) / (w0 + w1 + w2)
        o_ref[0, rs, :] = o.astype(o_ref.dtype)
        return carry

    lax.fori_loop(0, seq // chunk, merge, 0)


def _dilated_attn(proj3, posc, posr, slopes):
    bsz, seq, _ = proj3.shape
    in_specs = []
    args = []
    for g in range(A_NG):
        for base in (COL_QA, COL_KA, COL_VA):
            in_specs.append(pl.BlockSpec(
                (1, seq, LANE), lambda b, h, base=base, g=g: (b, 0, base + A_HPG * g + h)))
            args.append(proj3)
    in_specs.append(pl.BlockSpec((1, seq, 1), lambda b, h: (b, 0, 0)))
    args.append(posc)
    for g, (_, dil) in enumerate(A_GROUPS):
        in_specs.append(pl.BlockSpec((1, dil, seq // dil), lambda b, h: (b, 0, 0)))
        args.append(posr[g])
    in_specs.append(pl.BlockSpec((1, 8, LANE), lambda b, h: (h, 0, 0)))
    args.append(slopes)
    return pl.pallas_call(
        functools.partial(_dilated_kernel, seq=seq),
        grid=(bsz, A_HPG),
        in_specs=in_specs,
        out_specs=pl.BlockSpec((1, seq, LANE), lambda b, h: (b, 0, h)),
        out_shape=jax.ShapeDtypeStruct((bsz, seq, A_OUT), BF16),
        scratch_shapes=[pltpu.VMEM((seq, HEAD_DIM), F32)] * 3
        + [pltpu.VMEM((A_NG, seq, HEAD_DIM), F32)] * 2,
        compiler_params=_cparams(("arbitrary", "arbitrary")),
        name="dilated_attn",
    )(*args)


def _merge_out_kernel(ya_ref, yb_ref, ga_ref, gb_ref, x_ref, wa_ref, wb_ref, wo_ref, gt_ref,
                      g2_ref, sc_ref, sh_ref, wr_ref, br_ref, xo_ref, h_ref, lg_ref):
    a = jnp.dot(ya_ref[...], wa_ref[...], preferred_element_type=F32)
    b = jnp.dot(yb_ref[...], wb_ref[...], preferred_element_type=F32)
    ga = jax.nn.sigmoid(ga_ref[...].astype(F32))
    gb = jax.nn.sigmoid(gb_ref[...].astype(F32))
    merged = (ga * a + gb * b).astype(BF16)
    xn = x_ref[...] + (1.0 + gt_ref[0]) * jnp.dot(merged, wo_ref[...], preferred_element_type=F32)
    xo_ref[...] = xn
    h = _rms(xn, g2_ref[...]) * (1.0 + sc_ref[0]) + sh_ref[0]
    _pack_store(h, h_ref)
    h_hi = h.astype(BF16)
    h_lo = (h - h_hi.astype(F32)).astype(BF16)
    t = jnp.dot(h_hi, wr_ref[...], preferred_element_type=F32)
    u = jnp.dot(h_lo, wr_ref[:, :LANE], preferred_element_type=F32)
    lg_ref[...] = t[:, :LANE] + t[:, LANE:] + u + br_ref[...]


def _merge_out(ya, yb, proj, x2, wa, wb, wo, gt, g2, sc, sh, wr, br, seq):
    n, d = x2.shape
    tm = 512
    per_b = seq // tm
    const = lambda i: (0, 0)
    bidx = lambda i: (i // per_b, 0, 0)
    once = pl.Buffered(1)
    return pl.pallas_call(
        _merge_out_kernel,
        grid=(n // tm,),
        in_specs=[
            pl.BlockSpec((tm, A_OUT), lambda i: (i, 0)),
            pl.BlockSpec((tm, B_OUT), lambda i: (i, 0)),
            pl.BlockSpec((tm, d), lambda i: (i, COL_GA * LANE // d)),
            pl.BlockSpec((tm, d), lambda i: (i, COL_GB * LANE // d)),
            pl.BlockSpec((tm, d), lambda i: (i, 0)),
            pl.BlockSpec(wa.shape, const, pipeline_mode=once),
            pl.BlockSpec(wb.shape, const, pipeline_mode=once),
            pl.BlockSpec(wo.shape, const, pipeline_mode=once),
            pl.BlockSpec((1, 1, d), bidx),
            pl.BlockSpec((1, d), const),
            pl.BlockSpec((1, 1, d), bidx),
            pl.BlockSpec((1, 1, d), bidx),
            pl.BlockSpec(wr.shape, const),
            pl.BlockSpec((1, LANE), const),
        ],
        out_specs=[pl.BlockSpec((tm, d), lambda i: (i, 0)),
                   pl.BlockSpec((tm * PACK_ROWS, LANE), lambda i: (i, 0)),
                   pl.BlockSpec((tm, LANE), lambda i: (i, 0))],
        out_shape=[jax.ShapeDtypeStruct((n, d), F32),
                   jax.ShapeDtypeStruct((n * PACK_ROWS, LANE), U32),
                   jax.ShapeDtypeStruct((n, LANE), F32)],
        compiler_params=_cparams(("arbitrary",)),
        name="merge_out",
    )(ya, yb, proj, proj, x2, wa, wb, wo, gt, g2, sc, sh, wr, br)


def _route_kernel(lg_ref, rt_ref, rw_ref, cnt_ref, carry_ref):
    @pl.when(pl.program_id(0) == 0)
    def _():
        carry_ref[...] = jnp.zeros_like(carry_ref)

    lg = lg_ref[...]
    tm = lg.shape[0]
    lane = lax.broadcasted_iota(I32, lg.shape, 1).astype(F32)
    big = float(2 * LANE)
    is_grp = lane < N_GROUPS
    gl = jnp.where(is_grp, lg, NEG)
    gm = jnp.max(gl, axis=-1, keepdims=True)
    ge = jnp.where(is_grp, jnp.exp(gl - gm), 0.0)
    pg = ge / jnp.sum(ge, axis=-1, keepdims=True)
    p_top = jnp.max(pg, axis=-1, keepdims=True)
    g_idx = jnp.min(jnp.where(is_grp & (pg == p_top), lane, big), axis=-1, keepdims=True)

    lo = N_GROUPS + EPG * g_idx
    in_grp = (lane >= lo) & (lane < lo + EPG)
    el = jnp.where(in_grp, lg, NEG)
    em = jnp.max(el, axis=-1, keepdims=True)
    ee = jnp.where(in_grp, jnp.exp(el - em), 0.0)
    pin = ee / jnp.sum(ee, axis=-1, keepdims=True)
    v1 = jnp.max(pin, axis=-1, keepdims=True)
    i1 = jnp.min(jnp.where(in_grp & (pin == v1), lane, big), axis=-1, keepdims=True)
    rest = in_grp & (lane != i1)
    p2 = jnp.where(rest, pin, -1.0)
    v2 = jnp.max(p2, axis=-1, keepdims=True)
    i2 = jnp.min(jnp.where(rest & (p2 == v2), lane, big), axis=-1, keepdims=True)
    vs = v1 + v2
    w1 = p_top * v1 / vs
    w2 = p_top * v2 / vs

    oh1 = lane == i1
    oh2 = lane == i2
    mm = (oh1 | oh2).astype(BF16)
    tri = (lax.broadcasted_iota(I32, (tm, tm), 0) > lax.broadcasted_iota(I32, (tm, tm), 1))
    cnt = jnp.dot(tri.astype(BF16), mm, preferred_element_type=F32) + carry_ref[...]
    r1 = jnp.sum(jnp.where(oh1, cnt, 0.0), axis=-1, keepdims=True)
    r2 = jnp.sum(jnp.where(oh2, cnt, 0.0), axis=-1, keepdims=True)
    carry_ref[...] += jnp.sum(mm.astype(F32), axis=0, keepdims=True)
    cnt_ref[...] = carry_ref[...]

    e1 = i1 - N_GROUPS
    e2 = i2 - N_GROUPS
    packed = jnp.where(lane == 0, e1, jnp.where(lane == 1, e2, jnp.where(
        lane == 2, r1, jnp.where(lane == 3, r2, jnp.where(lane == 4, w1, w2)))))
    rt_ref[...] = packed.T[:8, :]
    rw_ref[...] = jnp.where(lane == 0, w1, w2)


def _route(logits):
    n = logits.shape[0]
    tm = 512
    return pl.pallas_call(
        _route_kernel,
        grid=(n // tm,),
        in_specs=[pl.BlockSpec((tm, LANE), lambda i: (i, 0))],
        out_specs=[pl.BlockSpec((8, tm), lambda i: (0, i)),
                   pl.BlockSpec((tm, LANE), lambda i: (i, 0)),
                   pl.BlockSpec((1, LANE), lambda i: (0, 0))],
        out_shape=[jax.ShapeDtypeStruct((8, n), F32),
                   jax.ShapeDtypeStruct((n, LANE), F32),
                   jax.ShapeDtypeStruct((1, LANE), F32)],
        scratch_shapes=[pltpu.VMEM((1, LANE), F32)],
        compiler_params=_cparams(("arbitrary",)),
        name="route",
    )(logits)


PACK_ROWS = 8
HI_MASK = 0xFFFF0000


def _pack_store(val, ref):
    rows = val.shape[0]
    for c in range(PACK_ROWS):
        lo = val[:, 2 * c * LANE:(2 * c + 1) * LANE].astype(BF16).astype(F32)
        hi = val[:, (2 * c + 1) * LANE:(2 * c + 2) * LANE].astype(BF16).astype(F32)
        word = (pltpu.bitcast(hi, U32) & jnp.uint32(HI_MASK)) | (pltpu.bitcast(lo, U32) >> 16)
        ref[pl.ds(c, rows, stride=PACK_ROWS), :] = word


def _unpack_load(load_chunk):
    pieces = []
    for c in range(PACK_ROWS):
        w = load_chunk(c)
        pieces.append(pltpu.bitcast(w << 16, F32))
        pieces.append(pltpu.bitcast(w & jnp.uint32(HI_MASK), F32))
    return pieces


def _start_rows(src_ref, dst_ref, idx_ref, base, count, sem):
    for j in range(count):
        src_row = pl.multiple_of(idx_ref[0, 0, base + j] * PACK_ROWS, PACK_ROWS)
        pltpu.make_async_copy(src_ref.at[pl.ds(src_row, PACK_ROWS)],
                              dst_ref.at[pl.ds(j * PACK_ROWS, PACK_ROWS)], sem).start()


def _wait_rows(src_ref, dst_ref, sem):
    pltpu.make_async_copy(src_ref.at[pl.ds(0, dst_ref.shape[0])], dst_ref, sem).wait()


def _expert_kernel(be_ref, nx_ref, nu_ref, tok0_ref, tokn_ref, h_ref, wgu_ref, wd_ref, o_ref,
                   xbuf0, xbuf1, wgu_f, wd_f, wgu_s, wd_s, sem, wsem, *, layer):
    i = pl.program_id(0)
    n_used = nu_ref[0]
    tb = xbuf0.shape[0] // PACK_ROWS

    def weight_copies(e):
        return (pltpu.make_async_copy(wgu_ref.at[layer, e], wgu_f, wsem.at[0]),
                pltpu.make_async_copy(wd_ref.at[layer, e], wd_f, wsem.at[1]))

    @pl.when(i == 0)
    def _():
        for cp in weight_copies(be_ref[0]):
            cp.start(priority=1)
        _start_rows(h_ref, xbuf0, tok0_ref, 0, tb, sem.at[0])

    @pl.when(i < n_used)
    def _():
        e = be_ref[i]
        prev = be_ref[jnp.maximum(i - 1, 0)]

        @pl.when((i == 0) | (e != prev))
        def _():
            for cp in weight_copies(e):
                cp.wait()
            wgu_s[...] = wgu_f[...].astype(BF16)
            wd_s[...] = wd_f[...].astype(BF16)

            @pl.when(nx_ref[i] >= 0)
            def _():
                for cp in weight_copies(nx_ref[i]):
                    cp.start(priority=1)

        def block(cur_buf, cur_sem, nxt_buf, nxt_sem):
            _wait_rows(h_ref, cur_buf, cur_sem)
            _start_rows(h_ref, nxt_buf, tokn_ref, 0, tb, nxt_sem)
            pieces = _unpack_load(lambda c: cur_buf[pl.ds(c, tb, stride=PACK_ROWS), :])
            x = jnp.concatenate([p.astype(BF16) for p in pieces], axis=-1)
            gu = jnp.dot(x, wgu_s[...], preferred_element_type=F32)
            gate = gu[:, :D_EXPERT]
            up = gu[:, D_EXPERT:]
            act = (gate * jax.nn.sigmoid(gate) * up).astype(BF16)
            _pack_store(jnp.dot(act, wd_s[...], preferred_element_type=F32), o_ref)

            @pl.when(i + 1 >= n_used)
            def _():
                _wait_rows(h_ref, nxt_buf, nxt_sem)

        @pl.when(i % 2 == 0)
        def _():
            block(xbuf0, sem.at[0], xbuf1, sem.at[1])

        @pl.when(i % 2 == 1)
        def _():
            block(xbuf1, sem.at[1], xbuf0, sem.at[0])

    @pl.when(i >= n_used)
    def _():
        o_ref[...] = jnp.zeros_like(o_ref)


def _experts(block_e, next_e, n_used, slot_tok3, h_packed, w_gu, w_down, layer):
    d = w_gu.shape[2]
    nb = slot_tok3.shape[0]
    tile_rows = EXPERT_TILE * PACK_ROWS

    def nxt_blk(i, be, nx, nu):
        return (jnp.minimum(i + 1, nu[0] - 1), 0, 0)

    grid_spec = pltpu.PrefetchScalarGridSpec(
        num_scalar_prefetch=3,
        grid=(nb,),
        in_specs=[
            pl.BlockSpec((1, 1, EXPERT_TILE), lambda i, be, nx, nu: (0, 0, 0),
                         memory_space=pltpu.SMEM),
            pl.BlockSpec((1, 1, EXPERT_TILE), nxt_blk, memory_space=pltpu.SMEM),
            pl.BlockSpec(memory_space=pl.ANY),
            pl.BlockSpec(memory_space=pl.ANY),
            pl.BlockSpec(memory_space=pl.ANY),
        ],
        out_specs=pl.BlockSpec((tile_rows, LANE), lambda i, be, nx, nu: (i, 0)),
        scratch_shapes=[pltpu.VMEM((tile_rows, LANE), U32), pltpu.VMEM((tile_rows, LANE), U32),
                        pltpu.VMEM((d, 2 * D_EXPERT), F32), pltpu.VMEM((D_EXPERT, d), F32),
                        pltpu.VMEM((d, 2 * D_EXPERT), BF16), pltpu.VMEM((D_EXPERT, d), BF16),
                        pltpu.SemaphoreType.DMA((2,)), pltpu.SemaphoreType.DMA((2,))],
    )
    return pl.pallas_call(
        functools.partial(_expert_kernel, layer=layer),
        grid_spec=grid_spec,
        out_shape=jax.ShapeDtypeStruct((nb * tile_rows, LANE), U32),
        compiler_params=_cparams(("arbitrary",)),
        name="experts",
    )(block_e, next_e, n_used, slot_tok3, slot_tok3, h_packed, w_gu, w_down)


def _combine_kernel(dest0_ref, destn_ref, ys_ref, x_ref, rw_ref, gt_ref, fg_ref, o_ref, ybuf, sem,
                    *, final):
    i = pl.program_id(0)
    tm = x_ref.shape[0]

    def start(dest_ref, slot):
        for k in range(TOP_K):
            _start_rows(ys_ref, ybuf.at[slot, k], dest_ref, k * tm, tm, sem.at[slot, k])

    @pl.when(i == 0)
    def _():
        start(dest0_ref, 0)

    @pl.when(i + 1 < pl.num_programs(0))
    def _():
        start(destn_ref, (i + 1) % 2)

    cur = i % 2
    for k in range(TOP_K):
        _wait_rows(ys_ref, ybuf.at[cur, k], sem.at[cur, k])
    rw = rw_ref[...]
    y0 = _unpack_load(lambda c: ybuf[cur, 0, pl.ds(c, tm, stride=PACK_ROWS), :])
    y1 = _unpack_load(lambda c: ybuf[cur, 1, pl.ds(c, tm, stride=PACK_ROWS), :])
    y = jnp.concatenate([rw[:, 0:1] * a + rw[:, 1:2] * b for a, b in zip(y0, y1)], axis=-1)
    xn = x_ref[...] + (1.0 + gt_ref[0]) * y
    if final:
        xn = _rms(xn, fg_ref[...])
    o_ref[...] = xn


def _combine(dest3, ys, x2, rw, gt, fg, seq, final):
    n, d = x2.shape
    tm = dest3.shape[2] // TOP_K
    per_b = seq // tm
    nt = n // tm
    return pl.pallas_call(
        functools.partial(_combine_kernel, final=final),
        grid=(nt,),
        in_specs=[pl.BlockSpec((1, 1, TOP_K * tm), lambda i: (0, 0, 0), memory_space=pltpu.SMEM),
                  pl.BlockSpec((1, 1, TOP_K * tm), lambda i: (jnp.minimum(i + 1, nt - 1), 0, 0),
                               memory_space=pltpu.SMEM),
                  pl.BlockSpec(memory_space=pl.ANY),
                  pl.BlockSpec((tm, d), lambda i: (i, 0)),
                  pl.BlockSpec((tm, LANE), lambda i: (i, 0)),
                  pl.BlockSpec((1, 1, d), lambda i: (i // per_b, 0, 0)),
                  pl.BlockSpec((1, d), lambda i: (0, 0))],
        out_specs=pl.BlockSpec((tm, d), lambda i: (i, 0)),
        out_shape=jax.ShapeDtypeStruct((n, d), F32),
        scratch_shapes=[pltpu.VMEM((2, TOP_K, tm * PACK_ROWS, LANE), U32),
                        pltpu.SemaphoreType.DMA((2, TOP_K))],
        compiler_params=_cparams(("arbitrary",)),
        name="combine",
    )(dest3, dest3, ys, x2, rw, gt, fg)


def _rope_cols(w3):
    half = B_ROPE // 2
    x1, x2 = w3[..., :half], w3[..., half:]
    z = jnp.zeros(w3.shape[:-1] + (LANE - B_ROPE,), w3.dtype)
    a = jnp.concatenate([x1, x2, z], axis=-1)
    b = jnp.concatenate([x2, x1, z], axis=-1)
    return a, b


def _layout_w_in(w):
    d = w.shape[0]
    o = 0
    qa = w[:, o:o + A_WIDTH]; o += A_WIDTH
    ka = w[:, o:o + A_WIDTH]; o += A_WIDTH
    va = w[:, o:o + A_WIDTH]; o += A_WIDTH
    cq = w[:, o:o + Q_LORA]; o += Q_LORA
    ckv = w[:, o:o + KV_LORA]; o += KV_LORA
    kr = w[:, o:o + B_ROPE]; o += B_ROPE
    gates = w[:, o:]
    kra, krb = _rope_cols(kr.reshape(d, 1, B_ROPE))
    return jnp.concatenate([gates, qa, ka, va, cq, ckv, kra.reshape(d, LANE),
                            krb.reshape(d, LANE)], axis=1).astype(BF16)


def _layout_w_uq(w):
    k = w.shape[0]
    w3 = w.reshape(k, B_HEADS, B_QK)
    nope = w3[:, :, :B_NOPE].reshape(k, B_HEADS * LANE)
    a, b = _rope_cols(w3[:, :, B_NOPE:])
    return jnp.concatenate([nope, a.reshape(k, -1), b.reshape(k, -1)], axis=1).astype(BF16)


def _layout_w_ukv(w):
    k = w.shape[0]
    w3 = w.reshape(k, B_HEADS, B_NOPE + B_V)
    return jnp.concatenate([w3[:, :, :B_NOPE].reshape(k, -1),
                            w3[:, :, B_NOPE:].reshape(k, -1)], axis=1).astype(BF16)


def kernel(x, c, positions, ln1_g, ln2_g, w_ada, b_ada, w_in, q_norm_g, w_uq, kv_norm_g, w_ukv,
           w_a_up, w_b_up, w_o, w_grp, b_grp, w_exp, b_exp, w_gu, w_down, final_g):
    bsz, seq, d = x.shape
    depth = w_in.shape[0]
    n = bsz * seq
    n_blocks = n * TOP_K // EXPERT_TILE + N_EXPERTS
    n_slots = n_blocks * EXPERT_TILE
    tile_tok = 256

    mod = _ada(c, w_ada, b_ada).reshape(depth, bsz, N_MOD, 1, d)

    pos_col = positions.reshape(n, 1)
    ctab, stab = _rope_tables(pos_col)
    posc = positions.reshape(bsz, seq, 1)
    posr = [positions.reshape(bsz, seq // dil, dil).transpose(0, 2, 1) for _, dil in A_GROUPS]
    slopes = 2.0 ** (-ALIBI_MAX_BIAS * jnp.arange(1, A_HEADS + 1, dtype=F32) / A_HEADS)
    slopes = slopes.reshape(A_NG, A_HPG).T
    slopes = jnp.pad(slopes, ((0, 0), (0, 8 - A_NG)))
    slopes = jnp.broadcast_to(slopes[:, :, None], (A_HPG, 8, LANE))

    x2 = x.reshape(n, d)
    for l in range(depth):
        sh1, sc1, gt1, sh2, sc2, gt2 = (mod[l, :, m] for m in range(N_MOD))

        proj = _norm_proj(x2, ln1_g[l].reshape(1, d), sc1, sh1, _layout_w_in(w_in[l]), seq)
        q = _mla_q(proj, q_norm_g[l].reshape(1, Q_LORA), _layout_w_uq(w_uq[l]), ctab, stab)
        k, v = _mla_kv(proj, kv_norm_g[l].reshape(1, KV_LORA), _layout_w_ukv(w_ukv[l]), ctab, stab)
        yb = _mla_attn(q, k, v, bsz, seq)
        ya = _dilated_attn(proj.reshape(bsz, seq, PROJ_COLS), posc, posr, slopes).reshape(n, A_OUT)

        w_r = jnp.concatenate([w_grp[l], w_exp[l],
                               jnp.zeros((d, LANE - N_GROUPS - N_EXPERTS), F32)], axis=1)
        w_r_hi = w_r.astype(BF16)
        w_r = jnp.concatenate([w_r_hi, (w_r - w_r_hi.astype(F32)).astype(BF16)], axis=1)
        b_r = jnp.concatenate([b_grp[l], b_exp[l],
                               jnp.zeros((LANE - N_GROUPS - N_EXPERTS,), F32)]).reshape(1, LANE)
        x2, h2, logits = _merge_out(ya, yb, proj, x2, w_a_up[l].astype(BF16),
                                    w_b_up[l].astype(BF16), w_o[l].astype(BF16), gt1,
                                    ln2_g[l].reshape(1, d), sc2, sh2, w_r, b_r, seq)

        rt, rw, cnt = _route(logits)
        counts = cnt[0, N_GROUPS:N_GROUPS + N_EXPERTS].astype(I32)
        padded = (counts + EXPERT_TILE - 1) // EXPERT_TILE * EXPERT_TILE
        pends = jnp.cumsum(padded)
        pstarts = pends - padded
        eids = rt[:TOP_K].astype(I32)
        onehot = eids[:, :, None] == jnp.arange(N_EXPERTS, dtype=I32)
        dest = jnp.sum(jnp.where(onehot, pstarts, 0), axis=-1) + rt[TOP_K:2 * TOP_K].astype(I32)
        dest3 = dest.reshape(TOP_K, n // tile_tok, tile_tok).transpose(1, 0, 2).reshape(
            n // tile_tok, 1, TOP_K * tile_tok)
        n_used = (pends[-1:] // EXPERT_TILE).astype(I32)
        blk_start = jnp.arange(n_blocks, dtype=I32) * EXPERT_TILE
        block_e = jnp.minimum(jnp.sum((pends[None, :] <= blk_start[:, None]).astype(I32), axis=1),
                              N_EXPERTS - 1)

        tok = jnp.broadcast_to(jnp.arange(n, dtype=I32), (TOP_K, n))
        slot_tok = jnp.zeros((n_slots,), I32).at[dest.reshape(-1)].set(
            tok.reshape(-1), unique_indices=True)
        eidx = jnp.arange(N_EXPERTS, dtype=I32)
        nxt_used = lax.cummin(jnp.where(counts > 0, eidx, N_EXPERTS), reverse=True)
        nxt_after = jnp.concatenate([nxt_used[1:], jnp.full((1,), N_EXPERTS, I32)])
        nxt_after = jnp.where(nxt_after >= N_EXPERTS, -1, nxt_after)
        next_e = jnp.sum(jnp.where(block_e[:, None] == eidx[None, :], nxt_after[None, :], 0),
                         axis=1).astype(I32)
        ys = _experts(block_e, next_e, n_used, slot_tok.reshape(n_blocks, 1, EXPERT_TILE), h2,
                      w_gu, w_down, l)
        x2 = _combine(dest3, ys, x2, rw, gt2, final_g.reshape(1, d), seq, final=(l == depth - 1))
    return x2.reshape(bsz, seq, d)
```

```python
import functools

import jax
import jax.numpy as jnp
from jax import lax
from jax.experimental import pallas as pl
from jax.experimental.pallas import tpu as pltpu

F32 = jnp.float32
BF16 = jnp.bfloat16
I32 = jnp.int32
U32 = jnp.uint32
HIGHEST = lax.Precision.HIGHEST

HEAD_DIM = 128
A_GROUPS = ((128, 1), (512, 4), (2048, 16))
A_HPG = 4
A_NG = len(A_GROUPS)
A_HEADS = A_HPG * A_NG
A_WIDTH = A_HEADS * HEAD_DIM
A_OUT = A_HPG * HEAD_DIM
ALIBI_MAX_BIAS = 8.0
B_HEADS = 8
B_NOPE = 128
B_ROPE = 64
B_V = 128
B_QK = B_NOPE + B_ROPE
Q_LORA = 512
KV_LORA = 512
B_OUT = B_HEADS * B_V
ROPE_THETA = 10000.0
N_GROUPS = 8
EPG = 8
N_EXPERTS = N_GROUPS * EPG
TOP_K = 2
D_EXPERT = 512
N_MOD = 6
EPS = 1e-6
NEG = -1e30
LOG2E = 1.4426950408889634

LANE = 128
VMEM_LIMIT = 56 * 1024 * 1024
QBLK = 128
EXPERT_TILE = 256

COL_GA = 0
COL_GB = 16
COL_QA = 32
COL_KA = 44
COL_VA = 56
COL_CQ = 68
COL_CKV = 72
COL_KR = 76
N_COLBLK = 78
PROJ_COLS = N_COLBLK * LANE


def _cparams(sem):
    return pltpu.CompilerParams(dimension_semantics=sem, vmem_limit_bytes=VMEM_LIMIT)


def _rms(x, g):
    return x * lax.rsqrt(jnp.mean(x * x, axis=-1, keepdims=True) + EPS) * g


def _ada_kernel(c_ref, w_ref, b_ref, o_ref):
    c = c_ref[...]
    cs = c * jax.nn.sigmoid(c)
    o_ref[0] = jnp.dot(cs, w_ref[0], precision=HIGHEST, preferred_element_type=F32) + b_ref[0]


def _ada(c, w_ada, b_ada):
    nl, d, nm = w_ada.shape
    bsz = c.shape[0]
    tn = 1536
    return pl.pallas_call(
        _ada_kernel,
        grid=(nl, nm // tn),
        in_specs=[
            pl.BlockSpec((bsz, d), lambda l, j: (0, 0)),
            pl.BlockSpec((1, d, tn), lambda l, j: (l, 0, j)),
            pl.BlockSpec((1, 1, tn), lambda l, j: (l, 0, j)),
        ],
        out_specs=pl.BlockSpec((1, bsz, tn), lambda l, j: (l, 0, j)),
        out_shape=jax.ShapeDtypeStruct((nl, bsz, nm), F32),
        compiler_params=_cparams(("arbitrary", "arbitrary")),
        name="ada",
    )(c, w_ada, b_ada.reshape(nl, 1, nm))


def _rope_table_kernel(pos_ref, inv_ref, c_ref, s_ref):
    ang = pos_ref[...].astype(F32) * inv_ref[...]
    lane = lax.broadcasted_iota(I32, ang.shape, 1)
    half = B_ROPE // 2
    cs = jnp.cos(ang)
    sn = jnp.sin(ang)
    c_ref[...] = jnp.where(lane < B_ROPE, cs, 0.0)
    s_ref[...] = jnp.where(lane < half, -sn, jnp.where(lane < B_ROPE, sn, 0.0))


def _rope_tables(pos_col):
    n = pos_col.shape[0]
    half = B_ROPE // 2
    inv = ROPE_THETA ** (-jnp.arange(half, dtype=F32) / half)
    inv_row = jnp.concatenate([inv, inv, jnp.zeros((LANE - B_ROPE,), F32)]).reshape(1, LANE)
    tm = 2048
    return pl.pallas_call(
        _rope_table_kernel,
        grid=(n // tm,),
        in_specs=[pl.BlockSpec((tm, 1), lambda i: (i, 0)),
                  pl.BlockSpec((1, LANE), lambda i: (0, 0))],
        out_specs=[pl.BlockSpec((tm, LANE), lambda i: (i, 0)),
                   pl.BlockSpec((tm, LANE), lambda i: (i, 0))],
        out_shape=[jax.ShapeDtypeStruct((n, LANE), F32)] * 2,
        compiler_params=_cparams(("arbitrary",)),
        name="rope_tables",
    )(pos_col, inv_row)


def _norm_proj_kernel(x_ref, g_ref, sc_ref, sh_ref, w_ref, o_ref, h_ref):
    @pl.when(pl.program_id(1) == 0)
    def _():
        h = _rms(x_ref[...], g_ref[...]) * (1.0 + sc_ref[0]) + sh_ref[0]
        h_ref[...] = h.astype(BF16)

    o_ref[...] = jnp.dot(h_ref[...], w_ref[...], preferred_element_type=F32).astype(o_ref.dtype)


def _norm_proj(x2, g, sc, sh, w, seq):
    n, d = x2.shape
    ncol = w.shape[1]
    tm = 1024
    tn = 13 * LANE
    per_b = seq // tm
    return pl.pallas_call(
        _norm_proj_kernel,
        grid=(n // tm, ncol // tn),
        in_specs=[
            pl.BlockSpec((tm, d), lambda i, j: (i, 0)),
            pl.BlockSpec((1, d), lambda i, j: (0, 0)),
            pl.BlockSpec((1, 1, d), lambda i, j: (i // per_b, 0, 0)),
            pl.BlockSpec((1, 1, d), lambda i, j: (i // per_b, 0, 0)),
            pl.BlockSpec((d, tn), lambda i, j: (0, j)),
        ],
        out_specs=pl.BlockSpec((tm, tn), lambda i, j: (i, j)),
        out_shape=jax.ShapeDtypeStruct((n, ncol), BF16),
        scratch_shapes=[pltpu.VMEM((tm, d), BF16)],
        compiler_params=_cparams(("arbitrary", "arbitrary")),
        name="norm_proj",
    )(x2, g, sc, sh, w)


def _mla_q_kernel(cq_ref, g_ref, w_ref, c_ref, s_ref, q_ref):
    h = _rms(cq_ref[...].astype(F32), g_ref[...]).astype(BF16)
    r = jnp.dot(h, w_ref[...], preferred_element_type=F32)
    scale = B_QK ** -0.5 * LOG2E
    c = c_ref[...]
    s = s_ref[...]
    nw = B_HEADS * LANE
    for hd in range(B_HEADS):
        nope = r[:, hd * LANE:(hd + 1) * LANE]
        ra = r[:, nw + hd * LANE: nw + (hd + 1) * LANE]
        rb = r[:, 2 * nw + hd * LANE: 2 * nw + (hd + 1) * LANE]
        q_ref[:, 2 * hd * LANE:(2 * hd + 1) * LANE] = (nope * scale).astype(BF16)
        q_ref[:, (2 * hd + 1) * LANE:(2 * hd + 2) * LANE] = ((ra * c + rb * s) * scale).astype(BF16)


def _mla_q(proj, g, w, ctab, stab):
    n = proj.shape[0]
    tm = 512
    return pl.pallas_call(
        _mla_q_kernel,
        grid=(n // tm,),
        in_specs=[
            pl.BlockSpec((tm, Q_LORA), lambda i: (i, COL_CQ * LANE // Q_LORA)),
            pl.BlockSpec((1, Q_LORA), lambda i: (0, 0)),
            pl.BlockSpec(w.shape, lambda i: (0, 0)),
            pl.BlockSpec((tm, LANE), lambda i: (i, 0)),
            pl.BlockSpec((tm, LANE), lambda i: (i, 0)),
        ],
        out_specs=pl.BlockSpec((tm, 2 * B_HEADS * LANE), lambda i: (i, 0)),
        out_shape=jax.ShapeDtypeStruct((n, 2 * B_HEADS * LANE), BF16),
        compiler_params=_cparams(("arbitrary",)),
        name="mla_q",
    )(proj, g, w, ctab, stab)


def _mla_kv_kernel(ckv_ref, kr_ref, g_ref, w_ref, c_ref, s_ref, k_ref, v_ref):
    h = _rms(ckv_ref[...].astype(F32), g_ref[...]).astype(BF16)
    r = jnp.dot(h, w_ref[...], preferred_element_type=F32)
    kr = kr_ref[...].astype(F32)
    krope = (kr[:, :LANE] * c_ref[...] + kr[:, LANE:] * s_ref[...]).astype(BF16)
    nw = B_HEADS * LANE
    for hd in range(B_HEADS):
        k_ref[:, 2 * hd * LANE:(2 * hd + 1) * LANE] = r[:, hd * LANE:(hd + 1) * LANE].astype(BF16)
        k_ref[:, (2 * hd + 1) * LANE:(2 * hd + 2) * LANE] = krope
    v_ref[...] = r[:, nw:].astype(BF16)


def _mla_kv(proj, g, w, ctab, stab):
    n = proj.shape[0]
    tm = 512
    return pl.pallas_call(
        _mla_kv_kernel,
        grid=(n // tm,),
        in_specs=[
            pl.BlockSpec((tm, KV_LORA), lambda i: (i, COL_CKV * LANE // KV_LORA)),
            pl.BlockSpec((tm, 2 * LANE), lambda i: (i, COL_KR // 2)),
            pl.BlockSpec((1, KV_LORA), lambda i: (0, 0)),
            pl.BlockSpec(w.shape, lambda i: (0, 0)),
            pl.BlockSpec((tm, LANE), lambda i: (i, 0)),
            pl.BlockSpec((tm, LANE), lambda i: (i, 0)),
        ],
        out_specs=[pl.BlockSpec((tm, 2 * B_HEADS * LANE), lambda i: (i, 0)),
                   pl.BlockSpec((tm, B_HEADS * LANE), lambda i: (i, 0))],
        out_shape=[jax.ShapeDtypeStruct((n, 2 * B_HEADS * LANE), BF16),
                   jax.ShapeDtypeStruct((n, B_HEADS * LANE), BF16)],
        compiler_params=_cparams(("arbitrary",)),
        name="mla_kv",
    )(proj, proj, g, w, ctab, stab)


MLA_SUB = 512


def _mla_attn_kernel(q_ref, k_ref, v_ref, o_ref, vx_ref):
    @pl.when(pl.program_id(2) == 0)
    def _():
        vx_ref[:, :LANE] = v_ref[...]
        vx_ref[:, LANE:] = jnp.ones((vx_ref.shape[0], LANE), BF16)

    for j in range(q_ref.shape[0] // MLA_SUB):
        rows = slice(j * MLA_SUB, (j + 1) * MLA_SUB)
        s = lax.dot_general(q_ref[rows, :], k_ref[...], (((1,), (1,)), ((), ())),
                            preferred_element_type=F32)
        m = jnp.max(s, axis=-1, keepdims=True)
        p = jnp.exp2(s - m).astype(BF16)
        ox = jnp.dot(p, vx_ref[...], preferred_element_type=F32)
        o_ref[rows, :] = (ox[:, :LANE] / ox[:, LANE:LANE + 1]).astype(o_ref.dtype)


def _mla_attn(q, k, v, bsz, seq):
    n = q.shape[0]
    tq = 2048
    nq = seq // tq
    return pl.pallas_call(
        _mla_attn_kernel,
        grid=(bsz, B_HEADS, nq),
        in_specs=[
            pl.BlockSpec((tq, 2 * LANE), lambda b, h, i: (b * nq + i, h)),
            pl.BlockSpec((seq, 2 * LANE), lambda b, h, i: (b, h)),
            pl.BlockSpec((seq, LANE), lambda b, h, i: (b, h)),
        ],
        out_specs=pl.BlockSpec((tq, LANE), lambda b, h, i: (b * nq + i, h)),
        out_shape=jax.ShapeDtypeStruct((n, B_HEADS * LANE), BF16),
        scratch_shapes=[pltpu.VMEM((seq, 2 * LANE), BF16)],
        compiler_params=_cparams(("arbitrary", "arbitrary", "arbitrary")),
        name="mla_attn",
    )(q, k, v)


def _dilated_kernel(*refs, seq):
    qkv = refs[:3 * A_NG]
    posc_ref = refs[3 * A_NG]
    posr = refs[3 * A_NG + 1: 3 * A_NG + 1 + A_NG]
    sl_ref = refs[3 * A_NG + 1 + A_NG]
    o_ref = refs[3 * A_NG + 2 + A_NG]
    qf, kf, vf, og, lg = refs[3 * A_NG + 3 + A_NG:]
    scale = HEAD_DIM ** -0.5

    for g, (win, dil) in enumerate(A_GROUPS):
        radius = win // (2 * dil)
        length = seq // dil
        kw = min(2 * QBLK, length)
        q_in, k_in, v_in = qkv[3 * g], qkv[3 * g + 1], qkv[3 * g + 2]
        slope2 = sl_ref[0, g:g + 1, 0:1] * LOG2E
        if dil > 1:
            qf[...] = q_in[0].astype(F32)
            kf[...] = k_in[0].astype(F32)
            vf[...] = v_in[0].astype(F32)

        def rows(start, size, dil=dil):
            if dil == 1:
                return pl.ds(start, size)
            return pl.ds(start, size, stride=dil)

        def load(ref_in, ref_f32, rws, dil=dil):
            if dil == 1:
                return ref_in[0, rws, :]
            return ref_f32[rws, :].astype(BF16)

        units = []
        for r in range(dil):
            for ib in range(length // QBLK):
                i0 = ib * QBLK
                ws = min(max(i0 - radius, 0), length - kw)
                units.append((r, i0, ws, rows(r + dil * i0, QBLK), rows(r + dil * ws, kw)))
        qs = [(load(q_in, qf, u[3]).astype(F32) * (scale * LOG2E)).astype(BF16) for u in units]
        ks = [load(k_in, kf, u[4]) for u in units]
        ss = [lax.dot_general(q, k, (((1,), (1,)), ((), ())), preferred_element_type=F32)
              for q, k in zip(qs, ks)]
        biased = []
        for (r, i0, ws, q_rows, _), s in zip(units, ss):
            pq = posc_ref[0, q_rows, :]
            pk = posr[g][0, r:r + 1, ws:ws + kw]
            dist = jnp.abs(pq - pk).astype(F32)
            ii = i0 + lax.broadcasted_iota(I32, (QBLK, kw), 0)
            jj = ws + lax.broadcasted_iota(I32, (QBLK, kw), 1)
            biased.append(jnp.where(jnp.abs(ii - jj) <= radius, s - slope2 * dist, NEG))
        ms = [jnp.max(s, axis=-1, keepdims=True) for s in biased]
        ps = [jnp.exp2(s - m) for s, m in zip(biased, ms)]
        dens = [jnp.sum(p, axis=-1, keepdims=True) for p in ps]
        vs = [load(v_in, vf, u[4]) for u in units]
        outs = [jnp.dot(p.astype(BF16), v, preferred_element_type=F32) / den
                for p, v, den in zip(ps, vs, dens)]
        for u, o, m, den in zip(units, outs, ms, dens):
            og[g, u[3], :] = o
            lg[g, u[3], :] = jnp.broadcast_to(m * (1.0 / LOG2E) + jnp.log(den), (QBLK, HEAD_DIM))

    chunk = 256

    def merge(ci, carry):
        rs = pl.ds(pl.multiple_of(ci * chunk, chunk), chunk)
        l0, l1, l2 = lg[0, rs, :], lg[1, rs, :], lg[2, rs, :]
        m = jnp.maximum(jnp.maximum(l0, l1), l2)
        w0, w1, w2 = jnp.exp(l0 - m), jnp.exp(l1 - m), jnp.exp(l2 - m)
        o = (w0 * og[0, rs, :] + w1 * og[1, rs, :] + w2 * og[2, rs, :]) / (w0 + w1 + w2)
        o_ref[0, rs, :] = o.astype(o_ref.dtype)
        return carry

    lax.fori_loop(0, seq // chunk, merge, 0)


def _dilated_attn(proj3, posc, posr, slopes):
    bsz, seq, _ = proj3.shape
    in_specs = []
    args = []
    for g in range(A_NG):
        for base in (COL_QA, COL_KA, COL_VA):
            in_specs.append(pl.BlockSpec(
                (1, seq, LANE), lambda b, h, base=base, g=g: (b, 0, base + A_HPG * g + h)))
            args.append(proj3)
    in_specs.append(pl.BlockSpec((1, seq, 1), lambda b, h: (b, 0, 0)))
    args.append(posc)
    for g, (_, dil) in enumerate(A_GROUPS):
        in_specs.append(pl.BlockSpec((1, dil, seq // dil), lambda b, h: (b, 0, 0)))
        args.append(posr[g])
    in_specs.append(pl.BlockSpec((1, 8, LANE), lambda b, h: (h, 0, 0)))
    args.append(slopes)
    return pl.pallas_call(
        functools.partial(_dilated_kernel, seq=seq),
        grid=(bsz, A_HPG),
        in_specs=in_specs,
        out_specs=pl.BlockSpec((1, seq, LANE), lambda b, h: (b, 0, h)),
        out_shape=jax.ShapeDtypeStruct((bsz, seq, A_OUT), BF16),
        scratch_shapes=[pltpu.VMEM((seq, HEAD_DIM), F32)] * 3
        + [pltpu.VMEM((A_NG, seq, HEAD_DIM), F32)] * 2,
        compiler_params=_cparams(("arbitrary", "arbitrary")),
        name="dilated_attn",
    )(*args)


def _merge_out_kernel(ya_ref, yb_ref, ga_ref, gb_ref, x_ref, wa_ref, wb_ref, wo_ref, gt_ref,
                      g2_ref, sc_ref, sh_ref, wr_ref, br_ref, xo_ref, h_ref, lg_ref):
    a = jnp.dot(ya_ref[...], wa_ref[...], preferred_element_type=F32)
    b = jnp.dot(yb_ref[...], wb_ref[...], preferred_element_type=F32)
    ga = jax.nn.sigmoid(ga_ref[...].astype(F32))
    gb = jax.nn.sigmoid(gb_ref[...].astype(F32))
    merged = (ga * a + gb * b).astype(BF16)
    xn = x_ref[...] + (1.0 + gt_ref[0]) * jnp.dot(merged, wo_ref[...], preferred_element_type=F32)
    xo_ref[...] = xn
    h = _rms(xn, g2_ref[...]) * (1.0 + sc_ref[0]) + sh_ref[0]
    _pack_store(h, h_ref)
    h_hi = h.astype(BF16)
    h_lo = (h - h_hi.astype(F32)).astype(BF16)
    t = jnp.dot(h_hi, wr_ref[...], preferred_element_type=F32)
    u = jnp.dot(h_lo, wr_ref[:, :LANE], preferred_element_type=F32)
    lg_ref[...] = t[:, :LANE] + t[:, LANE:] + u + br_ref[...]


def _merge_out(ya, yb, proj, x2, wa, wb, wo, gt, g2, sc, sh, wr, br, seq):
    n, d = x2.shape
    tm = 512
    per_b = seq // tm
    const = lambda i: (0, 0)
    bidx = lambda i: (i // per_b, 0, 0)
    once = pl.Buffered(1)
    return pl.pallas_call(
        _merge_out_kernel,
        grid=(n // tm,),
        in_specs=[
            pl.BlockSpec((tm, A_OUT), lambda i: (i, 0)),
            pl.BlockSpec((tm, B_OUT), lambda i: (i, 0)),
            pl.BlockSpec((tm, d), lambda i: (i, COL_GA * LANE // d)),
            pl.BlockSpec((tm, d), lambda i: (i, COL_GB * LANE // d)),
            pl.BlockSpec((tm, d), lambda i: (i, 0)),
            pl.BlockSpec(wa.shape, const, pipeline_mode=once),
            pl.BlockSpec(wb.shape, const, pipeline_mode=once),
            pl.BlockSpec(wo.shape, const, pipeline_mode=once),
            pl.BlockSpec((1, 1, d), bidx),
            pl.BlockSpec((1, d), const),
            pl.BlockSpec((1, 1, d), bidx),
            pl.BlockSpec((1, 1, d), bidx),
            pl.BlockSpec(wr.shape, const),
            pl.BlockSpec((1, LANE), const),
        ],
        out_specs=[pl.BlockSpec((tm, d), lambda i: (i, 0)),
                   pl.BlockSpec((tm * PACK_ROWS, LANE), lambda i: (i, 0)),
                   pl.BlockSpec((tm, LANE), lambda i: (i, 0))],
        out_shape=[jax.ShapeDtypeStruct((n, d), F32),
                   jax.ShapeDtypeStruct((n * PACK_ROWS, LANE), U32),
                   jax.ShapeDtypeStruct((n, LANE), F32)],
        compiler_params=_cparams(("arbitrary",)),
        name="merge_out",
    )(ya, yb, proj, proj, x2, wa, wb, wo, gt, g2, sc, sh, wr, br)


def _route_kernel(lg_ref, rt_ref, rw_ref, cnt_ref, carry_ref):
    @pl.when(pl.program_id(0) == 0)
    def _():
        carry_ref[...] = jnp.zeros_like(carry_ref)

    lg = lg_ref[...]
    tm = lg.shape[0]
    lane = lax.broadcasted_iota(I32, lg.shape, 1).astype(F32)
    big = float(2 * LANE)
    is_grp = lane < N_GROUPS
    gl = jnp.where(is_grp, lg, NEG)
    gm = jnp.max(gl, axis=-1, keepdims=True)
    ge = jnp.where(is_grp, jnp.exp(gl - gm), 0.0)
    pg = ge / jnp.sum(ge, axis=-1, keepdims=True)
    p_top = jnp.max(pg, axis=-1, keepdims=True)
    g_idx = jnp.min(jnp.where(is_grp & (pg == p_top), lane, big), axis=-1, keepdims=True)

    lo = N_GROUPS + EPG * g_idx
    in_grp = (lane >= lo) & (lane < lo + EPG)
    el = jnp.where(in_grp, lg, NEG)
    em = jnp.max(el, axis=-1, keepdims=True)
    ee = jnp.where(in_grp, jnp.exp(el - em), 0.0)
    pin = ee / jnp.sum(ee, axis=-1, keepdims=True)
    v1 = jnp.max(pin, axis=-1, keepdims=True)
    i1 = jnp.min(jnp.where(in_grp & (pin == v1), lane, big), axis=-1, keepdims=True)
    rest = in_grp & (lane != i1)
    p2 = jnp.where(rest, pin, -1.0)
    v2 = jnp.max(p2, axis=-1, keepdims=True)
    i2 = jnp.min(jnp.where(rest & (p2 == v2), lane, big), axis=-1, keepdims=True)
    vs = v1 + v2
    w1 = p_top * v1 / vs
    w2 = p_top * v2 / vs

    oh1 = lane == i1
    oh2 = lane == i2
    mm = (oh1 | oh2).astype(BF16)
    tri = (lax.broadcasted_iota(I32, (tm, tm), 0) > lax.broadcasted_iota(I32, (tm, tm), 1))
    cnt = jnp.dot(tri.astype(BF16), mm, preferred_element_type=F32) + carry_ref[...]
    r1 = jnp.sum(jnp.where(oh1, cnt, 0.0), axis=-1, keepdims=True)
    r2 = jnp.sum(jnp.where(oh2, cnt, 0.0), axis=-1, keepdims=True)
    carry_ref[...] += jnp.sum(mm.astype(F32), axis=0, keepdims=True)
    cnt_ref[...] = carry_ref[...]

    e1 = i1 - N_GROUPS
    e2 = i2 - N_GROUPS
    packed = jnp.where(lane == 0, e1, jnp.where(lane == 1, e2, jnp.where(
        lane == 2, r1, jnp.where(lane == 3, r2, jnp.where(lane == 4, w1, w2)))))
    rt_ref[...] = packed.T[:8, :]
    rw_ref[...] = jnp.where(lane == 0, w1, w2)


def _route(logits):
    n = logits.shape[0]
    tm = 512
    return pl.pallas_call(
        _route_kernel,
        grid=(n // tm,),
        in_specs=[pl.BlockSpec((tm, LANE), lambda i: (i, 0))],
        out_specs=[pl.BlockSpec((8, tm), lambda i: (0, i)),
                   pl.BlockSpec((tm, LANE), lambda i: (i, 0)),
                   pl.BlockSpec((1, LANE), lambda i: (0, 0))],
        out_shape=[jax.ShapeDtypeStruct((8, n), F32),
                   jax.ShapeDtypeStruct((n, LANE), F32),
                   jax.ShapeDtypeStruct((1, LANE), F32)],
        scratch_shapes=[pltpu.VMEM((1, LANE), F32)],
        compiler_params=_cparams(("arbitrary",)),
        name="route",
    )(logits)


PACK_ROWS = 8
HI_MASK = 0xFFFF0000


def _pack_store(val, ref):
    rows = val.shape[0]
    for c in range(PACK_ROWS):
        lo = val[:, 2 * c * LANE:(2 * c + 1) * LANE].astype(BF16).astype(F32)
        hi = val[:, (2 * c + 1) * LANE:(2 * c + 2) * LANE].astype(BF16).astype(F32)
        word = (pltpu.bitcast(hi, U32) & jnp.uint32(HI_MASK)) | (pltpu.bitcast(lo, U32) >> 16)
        ref[pl.ds(c, rows, stride=PACK_ROWS), :] = word


def _unpack_load(load_chunk):
    pieces = []
    for c in range(PACK_ROWS):
        w = load_chunk(c)
        pieces.append(pltpu.bitcast(w << 16, F32))
        pieces.append(pltpu.bitcast(w & jnp.uint32(HI_MASK), F32))
    return pieces


def _start_rows(src_ref, dst_ref, idx_ref, base, count, sem):
    for j in range(count):
        src_row = pl.multiple_of(idx_ref[0, 0, base + j] * PACK_ROWS, PACK_ROWS)
        pltpu.make_async_copy(src_ref.at[pl.ds(src_row, PACK_ROWS)],
                              dst_ref.at[pl.ds(j * PACK_ROWS, PACK_ROWS)], sem).start(priority=j % 2)


def _wait_rows(src_ref, dst_ref, sem):
    pltpu.make_async_copy(src_ref.at[pl.ds(0, dst_ref.shape[0])], dst_ref, sem).wait()


def _dispatch_kernel(dest_ref, h_ref, xs_in_ref, xs_ref, sem):
    del xs_in_ref
    tm = h_ref.shape[0] // PACK_ROWS
    for k in range(TOP_K):
        for j in range(tm):
            dst_row = pl.multiple_of(dest_ref[0, 0, k * tm + j] * PACK_ROWS, PACK_ROWS)
            pltpu.make_async_copy(h_ref.at[pl.ds(j * PACK_ROWS, PACK_ROWS)],
                                  xs_ref.at[pl.ds(dst_row, PACK_ROWS)],
                                  sem.at[k]).start(priority=j % 2)
    for k in range(TOP_K):
        pltpu.make_async_copy(h_ref, xs_ref.at[pl.ds(0, tm * PACK_ROWS)], sem.at[k]).wait()


def _dispatch(dest3, h_packed, xs0):
    tm = dest3.shape[2] // TOP_K
    nt = dest3.shape[0]
    return pl.pallas_call(
        _dispatch_kernel,
        grid=(nt,),
        in_specs=[pl.BlockSpec((1, 1, TOP_K * tm), lambda i: (i, 0, 0), memory_space=pltpu.SMEM),
                  pl.BlockSpec((tm * PACK_ROWS, LANE), lambda i: (i, 0)),
                  pl.BlockSpec(memory_space=pl.ANY)],
        out_specs=pl.BlockSpec(memory_space=pl.ANY),
        out_shape=jax.ShapeDtypeStruct(xs0.shape, xs0.dtype),
        scratch_shapes=[pltpu.SemaphoreType.DMA((TOP_K,))],
        input_output_aliases={2: 0},
        compiler_params=_cparams(("arbitrary",)),
        name="dispatch",
    )(dest3, h_packed, xs0)


def _expert_kernel(be_ref, nx_ref, nu_ref, x_ref, wgu_ref, wd_ref, o_ref,
                   wgu_f, wd_f, wgu_s, wd_s, wsem, *, layer):
    i = pl.program_id(0)
    n_used = nu_ref[0]
    tb = x_ref.shape[0] // PACK_ROWS

    def weight_copies(e):
        return (pltpu.make_async_copy(wgu_ref.at[layer, e], wgu_f, wsem.at[0]),
                pltpu.make_async_copy(wd_ref.at[layer, e], wd_f, wsem.at[1]))

    @pl.when(i == 0)
    def _():
        for cp in weight_copies(be_ref[0]):
            cp.start()

    @pl.when(i < n_used)
    def _():
        e = be_ref[i]
        prev = be_ref[jnp.maximum(i - 1, 0)]

        @pl.when((i == 0) | (e != prev))
        def _():
            for cp in weight_copies(e):
                cp.wait()
            wgu_s[...] = wgu_f[...].astype(BF16)
            wd_s[...] = wd_f[...].astype(BF16)

            @pl.when(nx_ref[i] >= 0)
            def _():
                for cp in weight_copies(nx_ref[i]):
                    cp.start()

        pieces = _unpack_load(lambda c: x_ref[pl.ds(c, tb, stride=PACK_ROWS), :])
        x = jnp.concatenate([p.astype(BF16) for p in pieces], axis=-1)
        gu = jnp.dot(x, wgu_s[...], preferred_element_type=F32)
        gate = gu[:, :D_EXPERT]
        up = gu[:, D_EXPERT:]
        act = (gate * jax.nn.sigmoid(gate) * up).astype(BF16)
        _pack_store(jnp.dot(act, wd_s[...], preferred_element_type=F32), o_ref)

    @pl.when(i >= n_used)
    def _():
        o_ref[...] = jnp.zeros_like(o_ref)


def _experts(block_e, next_e, n_used, xs, w_gu, w_down, layer):
    d = w_gu.shape[2]
    tile_rows = EXPERT_TILE * PACK_ROWS
    nb = xs.shape[0] // tile_rows

    grid_spec = pltpu.PrefetchScalarGridSpec(
        num_scalar_prefetch=3,
        grid=(nb,),
        in_specs=[
            pl.BlockSpec((tile_rows, LANE),
                         lambda i, be, nx, nu: (jnp.minimum(i, nu[0] - 1), 0)),
            pl.BlockSpec(memory_space=pl.ANY),
            pl.BlockSpec(memory_space=pl.ANY),
        ],
        out_specs=pl.BlockSpec((tile_rows, LANE), lambda i, be, nx, nu: (i, 0)),
        scratch_shapes=[pltpu.VMEM((d, 2 * D_EXPERT), F32), pltpu.VMEM((D_EXPERT, d), F32),
                        pltpu.VMEM((d, 2 * D_EXPERT), BF16), pltpu.VMEM((D_EXPERT, d), BF16),
                        pltpu.SemaphoreType.DMA((2,))],
    )
    return pl.pallas_call(
        functools.partial(_expert_kernel, layer=layer),
        grid_spec=grid_spec,
        out_shape=jax.ShapeDtypeStruct(xs.shape, U32),
        compiler_params=_cparams(("arbitrary",)),
        name="experts",
    )(block_e, next_e, n_used, xs, w_gu, w_down)


def _combine_kernel(dest0_ref, destn_ref, ys_ref, x_ref, rw_ref, gt_ref, fg_ref, o_ref, ybuf, sem,
                    *, final):
    i = pl.program_id(0)
    tm = x_ref.shape[0]

    def start(dest_ref, slot):
        for k in range(TOP_K):
            _start_rows(ys_ref, ybuf.at[slot, k], dest_ref, k * tm, tm, sem.at[slot, k])

    @pl.when(i == 0)
    def _():
        start(dest0_ref, 0)

    @pl.when(i + 1 < pl.num_programs(0))
    def _():
        start(destn_ref, (i + 1) % 2)

    cur = i % 2
    for k in range(TOP_K):
        _wait_rows(ys_ref, ybuf.at[cur, k], sem.at[cur, k])
    rw = rw_ref[...]
    y0 = _unpack_load(lambda c: ybuf[cur, 0, pl.ds(c, tm, stride=PACK_ROWS), :])
    y1 = _unpack_load(lambda c: ybuf[cur, 1, pl.ds(c, tm, stride=PACK_ROWS), :])
    y = jnp.concatenate([rw[:, 0:1] * a + rw[:, 1:2] * b for a, b in zip(y0, y1)], axis=-1)
    xn = x_ref[...] + (1.0 + gt_ref[0]) * y
    if final:
        xn = _rms(xn, fg_ref[...])
    o_ref[...] = xn


def _combine(dest3, ys, x2, rw, gt, fg, seq, final):
    n, d = x2.shape
    tm = dest3.shape[2] // TOP_K
    per_b = seq // tm
    nt = n // tm
    return pl.pallas_call(
        functools.partial(_combine_kernel, final=final),
        grid=(nt,),
        in_specs=[pl.BlockSpec((1, 1, TOP_K * tm), lambda i: (0, 0, 0), memory_space=pltpu.SMEM),
                  pl.BlockSpec((1, 1, TOP_K * tm), lambda i: (jnp.minimum(i + 1, nt - 1), 0, 0),
                               memory_space=pltpu.SMEM),
                  pl.BlockSpec(memory_space=pl.ANY),
                  pl.BlockSpec((tm, d), lambda i: (i, 0)),
                  pl.BlockSpec((tm, LANE), lambda i: (i, 0)),
                  pl.BlockSpec((1, 1, d), lambda i: (i // per_b, 0, 0)),
                  pl.BlockSpec((1, d), lambda i: (0, 0))],
        out_specs=pl.BlockSpec((tm, d), lambda i: (i, 0)),
        out_shape=jax.ShapeDtypeStruct((n, d), F32),
        scratch_shapes=[pltpu.VMEM((2, TOP_K, tm * PACK_ROWS, LANE), U32),
                        pltpu.SemaphoreType.DMA((2, TOP_K))],
        compiler_params=_cparams(("arbitrary",)),
        name="combine",
    )(dest3, dest3, ys, x2, rw, gt, fg)


def _rope_cols(w3):
    half = B_ROPE // 2
    x1, x2 = w3[..., :half], w3[..., half:]
    z = jnp.zeros(w3.shape[:-1] + (LANE - B_ROPE,), w3.dtype)
    a = jnp.concatenate([x1, x2, z], axis=-1)
    b = jnp.concatenate([x2, x1, z], axis=-1)
    return a, b


def _layout_w_in(w):
    d = w.shape[0]
    o = 0
    qa = w[:, o:o + A_WIDTH]; o += A_WIDTH
    ka = w[:, o:o + A_WIDTH]; o += A_WIDTH
    va = w[:, o:o + A_WIDTH]; o += A_WIDTH
    cq = w[:, o:o + Q_LORA]; o += Q_LORA
    ckv = w[:, o:o + KV_LORA]; o += KV_LORA
    kr = w[:, o:o + B_ROPE]; o += B_ROPE
    gates = w[:, o:]
    kra, krb = _rope_cols(kr.reshape(d, 1, B_ROPE))
    return jnp.concatenate([gates, qa, ka, va, cq, ckv, kra.reshape(d, LANE),
                            krb.reshape(d, LANE)], axis=1).astype(BF16)


def _layout_w_uq(w):
    k = w.shape[0]
    w3 = w.reshape(k, B_HEADS, B_QK)
    nope = w3[:, :, :B_NOPE].reshape(k, B_HEADS * LANE)
    a, b = _rope_cols(w3[:, :, B_NOPE:])
    return jnp.concatenate([nope, a.reshape(k, -1), b.reshape(k, -1)], axis=1).astype(BF16)


def _layout_w_ukv(w):
    k = w.shape[0]
    w3 = w.reshape(k, B_HEADS, B_NOPE + B_V)
    return jnp.concatenate([w3[:, :, :B_NOPE].reshape(k, -1),
                            w3[:, :, B_NOPE:].reshape(k, -1)], axis=1).astype(BF16)


def kernel(x, c, positions, ln1_g, ln2_g, w_ada, b_ada, w_in, q_norm_g, w_uq, kv_norm_g, w_ukv,
           w_a_up, w_b_up, w_o, w_grp, b_grp, w_exp, b_exp, w_gu, w_down, final_g):
    bsz, seq, d = x.shape
    depth = w_in.shape[0]
    n = bsz * seq
    n_blocks = n * TOP_K // EXPERT_TILE + N_EXPERTS
    n_slots = n_blocks * EXPERT_TILE
    tile_tok = 256

    mod = _ada(c, w_ada, b_ada).reshape(depth, bsz, N_MOD, 1, d)

    pos_col = positions.reshape(n, 1)
    ctab, stab = _rope_tables(pos_col)
    posc = positions.reshape(bsz, seq, 1)
    posr = [positions.reshape(bsz, seq // dil, dil).transpose(0, 2, 1) for _, dil in A_GROUPS]
    slopes = 2.0 ** (-ALIBI_MAX_BIAS * jnp.arange(1, A_HEADS + 1, dtype=F32) / A_HEADS)
    slopes = slopes.reshape(A_NG, A_HPG).T
    slopes = jnp.pad(slopes, ((0, 0), (0, 8 - A_NG)))
    slopes = jnp.broadcast_to(slopes[:, :, None], (A_HPG, 8, LANE))

    x2 = x.reshape(n, d)
    for l in range(depth):
        sh1, sc1, gt1, sh2, sc2, gt2 = (mod[l, :, m] for m in range(N_MOD))

        proj = _norm_proj(x2, ln1_g[l].reshape(1, d), sc1, sh1, _layout_w_in(w_in[l]), seq)
        q = _mla_q(proj, q_norm_g[l].reshape(1, Q_LORA), _layout_w_uq(w_uq[l]), ctab, stab)
        k, v = _mla_kv(proj, kv_norm_g[l].reshape(1, KV_LORA), _layout_w_ukv(w_ukv[l]), ctab, stab)
        yb = _mla_attn(q, k, v, bsz, seq)
        ya = _dilated_attn(proj.reshape(bsz, seq, PROJ_COLS), posc, posr, slopes).reshape(n, A_OUT)

        w_r = jnp.concatenate([w_grp[l], w_exp[l],
                               jnp.zeros((d, LANE - N_GROUPS - N_EXPERTS), F32)], axis=1)
        w_r_hi = w_r.astype(BF16)
        w_r = jnp.concatenate([w_r_hi, (w_r - w_r_hi.astype(F32)).astype(BF16)], axis=1)
        b_r = jnp.concatenate([b_grp[l], b_exp[l],
                               jnp.zeros((LANE - N_GROUPS - N_EXPERTS,), F32)]).reshape(1, LANE)
        x2, h2, logits = _merge_out(ya, yb, proj, x2, w_a_up[l].astype(BF16),
                                    w_b_up[l].astype(BF16), w_o[l].astype(BF16), gt1,
                                    ln2_g[l].reshape(1, d), sc2, sh2, w_r, b_r, seq)

        rt, rw, cnt = _route(logits)
        counts = cnt[0, N_GROUPS:N_GROUPS + N_EXPERTS].astype(I32)
        padded = (counts + EXPERT_TILE - 1) // EXPERT_TILE * EXPERT_TILE
        pends = jnp.cumsum(padded)
        pstarts = pends - padded
        eids = rt[:TOP_K].astype(I32)
        onehot = eids[:, :, None] == jnp.arange(N_EXPERTS, dtype=I32)
        dest = jnp.sum(jnp.where(onehot, pstarts, 0), axis=-1) + rt[TOP_K:2 * TOP_K].astype(I32)
        dest3 = dest.reshape(TOP_K, n // tile_tok, tile_tok).transpose(1, 0, 2).reshape(
            n // tile_tok, 1, TOP_K * tile_tok)
        n_used = (pends[-1:] // EXPERT_TILE).astype(I32)
        blk_start = jnp.arange(n_blocks, dtype=I32) * EXPERT_TILE
        block_e = jnp.minimum(jnp.sum((pends[None, :] <= blk_start[:, None]).astype(I32), axis=1),
                              N_EXPERTS - 1)

        eidx = jnp.arange(N_EXPERTS, dtype=I32)
        nxt_used = lax.cummin(jnp.where(counts > 0, eidx, N_EXPERTS), reverse=True)
        nxt_after = jnp.concatenate([nxt_used[1:], jnp.full((1,), N_EXPERTS, I32)])
        nxt_after = jnp.where(nxt_after >= N_EXPERTS, -1, nxt_after)
        next_e = jnp.sum(jnp.where(block_e[:, None] == eidx[None, :], nxt_after[None, :], 0),
                         axis=1).astype(I32)
        xs = _dispatch(dest3, h2, jnp.zeros((n_slots * PACK_ROWS, LANE), U32))
        ys = _experts(block_e, next_e, n_used, xs, w_gu, w_down, l)
        x2 = _combine(dest3, ys, x2, rw, gt2, final_g.reshape(1, d), seq, final=(l == depth - 1))
    return x2.reshape(bsz, seq, d)
```

```python
import functools

import jax
import jax.numpy as jnp
from jax import lax
from jax.experimental import pallas as pl
from jax.experimental.pallas import tpu as pltpu

F32 = jnp.float32
BF16 = jnp.bfloat16
I32 = jnp.int32
U32 = jnp.uint32
HIGHEST = lax.Precision.HIGHEST

HEAD_DIM = 128
A_GROUPS = ((128, 1), (512, 4), (2048, 16))
A_HPG = 4
A_NG = len(A_GROUPS)
A_HEADS = A_HPG * A_NG
A_WIDTH = A_HEADS * HEAD_DIM
A_OUT = A_HPG * HEAD_DIM
ALIBI_MAX_BIAS = 8.0
B_HEADS = 8
B_NOPE = 128
B_ROPE = 64
B_V = 128
B_QK = B_NOPE + B_ROPE
Q_LORA = 512
KV_LORA = 512
B_OUT = B_HEADS * B_V
ROPE_THETA = 10000.0
N_GROUPS = 8
EPG = 8
N_EXPERTS = N_GROUPS * EPG
TOP_K = 2
D_EXPERT = 512
N_MOD = 6
EPS = 1e-6
NEG = -1e30
LOG2E = 1.4426950408889634

LANE = 128
VMEM_LIMIT = 56 * 1024 * 1024
QBLK = 128
EXPERT_TILE = 256

COL_GA = 0
COL_GB = 16
COL_QA = 32
COL_KA = 44
COL_VA = 56
COL_CQ = 68
COL_CKV = 72
COL_KR = 76
N_COLBLK = 78
PROJ_COLS = N_COLBLK * LANE


def _cparams(sem):
    return pltpu.CompilerParams(dimension_semantics=sem, vmem_limit_bytes=VMEM_LIMIT)


def _rms(x, g):
    return x * lax.rsqrt(jnp.mean(x * x, axis=-1, keepdims=True) + EPS) * g


def _ada_kernel(c_ref, w_ref, b_ref, o_ref):
    c = c_ref[...]
    cs = c * jax.nn.sigmoid(c)
    o_ref[0] = jnp.dot(cs, w_ref[0], precision=HIGHEST, preferred_element_type=F32) + b_ref[0]


def _ada(c, w_ada, b_ada):
    nl, d, nm = w_ada.shape
    bsz = c.shape[0]
    tn = 1536
    return pl.pallas_call(
        _ada_kernel,
        grid=(nl, nm // tn),
        in_specs=[
            pl.BlockSpec((bsz, d), lambda l, j: (0, 0)),
            pl.BlockSpec((1, d, tn), lambda l, j: (l, 0, j)),
            pl.BlockSpec((1, 1, tn), lambda l, j: (l, 0, j)),
        ],
        out_specs=pl.BlockSpec((1, bsz, tn), lambda l, j: (l, 0, j)),
        out_shape=jax.ShapeDtypeStruct((nl, bsz, nm), F32),
        compiler_params=_cparams(("arbitrary", "arbitrary")),
        name="ada",
    )(c, w_ada, b_ada.reshape(nl, 1, nm))


def _rope_table_kernel(pos_ref, inv_ref, c_ref, s_ref):
    ang = pos_ref[...].astype(F32) * inv_ref[...]
    lane = lax.broadcasted_iota(I32, ang.shape, 1)
    half = B_ROPE // 2
    cs = jnp.cos(ang)
    sn = jnp.sin(ang)
    c_ref[...] = jnp.where(lane < B_ROPE, cs, 0.0)
    s_ref[...] = jnp.where(lane < half, -sn, jnp.where(lane < B_ROPE, sn, 0.0))


def _rope_tables(pos_col):
    n = pos_col.shape[0]
    half = B_ROPE // 2
    inv = ROPE_THETA ** (-jnp.arange(half, dtype=F32) / half)
    inv_row = jnp.concatenate([inv, inv, jnp.zeros((LANE - B_ROPE,), F32)]).reshape(1, LANE)
    tm = 2048
    return pl.pallas_call(
        _rope_table_kernel,
        grid=(n // tm,),
        in_specs=[pl.BlockSpec((tm, 1), lambda i: (i, 0)),
                  pl.BlockSpec((1, LANE), lambda i: (0, 0))],
        out_specs=[pl.BlockSpec((tm, LANE), lambda i: (i, 0)),
                   pl.BlockSpec((tm, LANE), lambda i: (i, 0))],
        out_shape=[jax.ShapeDtypeStruct((n, LANE), F32)] * 2,
        compiler_params=_cparams(("arbitrary",)),
        name="rope_tables",
    )(pos_col, inv_row)


def _norm_proj_kernel(x_ref, g_ref, sc_ref, sh_ref, w_ref, o_ref, h_ref):
    @pl.when(pl.program_id(1) == 0)
    def _():
        h = _rms(x_ref[...], g_ref[...]) * (1.0 + sc_ref[0]) + sh_ref[0]
        h_ref[...] = h.astype(BF16)

    o_ref[...] = jnp.dot(h_ref[...], w_ref[...], preferred_element_type=F32).astype(o_ref.dtype)


def _norm_proj(x2, g, sc, sh, w, seq):
    n, d = x2.shape
    ncol = w.shape[1]
    tm = 1024
    tn = 13 * LANE
    per_b = seq // tm
    return pl.pallas_call(
        _norm_proj_kernel,
        grid=(n // tm, ncol // tn),
        in_specs=[
            pl.BlockSpec((tm, d), lambda i, j: (i, 0)),
            pl.BlockSpec((1, d), lambda i, j: (0, 0)),
            pl.BlockSpec((1, 1, d), lambda i, j: (i // per_b, 0, 0)),
            pl.BlockSpec((1, 1, d), lambda i, j: (i // per_b, 0, 0)),
            pl.BlockSpec((d, tn), lambda i, j: (0, j)),
        ],
        out_specs=pl.BlockSpec((tm, tn), lambda i, j: (i, j)),
        out_shape=jax.ShapeDtypeStruct((n, ncol), BF16),
        scratch_shapes=[pltpu.VMEM((tm, d), BF16)],
        compiler_params=_cparams(("arbitrary", "arbitrary")),
        name="norm_proj",
    )(x2, g, sc, sh, w)


def _proj_kernel(h_ref, w_ref, o_ref):
    o_ref[...] = jnp.dot(h_ref[...], w_ref[...], preferred_element_type=F32).astype(o_ref.dtype)


def _proj(h, w):
    n, d = h.shape
    ncol = w.shape[1]
    tm = 2048
    tn = 13 * LANE
    return pl.pallas_call(
        _proj_kernel,
        grid=(n // tm, ncol // tn),
        in_specs=[pl.BlockSpec((tm, d), lambda i, j: (i, 0)),
                  pl.BlockSpec((d, tn), lambda i, j: (0, j))],
        out_specs=pl.BlockSpec((tm, tn), lambda i, j: (i, j)),
        out_shape=jax.ShapeDtypeStruct((n, ncol), BF16),
        compiler_params=_cparams(("arbitrary", "arbitrary")),
        name="proj",
    )(h, w)


def _mla_q_kernel(cq_ref, g_ref, w_ref, c_ref, s_ref, q_ref):
    h = _rms(cq_ref[...].astype(F32), g_ref[...]).astype(BF16)
    r = jnp.dot(h, w_ref[...], preferred_element_type=F32)
    scale = B_QK ** -0.5 * LOG2E
    c = c_ref[...]
    s = s_ref[...]
    nw = B_HEADS * LANE
    for hd in range(B_HEADS):
        nope = r[:, hd * LANE:(hd + 1) * LANE]
        ra = r[:, nw + hd * LANE: nw + (hd + 1) * LANE]
        rb = r[:, 2 * nw + hd * LANE: 2 * nw + (hd + 1) * LANE]
        q_ref[:, 2 * hd * LANE:(2 * hd + 1) * LANE] = (nope * scale).astype(BF16)
        q_ref[:, (2 * hd + 1) * LANE:(2 * hd + 2) * LANE] = ((ra * c + rb * s) * scale).astype(BF16)


def _mla_q(proj, g, w, ctab, stab):
    n = proj.shape[0]
    tm = 512
    return pl.pallas_call(
        _mla_q_kernel,
        grid=(n // tm,),
        in_specs=[
            pl.BlockSpec((tm, Q_LORA), lambda i: (i, COL_CQ * LANE // Q_LORA)),
            pl.BlockSpec((1, Q_LORA), lambda i: (0, 0)),
            pl.BlockSpec(w.shape, lambda i: (0, 0)),
            pl.BlockSpec((tm, LANE), lambda i: (i, 0)),
            pl.BlockSpec((tm, LANE), lambda i: (i, 0)),
        ],
        out_specs=pl.BlockSpec((tm, 2 * B_HEADS * LANE), lambda i: (i, 0)),
        out_shape=jax.ShapeDtypeStruct((n, 2 * B_HEADS * LANE), BF16),
        compiler_params=_cparams(("arbitrary",)),
        name="mla_q",
    )(proj, g, w, ctab, stab)


def _mla_kv_kernel(ckv_ref, kr_ref, g_ref, w_ref, c_ref, s_ref, k_ref, v_ref):
    h = _rms(ckv_ref[...].astype(F32), g_ref[...]).astype(BF16)
    r = jnp.dot(h, w_ref[...], preferred_element_type=F32)
    kr = kr_ref[...].astype(F32)
    krope = (kr[:, :LANE] * c_ref[...] + kr[:, LANE:] * s_ref[...]).astype(BF16)
    nw = B_HEADS * LANE
    for hd in range(B_HEADS):
        k_ref[:, 2 * hd * LANE:(2 * hd + 1) * LANE] = r[:, hd * LANE:(hd + 1) * LANE].astype(BF16)
        k_ref[:, (2 * hd + 1) * LANE:(2 * hd + 2) * LANE] = krope
    v_ref[...] = r[:, nw:].astype(BF16)


def _mla_kv(proj, g, w, ctab, stab):
    n = proj.shape[0]
    tm = 512
    return pl.pallas_call(
        _mla_kv_kernel,
        grid=(n // tm,),
        in_specs=[
            pl.BlockSpec((tm, KV_LORA), lambda i: (i, COL_CKV * LANE // KV_LORA)),
            pl.BlockSpec((tm, 2 * LANE), lambda i: (i, COL_KR // 2)),
            pl.BlockSpec((1, KV_LORA), lambda i: (0, 0)),
            pl.BlockSpec(w.shape, lambda i: (0, 0)),
            pl.BlockSpec((tm, LANE), lambda i: (i, 0)),
            pl.BlockSpec((tm, LANE), lambda i: (i, 0)),
        ],
        out_specs=[pl.BlockSpec((tm, 2 * B_HEADS * LANE), lambda i: (i, 0)),
                   pl.BlockSpec((tm, B_HEADS * LANE), lambda i: (i, 0))],
        out_shape=[jax.ShapeDtypeStruct((n, 2 * B_HEADS * LANE), BF16),
                   jax.ShapeDtypeStruct((n, B_HEADS * LANE), BF16)],
        compiler_params=_cparams(("arbitrary",)),
        name="mla_kv",
    )(proj, proj, g, w, ctab, stab)


MLA_SUB = 512


MLA_HPS = 2


def _mla_attn_kernel(q_ref, k_ref, v_ref, o_ref, vx_ref):
    for hh in range(MLA_HPS):
        vx_ref[hh, :, :LANE] = v_ref[:, hh * LANE:(hh + 1) * LANE]
        vx_ref[hh, :, LANE:] = jnp.ones((vx_ref.shape[1], LANE), BF16)

    for hh in range(MLA_HPS):
        cols = slice(2 * hh * LANE, 2 * (hh + 1) * LANE)
        for j in range(q_ref.shape[0] // MLA_SUB):
            rows = slice(j * MLA_SUB, (j + 1) * MLA_SUB)
            s = lax.dot_general(q_ref[rows, cols], k_ref[:, cols], (((1,), (1,)), ((), ())),
                                preferred_element_type=F32)
            m = jnp.max(s, axis=-1, keepdims=True)
            p = jnp.exp2(s - m).astype(BF16)
            ox = jnp.dot(p, vx_ref[hh], preferred_element_type=F32)
            o_ref[rows, hh * LANE:(hh + 1) * LANE] = (
                ox[:, :LANE] / ox[:, LANE:LANE + 1]).astype(o_ref.dtype)


def _mla_attn(q, k, v, bsz, seq):
    n = q.shape[0]
    qk_w = 2 * LANE * MLA_HPS
    return pl.pallas_call(
        _mla_attn_kernel,
        grid=(bsz, B_HEADS // MLA_HPS),
        in_specs=[
            pl.BlockSpec((seq, qk_w), lambda b, h: (b, h)),
            pl.BlockSpec((seq, qk_w), lambda b, h: (b, h)),
            pl.BlockSpec((seq, LANE * MLA_HPS), lambda b, h: (b, h)),
        ],
        out_specs=pl.BlockSpec((seq, LANE * MLA_HPS), lambda b, h: (b, h)),
        out_shape=jax.ShapeDtypeStruct((n, B_HEADS * LANE), BF16),
        scratch_shapes=[pltpu.VMEM((MLA_HPS, seq, 2 * LANE), BF16)],
        compiler_params=_cparams(("arbitrary", "arbitrary")),
        name="mla_attn",
    )(q, k, v)


def _dilated_kernel(*refs, seq):
    qkv = refs[:3 * A_NG]
    posc_ref = refs[3 * A_NG]
    posr = refs[3 * A_NG + 1: 3 * A_NG + 1 + A_NG]
    sl_ref = refs[3 * A_NG + 1 + A_NG]
    o_ref = refs[3 * A_NG + 2 + A_NG]
    qf, kf, vf, og, lg = refs[3 * A_NG + 3 + A_NG:]
    scale = HEAD_DIM ** -0.5

    for g, (win, dil) in enumerate(A_GROUPS):
        radius = win // (2 * dil)
        length = seq // dil
        kw = min(2 * QBLK, length)
        q_in, k_in, v_in = qkv[3 * g], qkv[3 * g + 1], qkv[3 * g + 2]
        slope2 = sl_ref[0, g:g + 1, 0:1] * LOG2E
        if dil > 1:
            qf[...] = q_in[0].astype(F32)
            kf[...] = k_in[0].astype(F32)
            vf[...] = v_in[0].astype(F32)

        def rows(start, size, dil=dil):
            if dil == 1:
                return pl.ds(start, size)
            return pl.ds(start, size, stride=dil)

        def load(ref_in, ref_f32, rws, dil=dil):
            if dil == 1:
                return ref_in[0, rws, :]
            return ref_f32[rws, :].astype(BF16)

        units = []
        for r in range(dil):
            for ib in range(length // QBLK):
                i0 = ib * QBLK
                ws = min(max(i0 - radius, 0), length - kw)
                units.append((r, i0, ws, rows(r + dil * i0, QBLK), rows(r + dil * ws, kw)))
        qs = [(load(q_in, qf, u[3]).astype(F32) * (scale * LOG2E)).astype(BF16) for u in units]
        ks = [load(k_in, kf, u[4]) for u in units]
        biased = []
        for (r, i0, ws, q_rows, _), q, k in zip(units, qs, ks):
            s = lax.dot_general(q, k, (((1,), (1,)), ((), ())), preferred_element_type=F32)
            pq = posc_ref[0, q_rows, :]
            pk = posr[g][0, r:r + 1, ws:ws + kw]
            dist = jnp.abs(pq - pk).astype(F32)
            ii = i0 + lax.broadcasted_iota(I32, (QBLK, kw), 0)
            jj = ws + lax.broadcasted_iota(I32, (QBLK, kw), 1)
            biased.append(jnp.where(jnp.abs(ii - jj) <= radius, s - slope2 * dist, NEG))
        ms = [jnp.max(s, axis=-1, keepdims=True) for s in biased]
        ps, dens = [], []
        for s, m in zip(biased, ms):
            p = jnp.exp2(s - m)
            dens.append(jnp.sum(p, axis=-1, keepdims=True))
            ps.append(p.astype(BF16))
        vs = [load(v_in, vf, u[4]) for u in units]
        outs = [jnp.dot(p, v, preferred_element_type=F32) / den
                for p, v, den in zip(ps, vs, dens)]
        for u, o, m, den in zip(units, outs, ms, dens):
            og[g, u[3], :] = o
            lg[g, u[3], :] = jnp.broadcast_to(m * (1.0 / LOG2E) + jnp.log(den), (QBLK, HEAD_DIM))

    chunk = 256

    def merge(ci, carry):
        rs = pl.ds(pl.multiple_of(ci * chunk, chunk), chunk)
        l0, l1, l2 = lg[0, rs, :], lg[1, rs, :], lg[2, rs, :]
        m = jnp.maximum(jnp.maximum(l0, l1), l2)
        w0, w1, w2 = jnp.exp(l0 - m), jnp.exp(l1 - m), jnp.exp(l2 - m)
        o = (w0 * og[0, rs, :] + w1 * og[1, rs, :] + w2 * og[2, rs, :]) / (w0 + w1 + w2)
        o_ref[0, rs, :] = o.astype(o_ref.dtype)
        return carry

    lax.fori_loop(0, seq // chunk, merge, 0)


def _dilated_attn(proj3, posc, posr, slopes):
    bsz, seq, _ = proj3.shape
    in_specs = []
    args = []
    for g in range(A_NG):
        for base in (COL_QA, COL_KA, COL_VA):
            in_specs.append(pl.BlockSpec(
                (1, seq, LANE), lambda b, h, base=base, g=g: (b, 0, base + A_HPG * g + h)))
            args.append(proj3)
    in_specs.append(pl.BlockSpec((1, seq, 1), lambda b, h: (b, 0, 0)))
    args.append(posc)
    for g, (_, dil) in enumerate(A_GROUPS):
        in_specs.append(pl.BlockSpec((1, dil, seq // dil), lambda b, h: (b, 0, 0)))
        args.append(posr[g])
    in_specs.append(pl.BlockSpec((1, 8, LANE), lambda b, h: (h, 0, 0)))
    args.append(slopes)
    return pl.pallas_call(
        functools.partial(_dilated_kernel, seq=seq),
        grid=(bsz, A_HPG),
        in_specs=in_specs,
        out_specs=pl.BlockSpec((1, seq, LANE), lambda b, h: (b, 0, h)),
        out_shape=jax.ShapeDtypeStruct((bsz, seq, A_OUT), BF16),
        scratch_shapes=[pltpu.VMEM((seq, HEAD_DIM), F32)] * 3
        + [pltpu.VMEM((A_NG, seq, HEAD_DIM), F32)] * 2,
        compiler_params=_cparams(("arbitrary", "arbitrary")),
        name="dilated_attn",
    )(*args)


def _merge_out_kernel(ya_ref, yb_ref, ga_ref, gb_ref, x_ref, wa_ref, wb_ref, wo_ref, gt_ref,
                      g2_ref, sc_ref, sh_ref, wr_ref, br_ref, xo_ref, h_ref, lg_ref):
    a = jnp.dot(ya_ref[...], wa_ref[...], preferred_element_type=F32)
    b = jnp.dot(yb_ref[...], wb_ref[...], preferred_element_type=F32)
    ga = jax.nn.sigmoid(ga_ref[...].astype(F32))
    gb = jax.nn.sigmoid(gb_ref[...].astype(F32))
    merged = (ga * a + gb * b).astype(BF16)
    xn = x_ref[...] + (1.0 + gt_ref[0]) * jnp.dot(merged, wo_ref[...], preferred_element_type=F32)
    xo_ref[...] = xn
    h = _rms(xn, g2_ref[...]) * (1.0 + sc_ref[0]) + sh_ref[0]
    _pack_store(h, h_ref)
    h_hi = h.astype(BF16)
    h_lo = (h - h_hi.astype(F32)).astype(BF16)
    t = jnp.dot(h_hi, wr_ref[...], preferred_element_type=F32)
    u = jnp.dot(h_lo, wr_ref[:, :LANE], preferred_element_type=F32)
    lg_ref[...] = t[:, :LANE] + t[:, LANE:] + u + br_ref[...]


def _merge_out(ya, yb, proj, x2, wa, wb, wo, gt, g2, sc, sh, wr, br, seq):
    n, d = x2.shape
    tm = 512
    per_b = seq // tm
    const = lambda i: (0, 0)
    bidx = lambda i: (i // per_b, 0, 0)
    once = pl.Buffered(1)
    return pl.pallas_call(
        _merge_out_kernel,
        grid=(n // tm,),
        in_specs=[
            pl.BlockSpec((tm, A_OUT), lambda i: (i, 0)),
            pl.BlockSpec((tm, B_OUT), lambda i: (i, 0)),
            pl.BlockSpec((tm, d), lambda i: (i, COL_GA * LANE // d)),
            pl.BlockSpec((tm, d), lambda i: (i, COL_GB * LANE // d)),
            pl.BlockSpec((tm, d), lambda i: (i, 0)),
            pl.BlockSpec(wa.shape, const, pipeline_mode=once),
            pl.BlockSpec(wb.shape, const, pipeline_mode=once),
            pl.BlockSpec(wo.shape, const, pipeline_mode=once),
            pl.BlockSpec((1, 1, d), bidx),
            pl.BlockSpec((1, d), const),
            pl.BlockSpec((1, 1, d), bidx),
            pl.BlockSpec((1, 1, d), bidx),
            pl.BlockSpec(wr.shape, const),
            pl.BlockSpec((1, LANE), const),
        ],
        out_specs=[pl.BlockSpec((tm, d), lambda i: (i, 0)),
                   pl.BlockSpec((tm * PACK_ROWS, LANE), lambda i: (i, 0)),
                   pl.BlockSpec((tm, LANE), lambda i: (i, 0))],
        out_shape=[jax.ShapeDtypeStruct((n, d), F32),
                   jax.ShapeDtypeStruct((n * PACK_ROWS, LANE), U32),
                   jax.ShapeDtypeStruct((n, LANE), F32)],
        compiler_params=_cparams(("arbitrary",)),
        name="merge_out",
    )(ya, yb, proj, proj, x2, wa, wb, wo, gt, g2, sc, sh, wr, br)


def _route_kernel(lg_ref, rt_ref, rw_ref, cnt_ref, carry_ref):
    @pl.when(pl.program_id(0) == 0)
    def _():
        carry_ref[...] = jnp.zeros_like(carry_ref)

    lg = lg_ref[...]
    tm = lg.shape[0]
    lane = lax.broadcasted_iota(I32, lg.shape, 1).astype(F32)
    big = float(2 * LANE)
    is_grp = lane < N_GROUPS
    gl = jnp.where(is_grp, lg, NEG)
    gm = jnp.max(gl, axis=-1, keepdims=True)
    ge = jnp.where(is_grp, jnp.exp(gl - gm), 0.0)
    pg = ge / jnp.sum(ge, axis=-1, keepdims=True)
    p_top = jnp.max(pg, axis=-1, keepdims=True)
    g_idx = jnp.min(jnp.where(is_grp & (pg == p_top), lane, big), axis=-1, keepdims=True)

    lo = N_GROUPS + EPG * g_idx
    in_grp = (lane >= lo) & (lane < lo + EPG)
    el = jnp.where(in_grp, lg, NEG)
    em = jnp.max(el, axis=-1, keepdims=True)
    ee = jnp.where(in_grp, jnp.exp(el - em), 0.0)
    pin = ee / jnp.sum(ee, axis=-1, keepdims=True)
    v1 = jnp.max(pin, axis=-1, keepdims=True)
    i1 = jnp.min(jnp.where(in_grp & (pin == v1), lane, big), axis=-1, keepdims=True)
    rest = in_grp & (lane != i1)
    p2 = jnp.where(rest, pin, -1.0)
    v2 = jnp.max(p2, axis=-1, keepdims=True)
    i2 = jnp.min(jnp.where(rest & (p2 == v2), lane, big), axis=-1, keepdims=True)
    vs = v1 + v2
    w1 = p_top * v1 / vs
    w2 = p_top * v2 / vs

    oh1 = lane == i1
    oh2 = lane == i2
    mm = (oh1 | oh2).astype(BF16)
    tri = (lax.broadcasted_iota(I32, (tm, tm), 0) > lax.broadcasted_iota(I32, (tm, tm), 1))
    cnt = jnp.dot(tri.astype(BF16), mm, preferred_element_type=F32) + carry_ref[...]
    r1 = jnp.sum(jnp.where(oh1, cnt, 0.0), axis=-1, keepdims=True)
    r2 = jnp.sum(jnp.where(oh2, cnt, 0.0), axis=-1, keepdims=True)
    carry_ref[...] += jnp.sum(mm.astype(F32), axis=0, keepdims=True)
    cnt_ref[...] = carry_ref[...]

    e1 = i1 - N_GROUPS
    e2 = i2 - N_GROUPS
    packed = jnp.where(lane == 0, e1, jnp.where(lane == 1, e2, jnp.where(
        lane == 2, r1, jnp.where(lane == 3, r2, jnp.where(lane == 4, w1, w2)))))
    rt_ref[...] = packed.T[:8, :]
    rw_ref[...] = jnp.where(lane == 0, w1, w2)


def _route(logits):
    n = logits.shape[0]
    tm = 512
    return pl.pallas_call(
        _route_kernel,
        grid=(n // tm,),
        in_specs=[pl.BlockSpec((tm, LANE), lambda i: (i, 0))],
        out_specs=[pl.BlockSpec((8, tm), lambda i: (0, i)),
                   pl.BlockSpec((tm, LANE), lambda i: (i, 0)),
                   pl.BlockSpec((1, LANE), lambda i: (0, 0))],
        out_shape=[jax.ShapeDtypeStruct((8, n), F32),
                   jax.ShapeDtypeStruct((n, LANE), F32),
                   jax.ShapeDtypeStruct((1, LANE), F32)],
        scratch_shapes=[pltpu.VMEM((1, LANE), F32)],
        compiler_params=_cparams(("arbitrary",)),
        name="route",
    )(logits)


PACK_ROWS = 8
HI_MASK = 0xFFFF0000


def _pack_store(val, ref):
    rows = val.shape[0]
    for c in range(PACK_ROWS):
        lo = val[:, 2 * c * LANE:(2 * c + 1) * LANE].astype(BF16).astype(F32)
        hi = val[:, (2 * c + 1) * LANE:(2 * c + 2) * LANE].astype(BF16).astype(F32)
        word = (pltpu.bitcast(hi, U32) & jnp.uint32(HI_MASK)) | (pltpu.bitcast(lo, U32) >> 16)
        ref[pl.ds(c, rows, stride=PACK_ROWS), :] = word


def _unpack_load(load_chunk):
    pieces = []
    for c in range(PACK_ROWS):
        w = load_chunk(c)
        pieces.append(pltpu.bitcast(w << 16, F32))
        pieces.append(pltpu.bitcast(w & jnp.uint32(HI_MASK), F32))
    return pieces


def _start_rows(src_ref, dst_ref, idx_ref, base, count, sem):
    for j in range(count):
        src_row = pl.multiple_of(idx_ref[0, 0, base + j] * PACK_ROWS, PACK_ROWS)
        pltpu.make_async_copy(src_ref.at[pl.ds(src_row, PACK_ROWS)],
                              dst_ref.at[pl.ds(j * PACK_ROWS, PACK_ROWS)], sem).start(priority=j % 2)


def _wait_rows(src_ref, dst_ref, sem):
    pltpu.make_async_copy(src_ref.at[pl.ds(0, dst_ref.shape[0])], dst_ref, sem).wait()


def _dispatch_kernel(dest_ref, h_ref, xs_in_ref, xs_ref, sem):
    del xs_in_ref
    tm = h_ref.shape[0] // PACK_ROWS
    for k in range(TOP_K):
        for j in range(tm):
            dst_row = pl.multiple_of(dest_ref[0, 0, k * tm + j] * PACK_ROWS, PACK_ROWS)
            pltpu.make_async_copy(h_ref.at[pl.ds(j * PACK_ROWS, PACK_ROWS)],
                                  xs_ref.at[pl.ds(dst_row, PACK_ROWS)],
                                  sem.at[k]).start(priority=j % 2)
    for k in range(TOP_K):
        pltpu.make_async_copy(h_ref, xs_ref.at[pl.ds(0, tm * PACK_ROWS)], sem.at[k]).wait()


def _dispatch(dest3, h_packed, xs0):
    tm = dest3.shape[2] // TOP_K
    nt = dest3.shape[0]
    return pl.pallas_call(
        _dispatch_kernel,
        grid=(nt,),
        in_specs=[pl.BlockSpec((1, 1, TOP_K * tm), lambda i: (i, 0, 0), memory_space=pltpu.SMEM),
                  pl.BlockSpec((tm * PACK_ROWS, LANE), lambda i: (i, 0)),
                  pl.BlockSpec(memory_space=pl.ANY)],
        out_specs=pl.BlockSpec(memory_space=pl.ANY),
        out_shape=jax.ShapeDtypeStruct(xs0.shape, xs0.dtype),
        scratch_shapes=[pltpu.SemaphoreType.DMA((TOP_K,))],
        input_output_aliases={2: 0},
        compiler_params=_cparams(("arbitrary",)),
        name="dispatch",
    )(dest3, h_packed, xs0)


def _expert_kernel(be_ref, nx_ref, nu_ref, x_ref, wgu_ref, wd_ref, o_ref,
                   wgu_f, wd_f, wgu_s, wd_s, wsem, *, layer):
    i = pl.program_id(0)
    n_used = nu_ref[0]
    tb = x_ref.shape[0] // PACK_ROWS

    def weight_copies(e):
        return (pltpu.make_async_copy(wgu_ref.at[layer, e], wgu_f, wsem.at[0]),
                pltpu.make_async_copy(wd_ref.at[layer, e], wd_f, wsem.at[1]))

    @pl.when(i == 0)
    def _():
        for cp in weight_copies(be_ref[0]):
            cp.start()

    @pl.when(i < n_used)
    def _():
        e = be_ref[i]
        prev = be_ref[jnp.maximum(i - 1, 0)]

        @pl.when((i == 0) | (e != prev))
        def _():
            for cp in weight_copies(e):
                cp.wait()
            wgu_s[...] = wgu_f[...].astype(BF16)
            wd_s[...] = wd_f[...].astype(BF16)

            @pl.when(nx_ref[i] >= 0)
            def _():
                for cp in weight_copies(nx_ref[i]):
                    cp.start()

        pieces = _unpack_load(lambda c: x_ref[pl.ds(c, tb, stride=PACK_ROWS), :])
        x = jnp.concatenate([p.astype(BF16) for p in pieces], axis=-1)
        gu = jnp.dot(x, wgu_s[...], preferred_element_type=F32)
        gate = gu[:, :D_EXPERT]
        up = gu[:, D_EXPERT:]
        act = (gate * jax.nn.sigmoid(gate) * up).astype(BF16)
        _pack_store(jnp.dot(act, wd_s[...], preferred_element_type=F32), o_ref)

    @pl.when(i >= n_used)
    def _():
        o_ref[...] = jnp.zeros_like(o_ref)


def _experts(block_e, next_e, n_used, xs, w_gu, w_down, layer):
    d = w_gu.shape[2]
    tile_rows = EXPERT_TILE * PACK_ROWS
    nb = xs.shape[0] // tile_rows

    grid_spec = pltpu.PrefetchScalarGridSpec(
        num_scalar_prefetch=3,
        grid=(nb,),
        in_specs=[
            pl.BlockSpec((tile_rows, LANE),
                         lambda i, be, nx, nu: (jnp.minimum(i, nu[0] - 1), 0)),
            pl.BlockSpec(memory_space=pl.ANY),
            pl.BlockSpec(memory_space=pl.ANY),
        ],
        out_specs=pl.BlockSpec((tile_rows, LANE), lambda i, be, nx, nu: (i, 0)),
        scratch_shapes=[pltpu.VMEM((d, 2 * D_EXPERT), F32), pltpu.VMEM((D_EXPERT, d), F32),
                        pltpu.VMEM((d, 2 * D_EXPERT), BF16), pltpu.VMEM((D_EXPERT, d), BF16),
                        pltpu.SemaphoreType.DMA((2,))],
    )
    return pl.pallas_call(
        functools.partial(_expert_kernel, layer=layer),
        grid_spec=grid_spec,
        out_shape=jax.ShapeDtypeStruct(xs.shape, U32),
        compiler_params=_cparams(("arbitrary",)),
        name="experts",
    )(block_e, next_e, n_used, xs, w_gu, w_down)


def _combine_kernel(dest0_ref, destn_ref, ys_ref, x_ref, rw_ref, gt_ref, g_ref, sc_ref, sh_ref,
                    *rest, final):
    if final:
        o_ref, ybuf, sem = rest
    else:
        o_ref, hn_ref, ybuf, sem = rest
    i = pl.program_id(0)
    tm = x_ref.shape[0]

    def start(dest_ref, slot):
        for k in range(TOP_K):
            _start_rows(ys_ref, ybuf.at[slot, k], dest_ref, k * tm, tm, sem.at[slot, k])

    @pl.when(i == 0)
    def _():
        start(dest0_ref, 0)

    @pl.when(i + 1 < pl.num_programs(0))
    def _():
        start(destn_ref, (i + 1) % 2)

    cur = i % 2
    for k in range(TOP_K):
        _wait_rows(ys_ref, ybuf.at[cur, k], sem.at[cur, k])
    rw = rw_ref[...]
    y0 = _unpack_load(lambda c: ybuf[cur, 0, pl.ds(c, tm, stride=PACK_ROWS), :])
    y1 = _unpack_load(lambda c: ybuf[cur, 1, pl.ds(c, tm, stride=PACK_ROWS), :])
    y = jnp.concatenate([rw[:, 0:1] * a + rw[:, 1:2] * b for a, b in zip(y0, y1)], axis=-1)
    xn = x_ref[...] + (1.0 + gt_ref[0]) * y
    if final:
        o_ref[...] = _rms(xn, g_ref[...])
    else:
        o_ref[...] = xn
        hn_ref[...] = (_rms(xn, g_ref[...]) * (1.0 + sc_ref[0]) + sh_ref[0]).astype(BF16)


def _combine(dest3, ys, x2, rw, gt, g, sc, sh, seq, final):
    n, d = x2.shape
    tm = dest3.shape[2] // TOP_K
    per_b = seq // tm
    nt = n // tm
    bidx = lambda i: (i // per_b, 0, 0)
    row = pl.BlockSpec((tm, d), lambda i: (i, 0))
    out_specs = row if final else [row, row]
    out_shape = jax.ShapeDtypeStruct((n, d), F32)
    if not final:
        out_shape = [out_shape, jax.ShapeDtypeStruct((n, d), BF16)]
    return pl.pallas_call(
        functools.partial(_combine_kernel, final=final),
        grid=(nt,),
        in_specs=[pl.BlockSpec((1, 1, TOP_K * tm), lambda i: (0, 0, 0), memory_space=pltpu.SMEM),
                  pl.BlockSpec((1, 1, TOP_K * tm), lambda i: (jnp.minimum(i + 1, nt - 1), 0, 0),
                               memory_space=pltpu.SMEM),
                  pl.BlockSpec(memory_space=pl.ANY),
                  row,
                  pl.BlockSpec((tm, LANE), lambda i: (i, 0)),
                  pl.BlockSpec((1, 1, d), bidx),
                  pl.BlockSpec((1, d), lambda i: (0, 0)),
                  pl.BlockSpec((1, 1, d), bidx),
                  pl.BlockSpec((1, 1, d), bidx)],
        out_specs=out_specs,
        out_shape=out_shape,
        scratch_shapes=[pltpu.VMEM((2, TOP_K, tm * PACK_ROWS, LANE), U32),
                        pltpu.SemaphoreType.DMA((2, TOP_K))],
        compiler_params=_cparams(("arbitrary",)),
        name="combine",
    )(dest3, dest3, ys, x2, rw, gt, g, sc, sh)


def _rope_cols(w3):
    half = B_ROPE // 2
    x1, x2 = w3[..., :half], w3[..., half:]
    z = jnp.zeros(w3.shape[:-1] + (LANE - B_ROPE,), w3.dtype)
    a = jnp.concatenate([x1, x2, z], axis=-1)
    b = jnp.concatenate([x2, x1, z], axis=-1)
    return a, b


def _layout_w_in(w):
    d = w.shape[0]
    o = 0
    qa = w[:, o:o + A_WIDTH]; o += A_WIDTH
    ka = w[:, o:o + A_WIDTH]; o += A_WIDTH
    va = w[:, o:o + A_WIDTH]; o += A_WIDTH
    cq = w[:, o:o + Q_LORA]; o += Q_LORA
    ckv = w[:, o:o + KV_LORA]; o += KV_LORA
    kr = w[:, o:o + B_ROPE]; o += B_ROPE
    gates = w[:, o:]
    kra, krb = _rope_cols(kr.reshape(d, 1, B_ROPE))
    return jnp.concatenate([gates, qa, ka, va, cq, ckv, kra.reshape(d, LANE),
                            krb.reshape(d, LANE)], axis=1).astype(BF16)


def _layout_w_uq(w):
    k = w.shape[0]
    w3 = w.reshape(k, B_HEADS, B_QK)
    nope = w3[:, :, :B_NOPE].reshape(k, B_HEADS * LANE)
    a, b = _rope_cols(w3[:, :, B_NOPE:])
    return jnp.concatenate([nope, a.reshape(k, -1), b.reshape(k, -1)], axis=1).astype(BF16)


def _layout_w_ukv(w):
    k = w.shape[0]
    w3 = w.reshape(k, B_HEADS, B_NOPE + B_V)
    return jnp.concatenate([w3[:, :, :B_NOPE].reshape(k, -1),
                            w3[:, :, B_NOPE:].reshape(k, -1)], axis=1).astype(BF16)


def kernel(x, c, positions, ln1_g, ln2_g, w_ada, b_ada, w_in, q_norm_g, w_uq, kv_norm_g, w_ukv,
           w_a_up, w_b_up, w_o, w_grp, b_grp, w_exp, b_exp, w_gu, w_down, final_g):
    bsz, seq, d = x.shape
    depth = w_in.shape[0]
    n = bsz * seq
    n_blocks = n * TOP_K // EXPERT_TILE + N_EXPERTS
    n_slots = n_blocks * EXPERT_TILE
    tile_tok = 256

    mod = _ada(c, w_ada, b_ada).reshape(depth, bsz, N_MOD, 1, d)

    pos_col = positions.reshape(n, 1)
    ctab, stab = _rope_tables(pos_col)
    posc = positions.reshape(bsz, seq, 1)
    posr = [positions.reshape(bsz, seq // dil, dil).transpose(0, 2, 1) for _, dil in A_GROUPS]
    slopes = 2.0 ** (-ALIBI_MAX_BIAS * jnp.arange(1, A_HEADS + 1, dtype=F32) / A_HEADS)
    slopes = slopes.reshape(A_NG, A_HPG).T
    slopes = jnp.pad(slopes, ((0, 0), (0, 8 - A_NG)))
    slopes = jnp.broadcast_to(slopes[:, :, None], (A_HPG, 8, LANE))

    x2 = x.reshape(n, d)
    for l in range(depth):
        sh1, sc1, gt1, sh2, sc2, gt2 = (mod[l, :, m] for m in range(N_MOD))

        if l == 0:
            proj = _norm_proj(x2, ln1_g[l].reshape(1, d), sc1, sh1, _layout_w_in(w_in[l]), seq)
        else:
            proj = _proj(h_next, _layout_w_in(w_in[l]))
        q = _mla_q(proj, q_norm_g[l].reshape(1, Q_LORA), _layout_w_uq(w_uq[l]), ctab, stab)
        k, v = _mla_kv(proj, kv_norm_g[l].reshape(1, KV_LORA), _layout_w_ukv(w_ukv[l]), ctab, stab)
        yb = _mla_attn(q, k, v, bsz, seq)
        ya = _dilated_attn(proj.reshape(bsz, seq, PROJ_COLS), posc, posr, slopes).reshape(n, A_OUT)

        w_r = jnp.concatenate([w_grp[l], w_exp[l],
                               jnp.zeros((d, LANE - N_GROUPS - N_EXPERTS), F32)], axis=1)
        w_r_hi = w_r.astype(BF16)
        w_r = jnp.concatenate([w_r_hi, (w_r - w_r_hi.astype(F32)).astype(BF16)], axis=1)
        b_r = jnp.concatenate([b_grp[l], b_exp[l],
                               jnp.zeros((LANE - N_GROUPS - N_EXPERTS,), F32)]).reshape(1, LANE)
        x2, h2, logits = _merge_out(ya, yb, proj, x2, w_a_up[l].astype(BF16),
                                    w_b_up[l].astype(BF16), w_o[l].astype(BF16), gt1,
                                    ln2_g[l].reshape(1, d), sc2, sh2, w_r, b_r, seq)

        rt, rw, cnt = _route(logits)
        counts = cnt[0, N_GROUPS:N_GROUPS + N_EXPERTS].astype(I32)
        padded = (counts + EXPERT_TILE - 1) // EXPERT_TILE * EXPERT_TILE
        pends = jnp.cumsum(padded)
        pstarts = pends - padded
        eids = rt[:TOP_K].astype(I32)
        onehot = eids[:, :, None] == jnp.arange(N_EXPERTS, dtype=I32)
        dest = jnp.sum(jnp.where(onehot, pstarts, 0), axis=-1) + rt[TOP_K:2 * TOP_K].astype(I32)
        dest3 = dest.reshape(TOP_K, n // tile_tok, tile_tok).transpose(1, 0, 2).reshape(
            n // tile_tok, 1, TOP_K * tile_tok)
        n_used = (pends[-1:] // EXPERT_TILE).astype(I32)
        blk_start = jnp.arange(n_blocks, dtype=I32) * EXPERT_TILE
        block_e = jnp.minimum(jnp.sum((pends[None, :] <= blk_start[:, None]).astype(I32), axis=1),
                              N_EXPERTS - 1)

        eidx = jnp.arange(N_EXPERTS, dtype=I32)
        nxt_used = lax.cummin(jnp.where(counts > 0, eidx, N_EXPERTS), reverse=True)
        nxt_after = jnp.concatenate([nxt_used[1:], jnp.full((1,), N_EXPERTS, I32)])
        nxt_after = jnp.where(nxt_after >= N_EXPERTS, -1, nxt_after)
        next_e = jnp.sum(jnp.where(block_e[:, None] == eidx[None, :], nxt_after[None, :], 0),
                         axis=1).astype(I32)
        xs = _dispatch(dest3, h2, jnp.zeros((n_slots * PACK_ROWS, LANE), U32))
        ys = _experts(block_e, next_e, n_used, xs, w_gu, w_down, l)
        if l == depth - 1:
            x2 = _combine(dest3, ys, x2, rw, gt2, final_g.reshape(1, d), gt2, gt2, seq, final=True)
        else:
            x2, h_next = _combine(dest3, ys, x2, rw, gt2, ln1_g[l + 1].reshape(1, d),
                                  mod[l + 1, :, 1], mod[l + 1, :, 0], seq, final=False)
    return x2.reshape(bsz, seq, d)
```

```python
import functools

import jax
import jax.numpy as jnp
from jax import lax
from jax.experimental import pallas as pl
from jax.experimental.pallas import tpu as pltpu

F32 = jnp.float32
BF16 = jnp.bfloat16
I32 = jnp.int32
U32 = jnp.uint32
HIGHEST = lax.Precision.HIGHEST

HEAD_DIM = 128
A_GROUPS = ((128, 1), (512, 4), (2048, 16))
A_HPG = 4
A_NG = len(A_GROUPS)
A_HEADS = A_HPG * A_NG
A_WIDTH = A_HEADS * HEAD_DIM
A_OUT = A_HPG * HEAD_DIM
ALIBI_MAX_BIAS = 8.0
B_HEADS = 8
B_NOPE = 128
B_ROPE = 64
B_V = 128
B_QK = B_NOPE + B_ROPE
Q_LORA = 512
KV_LORA = 512
B_OUT = B_HEADS * B_V
ROPE_THETA = 10000.0
N_GROUPS = 8
EPG = 8
N_EXPERTS = N_GROUPS * EPG
TOP_K = 2
D_EXPERT = 512
N_MOD = 6
EPS = 1e-6
NEG = -1e30
LOG2E = 1.4426950408889634

LANE = 128
VMEM_LIMIT = 56 * 1024 * 1024
QBLK = 128
EXPERT_TILE = 256

COL_GA = 0
COL_GB = 16
COL_QA = 32
COL_KA = 44
COL_VA = 56
COL_CQ = 68
COL_CKV = 72
COL_KR = 76
N_COLBLK = 78
PROJ_COLS = N_COLBLK * LANE


def _cparams(sem):
    return pltpu.CompilerParams(dimension_semantics=sem, vmem_limit_bytes=VMEM_LIMIT)


def _rms(x, g):
    return x * lax.rsqrt(jnp.mean(x * x, axis=-1, keepdims=True) + EPS) * g


def _ada_kernel(c_ref, w_ref, b_ref, o_ref):
    c = c_ref[...]
    cs = c * jax.nn.sigmoid(c)
    o_ref[0] = jnp.dot(cs, w_ref[0], precision=HIGHEST, preferred_element_type=F32) + b_ref[0]


def _ada(c, w_ada, b_ada):
    nl, d, nm = w_ada.shape
    bsz = c.shape[0]
    tn = 1536
    return pl.pallas_call(
        _ada_kernel,
        grid=(nl, nm // tn),
        in_specs=[
            pl.BlockSpec((bsz, d), lambda l, j: (0, 0)),
            pl.BlockSpec((1, d, tn), lambda l, j: (l, 0, j)),
            pl.BlockSpec((1, 1, tn), lambda l, j: (l, 0, j)),
        ],
        out_specs=pl.BlockSpec((1, bsz, tn), lambda l, j: (l, 0, j)),
        out_shape=jax.ShapeDtypeStruct((nl, bsz, nm), F32),
        compiler_params=_cparams(("arbitrary", "arbitrary")),
        name="ada",
    )(c, w_ada, b_ada.reshape(nl, 1, nm))


def _rope_table_kernel(pos_ref, inv_ref, c_ref, s_ref):
    ang = pos_ref[...].astype(F32) * inv_ref[...]
    lane = lax.broadcasted_iota(I32, ang.shape, 1)
    half = B_ROPE // 2
    cs = jnp.cos(ang)
    sn = jnp.sin(ang)
    c_ref[...] = jnp.where(lane < B_ROPE, cs, 0.0)
    s_ref[...] = jnp.where(lane < half, -sn, jnp.where(lane < B_ROPE, sn, 0.0))


def _rope_tables(pos_col):
    n = pos_col.shape[0]
    half = B_ROPE // 2
    inv = ROPE_THETA ** (-jnp.arange(half, dtype=F32) / half)
    inv_row = jnp.concatenate([inv, inv, jnp.zeros((LANE - B_ROPE,), F32)]).reshape(1, LANE)
    tm = 2048
    return pl.pallas_call(
        _rope_table_kernel,
        grid=(n // tm,),
        in_specs=[pl.BlockSpec((tm, 1), lambda i: (i, 0)),
                  pl.BlockSpec((1, LANE), lambda i: (0, 0))],
        out_specs=[pl.BlockSpec((tm, LANE), lambda i: (i, 0)),
                   pl.BlockSpec((tm, LANE), lambda i: (i, 0))],
        out_shape=[jax.ShapeDtypeStruct((n, LANE), F32)] * 2,
        compiler_params=_cparams(("arbitrary",)),
        name="rope_tables",
    )(pos_col, inv_row)


def _norm_proj_kernel(x_ref, g_ref, sc_ref, sh_ref, w_ref, o_ref, h_ref):
    @pl.when(pl.program_id(1) == 0)
    def _():
        h = _rms(x_ref[...], g_ref[...]) * (1.0 + sc_ref[0]) + sh_ref[0]
        h_ref[...] = h.astype(BF16)

    o_ref[...] = jnp.dot(h_ref[...], w_ref[...], preferred_element_type=F32).astype(o_ref.dtype)


def _norm_proj(x2, g, sc, sh, w, seq):
    n, d = x2.shape
    ncol = w.shape[1]
    tm = 1024
    tn = 13 * LANE
    per_b = seq // tm
    return pl.pallas_call(
        _norm_proj_kernel,
        grid=(n // tm, ncol // tn),
        in_specs=[
            pl.BlockSpec((tm, d), lambda i, j: (i, 0)),
            pl.BlockSpec((1, d), lambda i, j: (0, 0)),
            pl.BlockSpec((1, 1, d), lambda i, j: (i // per_b, 0, 0)),
            pl.BlockSpec((1, 1, d), lambda i, j: (i // per_b, 0, 0)),
            pl.BlockSpec((d, tn), lambda i, j: (0, j)),
        ],
        out_specs=pl.BlockSpec((tm, tn), lambda i, j: (i, j)),
        out_shape=jax.ShapeDtypeStruct((n, ncol), BF16),
        scratch_shapes=[pltpu.VMEM((tm, d), BF16)],
        compiler_params=_cparams(("arbitrary", "arbitrary")),
        name="norm_proj",
    )(x2, g, sc, sh, w)


def _proj_kernel(h_ref, w_ref, o_ref):
    o_ref[...] = jnp.dot(h_ref[...], w_ref[...], preferred_element_type=F32).astype(o_ref.dtype)


def _proj(h, w):
    n, d = h.shape
    ncol = w.shape[1]
    tm = 2048
    tn = 13 * LANE
    return pl.pallas_call(
        _proj_kernel,
        grid=(n // tm, ncol // tn),
        in_specs=[pl.BlockSpec((tm, d), lambda i, j: (i, 0)),
                  pl.BlockSpec((d, tn), lambda i, j: (0, j))],
        out_specs=pl.BlockSpec((tm, tn), lambda i, j: (i, j)),
        out_shape=jax.ShapeDtypeStruct((n, ncol), BF16),
        compiler_params=_cparams(("arbitrary", "arbitrary")),
        name="proj",
    )(h, w)


def _mla_q_kernel(cq_ref, g_ref, w_ref, c_ref, s_ref, q_ref):
    h = _rms(cq_ref[...].astype(F32), g_ref[...]).astype(BF16)
    r = jnp.dot(h, w_ref[...], preferred_element_type=F32)
    scale = B_QK ** -0.5 * LOG2E
    c = c_ref[...]
    s = s_ref[...]
    nw = B_HEADS * LANE
    for hd in range(B_HEADS):
        nope = r[:, hd * LANE:(hd + 1) * LANE]
        ra = r[:, nw + hd * LANE: nw + (hd + 1) * LANE]
        rb = r[:, 2 * nw + hd * LANE: 2 * nw + (hd + 1) * LANE]
        q_ref[:, 2 * hd * LANE:(2 * hd + 1) * LANE] = (nope * scale).astype(BF16)
        q_ref[:, (2 * hd + 1) * LANE:(2 * hd + 2) * LANE] = ((ra * c + rb * s) * scale).astype(BF16)


def _mla_q(proj, g, w, ctab, stab):
    n = proj.shape[0]
    tm = 512
    return pl.pallas_call(
        _mla_q_kernel,
        grid=(n // tm,),
        in_specs=[
            pl.BlockSpec((tm, Q_LORA), lambda i: (i, COL_CQ * LANE // Q_LORA)),
            pl.BlockSpec((1, Q_LORA), lambda i: (0, 0)),
            pl.BlockSpec(w.shape, lambda i: (0, 0)),
            pl.BlockSpec((tm, LANE), lambda i: (i, 0)),
            pl.BlockSpec((tm, LANE), lambda i: (i, 0)),
        ],
        out_specs=pl.BlockSpec((tm, 2 * B_HEADS * LANE), lambda i: (i, 0)),
        out_shape=jax.ShapeDtypeStruct((n, 2 * B_HEADS * LANE), BF16),
        compiler_params=_cparams(("arbitrary",)),
        name="mla_q",
    )(proj, g, w, ctab, stab)


def _mla_kv_kernel(ckv_ref, kr_ref, g_ref, w_ref, c_ref, s_ref, k_ref, v_ref):
    h = _rms(ckv_ref[...].astype(F32), g_ref[...]).astype(BF16)
    r = jnp.dot(h, w_ref[...], preferred_element_type=F32)
    kr = kr_ref[...].astype(F32)
    krope = (kr[:, :LANE] * c_ref[...] + kr[:, LANE:] * s_ref[...]).astype(BF16)
    nw = B_HEADS * LANE
    for hd in range(B_HEADS):
        k_ref[:, 2 * hd * LANE:(2 * hd + 1) * LANE] = r[:, hd * LANE:(hd + 1) * LANE].astype(BF16)
        k_ref[:, (2 * hd + 1) * LANE:(2 * hd + 2) * LANE] = krope
    v_ref[...] = r[:, nw:].astype(BF16)


def _mla_kv(proj, g, w, ctab, stab):
    n = proj.shape[0]
    tm = 512
    return pl.pallas_call(
        _mla_kv_kernel,
        grid=(n // tm,),
        in_specs=[
            pl.BlockSpec((tm, KV_LORA), lambda i: (i, COL_CKV * LANE // KV_LORA)),
            pl.BlockSpec((tm, 2 * LANE), lambda i: (i, COL_KR // 2)),
            pl.BlockSpec((1, KV_LORA), lambda i: (0, 0)),
            pl.BlockSpec(w.shape, lambda i: (0, 0)),
            pl.BlockSpec((tm, LANE), lambda i: (i, 0)),
            pl.BlockSpec((tm, LANE), lambda i: (i, 0)),
        ],
        out_specs=[pl.BlockSpec((tm, 2 * B_HEADS * LANE), lambda i: (i, 0)),
                   pl.BlockSpec((tm, B_HEADS * LANE), lambda i: (i, 0))],
        out_shape=[jax.ShapeDtypeStruct((n, 2 * B_HEADS * LANE), BF16),
                   jax.ShapeDtypeStruct((n, B_HEADS * LANE), BF16)],
        compiler_params=_cparams(("arbitrary",)),
        name="mla_kv",
    )(proj, proj, g, w, ctab, stab)


MLA_SUB = 512


MLA_HPS = 2


def _mla_attn_kernel(q_ref, k_ref, v_ref, o_ref, vx_ref):
    for hh in range(MLA_HPS):
        vx_ref[hh, :, :LANE] = v_ref[:, hh * LANE:(hh + 1) * LANE]
        vx_ref[hh, :, LANE:] = jnp.ones((vx_ref.shape[1], LANE), BF16)

    for hh in range(MLA_HPS):
        cols = slice(2 * hh * LANE, 2 * (hh + 1) * LANE)
        for j in range(q_ref.shape[0] // MLA_SUB):
            rows = slice(j * MLA_SUB, (j + 1) * MLA_SUB)
            s = lax.dot_general(q_ref[rows, cols], k_ref[:, cols], (((1,), (1,)), ((), ())),
                                preferred_element_type=F32)
            m = jnp.max(s, axis=-1, keepdims=True)
            p = jnp.exp2(s - m).astype(BF16)
            ox = jnp.dot(p, vx_ref[hh], preferred_element_type=F32)
            o_ref[rows, hh * LANE:(hh + 1) * LANE] = (
                ox[:, :LANE] / ox[:, LANE:LANE + 1]).astype(o_ref.dtype)


def _mla_attn(q, k, v, bsz, seq):
    n = q.shape[0]
    qk_w = 2 * LANE * MLA_HPS
    return pl.pallas_call(
        _mla_attn_kernel,
        grid=(bsz, B_HEADS // MLA_HPS),
        in_specs=[
            pl.BlockSpec((seq, qk_w), lambda b, h: (b, h)),
            pl.BlockSpec((seq, qk_w), lambda b, h: (b, h)),
            pl.BlockSpec((seq, LANE * MLA_HPS), lambda b, h: (b, h)),
        ],
        out_specs=pl.BlockSpec((seq, LANE * MLA_HPS), lambda b, h: (b, h)),
        out_shape=jax.ShapeDtypeStruct((n, B_HEADS * LANE), BF16),
        scratch_shapes=[pltpu.VMEM((MLA_HPS, seq, 2 * LANE), BF16)],
        compiler_params=_cparams(("arbitrary", "arbitrary")),
        name="mla_attn",
    )(q, k, v)


def _dilated_kernel(*refs, seq):
    qkv = refs[:3 * A_NG]
    posc_ref = refs[3 * A_NG]
    posr = refs[3 * A_NG + 1: 3 * A_NG + 1 + A_NG]
    sl_ref = refs[3 * A_NG + 1 + A_NG]
    o_ref = refs[3 * A_NG + 2 + A_NG]
    qf, kf, vf, og, lg = refs[3 * A_NG + 3 + A_NG:]
    scale = HEAD_DIM ** -0.5

    for g, (win, dil) in enumerate(A_GROUPS):
        radius = win // (2 * dil)
        length = seq // dil
        kw = min(2 * QBLK, length)
        q_in, k_in, v_in = qkv[3 * g], qkv[3 * g + 1], qkv[3 * g + 2]
        slope2 = sl_ref[0, g:g + 1, 0:1] * LOG2E
        if dil > 1:
            qf[...] = q_in[0].astype(F32)
            kf[...] = k_in[0].astype(F32)
            vf[...] = v_in[0].astype(F32)

        def rows(start, size, dil=dil):
            if dil == 1:
                return pl.ds(start, size)
            return pl.ds(start, size, stride=dil)

        def load(ref_in, ref_f32, rws, dil=dil):
            if dil == 1:
                return ref_in[0, rws, :]
            return ref_f32[rws, :].astype(BF16)

        units = []
        for r in range(dil):
            for ib in range(length // QBLK):
                i0 = ib * QBLK
                ws = min(max(i0 - radius, 0), length - kw)
                units.append((r, i0, ws, rows(r + dil * i0, QBLK), rows(r + dil * ws, kw)))
        qs = [(load(q_in, qf, u[3]).astype(F32) * (scale * LOG2E)).astype(BF16) for u in units]
        ks = [load(k_in, kf, u[4]) for u in units]
        biased = []
        for (r, i0, ws, q_rows, _), q, k in zip(units, qs, ks):
            s = lax.dot_general(q, k, (((1,), (1,)), ((), ())), preferred_element_type=F32)
            pq = posc_ref[0, q_rows, :]
            pk = posr[g][0, r:r + 1, ws:ws + kw]
            dist = jnp.abs(pq - pk).astype(F32)
            ii = i0 + lax.broadcasted_iota(I32, (QBLK, kw), 0)
            jj = ws + lax.broadcasted_iota(I32, (QBLK, kw), 1)
            biased.append(jnp.where(jnp.abs(ii - jj) <= radius, s - slope2 * dist, NEG))
        ms = [jnp.max(s, axis=-1, keepdims=True) for s in biased]
        ps, dens = [], []
        for s, m in zip(biased, ms):
            p = jnp.exp2(s - m)
            dens.append(jnp.sum(p, axis=-1, keepdims=True))
            ps.append(p.astype(BF16))
        vs = [load(v_in, vf, u[4]) for u in units]
        outs = [jnp.dot(p, v, preferred_element_type=F32) / den
                for p, v, den in zip(ps, vs, dens)]
        for u, o, m, den in zip(units, outs, ms, dens):
            og[g, u[3], :] = o
            lg[g, u[3], :] = jnp.broadcast_to(m * (1.0 / LOG2E) + jnp.log(den), (QBLK, HEAD_DIM))

    chunk = 256

    def merge(ci, carry):
        rs = pl.ds(pl.multiple_of(ci * chunk, chunk), chunk)
        l0, l1, l2 = lg[0, rs, :], lg[1, rs, :], lg[2, rs, :]
        m = jnp.maximum(jnp.maximum(l0, l1), l2)
        w0, w1, w2 = jnp.exp(l0 - m), jnp.exp(l1 - m), jnp.exp(l2 - m)
        o = (w0 * og[0, rs, :] + w1 * og[1, rs, :] + w2 * og[2, rs, :]) / (w0 + w1 + w2)
        o_ref[0, rs, :] = o.astype(o_ref.dtype)
        return carry

    lax.fori_loop(0, seq // chunk, merge, 0)


def _dilated_attn(proj3, posc, posr, slopes):
    bsz, seq, _ = proj3.shape
    in_specs = []
    args = []
    for g in range(A_NG):
        for base in (COL_QA, COL_KA, COL_VA):
            in_specs.append(pl.BlockSpec(
                (1, seq, LANE), lambda b, h, base=base, g=g: (b, 0, base + A_HPG * g + h)))
            args.append(proj3)
    in_specs.append(pl.BlockSpec((1, seq, 1), lambda b, h: (b, 0, 0)))
    args.append(posc)
    for g, (_, dil) in enumerate(A_GROUPS):
        in_specs.append(pl.BlockSpec((1, dil, seq // dil), lambda b, h: (b, 0, 0)))
        args.append(posr[g])
    in_specs.append(pl.BlockSpec((1, 8, LANE), lambda b, h: (h, 0, 0)))
    args.append(slopes)
    return pl.pallas_call(
        functools.partial(_dilated_kernel, seq=seq),
        grid=(bsz, A_HPG),
        in_specs=in_specs,
        out_specs=pl.BlockSpec((1, seq, LANE), lambda b, h: (b, 0, h)),
        out_shape=jax.ShapeDtypeStruct((bsz, seq, A_OUT), BF16),
        scratch_shapes=[pltpu.VMEM((seq, HEAD_DIM), F32)] * 3
        + [pltpu.VMEM((A_NG, seq, HEAD_DIM), F32)] * 2,
        compiler_params=_cparams(("arbitrary", "arbitrary")),
        name="dilated_attn",
    )(*args)


def _merge_out_kernel(ya_ref, yb_ref, ga_ref, gb_ref, x_ref, wa_ref, wb_ref, wo_ref, gt_ref,
                      g2_ref, sc_ref, sh_ref, wr_ref, br_ref, xo_ref, h_ref, lg_ref):
    a = jnp.dot(ya_ref[...], wa_ref[...], preferred_element_type=F32)
    b = jnp.dot(yb_ref[...], wb_ref[...], preferred_element_type=F32)
    ga = jax.nn.sigmoid(ga_ref[...].astype(F32))
    gb = jax.nn.sigmoid(gb_ref[...].astype(F32))
    merged = (ga * a + gb * b).astype(BF16)
    xn = x_ref[...] + (1.0 + gt_ref[0]) * jnp.dot(merged, wo_ref[...], preferred_element_type=F32)
    xo_ref[...] = xn
    h = _rms(xn, g2_ref[...]) * (1.0 + sc_ref[0]) + sh_ref[0]
    _pack_store(h, h_ref)
    h_hi = h.astype(BF16)
    h_lo = (h - h_hi.astype(F32)).astype(BF16)
    t = jnp.dot(h_hi, wr_ref[...], preferred_element_type=F32)
    u = jnp.dot(h_lo, wr_ref[:, :LANE], preferred_element_type=F32)
    lg_ref[...] = t[:, :LANE] + t[:, LANE:] + u + br_ref[...]


def _merge_out(ya, yb, proj, x2, wa, wb, wo, gt, g2, sc, sh, wr, br, seq):
    n, d = x2.shape
    tm = 512
    per_b = seq // tm
    const = lambda i: (0, 0)
    bidx = lambda i: (i // per_b, 0, 0)
    once = pl.Buffered(1)
    return pl.pallas_call(
        _merge_out_kernel,
        grid=(n // tm,),
        in_specs=[
            pl.BlockSpec((tm, A_OUT), lambda i: (i, 0)),
            pl.BlockSpec((tm, B_OUT), lambda i: (i, 0)),
            pl.BlockSpec((tm, d), lambda i: (i, COL_GA * LANE // d)),
            pl.BlockSpec((tm, d), lambda i: (i, COL_GB * LANE // d)),
            pl.BlockSpec((tm, d), lambda i: (i, 0)),
            pl.BlockSpec(wa.shape, const, pipeline_mode=once),
            pl.BlockSpec(wb.shape, const, pipeline_mode=once),
            pl.BlockSpec(wo.shape, const, pipeline_mode=once),
            pl.BlockSpec((1, 1, d), bidx),
            pl.BlockSpec((1, d), const),
            pl.BlockSpec((1, 1, d), bidx),
            pl.BlockSpec((1, 1, d), bidx),
            pl.BlockSpec(wr.shape, const),
            pl.BlockSpec((1, LANE), const),
        ],
        out_specs=[pl.BlockSpec((tm, d), lambda i: (i, 0)),
                   pl.BlockSpec((tm * PACK_ROWS, LANE), lambda i: (i, 0)),
                   pl.BlockSpec((tm, LANE), lambda i: (i, 0))],
        out_shape=[jax.ShapeDtypeStruct((n, d), F32),
                   jax.ShapeDtypeStruct((n * PACK_ROWS, LANE), U32),
                   jax.ShapeDtypeStruct((n, LANE), F32)],
        compiler_params=_cparams(("arbitrary",)),
        name="merge_out",
    )(ya, yb, proj, proj, x2, wa, wb, wo, gt, g2, sc, sh, wr, br)


def _route_kernel(lg_ref, rt_ref, rw_ref, cnt_ref, carry_ref):
    @pl.when(pl.program_id(0) == 0)
    def _():
        carry_ref[...] = jnp.zeros_like(carry_ref)

    lg = lg_ref[...]
    tm = lg.shape[0]
    lane = lax.broadcasted_iota(I32, lg.shape, 1).astype(F32)
    big = float(2 * LANE)
    is_grp = lane < N_GROUPS
    gl = jnp.where(is_grp, lg, NEG)
    gm = jnp.max(gl, axis=-1, keepdims=True)
    ge = jnp.where(is_grp, jnp.exp(gl - gm), 0.0)
    pg = ge / jnp.sum(ge, axis=-1, keepdims=True)
    p_top = jnp.max(pg, axis=-1, keepdims=True)
    g_idx = jnp.min(jnp.where(is_grp & (pg == p_top), lane, big), axis=-1, keepdims=True)

    lo = N_GROUPS + EPG * g_idx
    in_grp = (lane >= lo) & (lane < lo + EPG)
    el = jnp.where(in_grp, lg, NEG)
    em = jnp.max(el, axis=-1, keepdims=True)
    ee = jnp.where(in_grp, jnp.exp(el - em), 0.0)
    pin = ee / jnp.sum(ee, axis=-1, keepdims=True)
    v1 = jnp.max(pin, axis=-1, keepdims=True)
    i1 = jnp.min(jnp.where(in_grp & (pin == v1), lane, big), axis=-1, keepdims=True)
    rest = in_grp & (lane != i1)
    p2 = jnp.where(rest, pin, -1.0)
    v2 = jnp.max(p2, axis=-1, keepdims=True)
    i2 = jnp.min(jnp.where(rest & (p2 == v2), lane, big), axis=-1, keepdims=True)
    vs = v1 + v2
    w1 = p_top * v1 / vs
    w2 = p_top * v2 / vs

    oh1 = lane == i1
    oh2 = lane == i2
    mm = (oh1 | oh2).astype(BF16)
    tri = (lax.broadcasted_iota(I32, (tm, tm), 0) > lax.broadcasted_iota(I32, (tm, tm), 1))
    cnt = jnp.dot(tri.astype(BF16), mm, preferred_element_type=F32) + carry_ref[...]
    r1 = jnp.sum(jnp.where(oh1, cnt, 0.0), axis=-1, keepdims=True)
    r2 = jnp.sum(jnp.where(oh2, cnt, 0.0), axis=-1, keepdims=True)
    carry_ref[...] += jnp.sum(mm.astype(F32), axis=0, keepdims=True)
    cnt_ref[...] = carry_ref[...]

    e1 = i1 - N_GROUPS
    e2 = i2 - N_GROUPS
    packed = jnp.where(lane == 0, e1, jnp.where(lane == 1, e2, jnp.where(
        lane == 2, r1, jnp.where(lane == 3, r2, jnp.where(lane == 4, w1, w2)))))
    rt_ref[...] = packed.T[:8, :]
    rw_ref[...] = jnp.where(lane == 0, w1, w2)


def _route(logits):
    n = logits.shape[0]
    tm = 512
    return pl.pallas_call(
        _route_kernel,
        grid=(n // tm,),
        in_specs=[pl.BlockSpec((tm, LANE), lambda i: (i, 0))],
        out_specs=[pl.BlockSpec((8, tm), lambda i: (0, i)),
                   pl.BlockSpec((tm, LANE), lambda i: (i, 0)),
                   pl.BlockSpec((1, LANE), lambda i: (0, 0))],
        out_shape=[jax.ShapeDtypeStruct((8, n), F32),
                   jax.ShapeDtypeStruct((n, LANE), F32),
                   jax.ShapeDtypeStruct((1, LANE), F32)],
        scratch_shapes=[pltpu.VMEM((1, LANE), F32)],
        compiler_params=_cparams(("arbitrary",)),
        name="route",
    )(logits)


PACK_ROWS = 8
HI_MASK = 0xFFFF0000
WEIGHT_SLABS = 4


def _pack_store(val, ref):
    rows = val.shape[0]
    for c in range(PACK_ROWS):
        lo = val[:, 2 * c * LANE:(2 * c + 1) * LANE].astype(BF16).astype(F32)
        hi = val[:, (2 * c + 1) * LANE:(2 * c + 2) * LANE].astype(BF16).astype(F32)
        word = (pltpu.bitcast(hi, U32) & jnp.uint32(HI_MASK)) | (pltpu.bitcast(lo, U32) >> 16)
        ref[pl.ds(c, rows, stride=PACK_ROWS), :] = word


def _unpack_load(load_chunk):
    pieces = []
    for c in range(PACK_ROWS):
        w = load_chunk(c)
        pieces.append(pltpu.bitcast(w << 16, F32))
        pieces.append(pltpu.bitcast(w & jnp.uint32(HI_MASK), F32))
    return pieces


def _start_rows(src_ref, dst_ref, idx_ref, base, count, sem):
    for j in range(count):
        src_row = pl.multiple_of(idx_ref[0, 0, base + j] * PACK_ROWS, PACK_ROWS)
        pltpu.make_async_copy(src_ref.at[pl.ds(src_row, PACK_ROWS)],
                              dst_ref.at[pl.ds(j * PACK_ROWS, PACK_ROWS)], sem).start(priority=j % 2)


def _wait_rows(src_ref, dst_ref, sem):
    pltpu.make_async_copy(src_ref.at[pl.ds(0, dst_ref.shape[0])], dst_ref, sem).wait()


def _dispatch_kernel(dest_ref, h_ref, xs_in_ref, xs_ref, sem):
    del xs_in_ref
    tm = h_ref.shape[0] // PACK_ROWS
    for k in range(TOP_K):
        for j in range(tm):
            dst_row = pl.multiple_of(dest_ref[0, 0, k * tm + j] * PACK_ROWS, PACK_ROWS)
            pltpu.make_async_copy(h_ref.at[pl.ds(j * PACK_ROWS, PACK_ROWS)],
                                  xs_ref.at[pl.ds(dst_row, PACK_ROWS)],
                                  sem.at[k]).start(priority=j % 2)
    for k in range(TOP_K):
        pltpu.make_async_copy(h_ref, xs_ref.at[pl.ds(0, tm * PACK_ROWS)], sem.at[k]).wait()


def _dispatch(dest3, h_packed, xs0):
    tm = dest3.shape[2] // TOP_K
    nt = dest3.shape[0]
    return pl.pallas_call(
        _dispatch_kernel,
        grid=(nt,),
        in_specs=[pl.BlockSpec((1, 1, TOP_K * tm), lambda i: (i, 0, 0), memory_space=pltpu.SMEM),
                  pl.BlockSpec((tm * PACK_ROWS, LANE), lambda i: (i, 0)),
                  pl.BlockSpec(memory_space=pl.ANY)],
        out_specs=pl.BlockSpec(memory_space=pl.ANY),
        out_shape=jax.ShapeDtypeStruct(xs0.shape, xs0.dtype),
        scratch_shapes=[pltpu.SemaphoreType.DMA((TOP_K,))],
        input_output_aliases={2: 0},
        compiler_params=_cparams(("arbitrary",)),
        name="dispatch",
    )(dest3, h_packed, xs0)


def _expert_kernel(be_ref, nx_ref, nu_ref, x_ref, wgu_ref, wd_ref, o_ref,
                   wgu_f, wd_f, wgu_s, wd_s, wsem, *, layer):
    i = pl.program_id(0)
    n_used = nu_ref[0]
    tb = x_ref.shape[0] // PACK_ROWS

    def weight_copies(e):
        cps = []
        for src, dst, s in ((wgu_ref, wgu_f, wsem.at[0]), (wd_ref, wd_f, wsem.at[1])):
            rows = dst.shape[0] // WEIGHT_SLABS
            for c in range(WEIGHT_SLABS):
                cps.append(pltpu.make_async_copy(src.at[layer, e, pl.ds(c * rows, rows)],
                                                 dst.at[pl.ds(c * rows, rows)], s))
        return cps

    @pl.when(i == 0)
    def _():
        for cp in weight_copies(be_ref[0]):
            cp.start()

    @pl.when(i < n_used)
    def _():
        e = be_ref[i]
        prev = be_ref[jnp.maximum(i - 1, 0)]

        @pl.when((i == 0) | (e != prev))
        def _():
            for cp in weight_copies(e):
                cp.wait()
            wgu_s[...] = wgu_f[...].astype(BF16)
            wd_s[...] = wd_f[...].astype(BF16)

            @pl.when(nx_ref[i] >= 0)
            def _():
                for cp in weight_copies(nx_ref[i]):
                    cp.start()

        pieces = _unpack_load(lambda c: x_ref[pl.ds(c, tb, stride=PACK_ROWS), :])
        x = jnp.concatenate([p.astype(BF16) for p in pieces], axis=-1)
        gu = jnp.dot(x, wgu_s[...], preferred_element_type=F32)
        gate = gu[:, :D_EXPERT]
        up = gu[:, D_EXPERT:]
        act = (gate * jax.nn.sigmoid(gate) * up).astype(BF16)
        _pack_store(jnp.dot(act, wd_s[...], preferred_element_type=F32), o_ref)

    @pl.when(i >= n_used)
    def _():
        o_ref[...] = jnp.zeros_like(o_ref)


def _experts(block_e, next_e, n_used, xs, w_gu, w_down, layer):
    d = w_gu.shape[2]
    tile_rows = EXPERT_TILE * PACK_ROWS
    nb = xs.shape[0] // tile_rows

    grid_spec = pltpu.PrefetchScalarGridSpec(
        num_scalar_prefetch=3,
        grid=(nb,),
        in_specs=[
            pl.BlockSpec((tile_rows, LANE),
                         lambda i, be, nx, nu: (jnp.minimum(i, nu[0] - 1), 0)),
            pl.BlockSpec(memory_space=pl.ANY),
            pl.BlockSpec(memory_space=pl.ANY),
        ],
        out_specs=pl.BlockSpec((tile_rows, LANE), lambda i, be, nx, nu: (i, 0)),
        scratch_shapes=[pltpu.VMEM((d, 2 * D_EXPERT), F32), pltpu.VMEM((D_EXPERT, d), F32),
                        pltpu.VMEM((d, 2 * D_EXPERT), BF16), pltpu.VMEM((D_EXPERT, d), BF16),
                        pltpu.SemaphoreType.DMA((2,))],
    )
    return pl.pallas_call(
        functools.partial(_expert_kernel, layer=layer),
        grid_spec=grid_spec,
        out_shape=jax.ShapeDtypeStruct(xs.shape, U32),
        compiler_params=_cparams(("arbitrary",)),
        name="experts",
    )(block_e, next_e, n_used, xs, w_gu, w_down)


def _combine_kernel(dest0_ref, destn_ref, ys_ref, x_ref, rw_ref, gt_ref, g_ref, sc_ref, sh_ref,
                    *rest, final):
    if final:
        o_ref, ybuf, sem = rest
    else:
        o_ref, hn_ref, ybuf, sem = rest
    i = pl.program_id(0)
    tm = x_ref.shape[0]

    def start(dest_ref, slot):
        for k in range(TOP_K):
            _start_rows(ys_ref, ybuf.at[slot, k], dest_ref, k * tm, tm, sem.at[slot, k])

    @pl.when(i == 0)
    def _():
        start(dest0_ref, 0)

    @pl.when(i + 1 < pl.num_programs(0))
    def _():
        start(destn_ref, (i + 1) % 2)

    cur = i % 2
    for k in range(TOP_K):
        _wait_rows(ys_ref, ybuf.at[cur, k], sem.at[cur, k])
    rw = rw_ref[...]
    y0 = _unpack_load(lambda c: ybuf[cur, 0, pl.ds(c, tm, stride=PACK_ROWS), :])
    y1 = _unpack_load(lambda c: ybuf[cur, 1, pl.ds(c, tm, stride=PACK_ROWS), :])
    y = jnp.concatenate([rw[:, 0:1] * a + rw[:, 1:2] * b for a, b in zip(y0, y1)], axis=-1)
    xn = x_ref[...] + (1.0 + gt_ref[0]) * y
    if final:
        o_ref[...] = _rms(xn, g_ref[...])
    else:
        o_ref[...] = xn
        hn_ref[...] = (_rms(xn, g_ref[...]) * (1.0 + sc_ref[0]) + sh_ref[0]).astype(BF16)


def _combine(dest3, ys, x2, rw, gt, g, sc, sh, seq, final):
    n, d = x2.shape
    tm = dest3.shape[2] // TOP_K
    per_b = seq // tm
    nt = n // tm
    bidx = lambda i: (i // per_b, 0, 0)
    row = pl.BlockSpec((tm, d), lambda i: (i, 0))
    out_specs = row if final else [row, row]
    out_shape = jax.ShapeDtypeStruct((n, d), F32)
    if not final:
        out_shape = [out_shape, jax.ShapeDtypeStruct((n, d), BF16)]
    return pl.pallas_call(
        functools.partial(_combine_kernel, final=final),
        grid=(nt,),
        in_specs=[pl.BlockSpec((1, 1, TOP_K * tm), lambda i: (0, 0, 0), memory_space=pltpu.SMEM),
                  pl.BlockSpec((1, 1, TOP_K * tm), lambda i: (jnp.minimum(i + 1, nt - 1), 0, 0),
                               memory_space=pltpu.SMEM),
                  pl.BlockSpec(memory_space=pl.ANY),
                  row,
                  pl.BlockSpec((tm, LANE), lambda i: (i, 0)),
                  pl.BlockSpec((1, 1, d), bidx),
                  pl.BlockSpec((1, d), lambda i: (0, 0)),
                  pl.BlockSpec((1, 1, d), bidx),
                  pl.BlockSpec((1, 1, d), bidx)],
        out_specs=out_specs,
        out_shape=out_shape,
        scratch_shapes=[pltpu.VMEM((2, TOP_K, tm * PACK_ROWS, LANE), U32),
                        pltpu.SemaphoreType.DMA((2, TOP_K))],
        compiler_params=_cparams(("arbitrary",)),
        name="combine",
    )(dest3, dest3, ys, x2, rw, gt, g, sc, sh)


def _rope_cols(w3):
    half = B_ROPE // 2
    x1, x2 = w3[..., :half], w3[..., half:]
    z = jnp.zeros(w3.shape[:-1] + (LANE - B_ROPE,), w3.dtype)
    a = jnp.concatenate([x1, x2, z], axis=-1)
    b = jnp.concatenate([x2, x1, z], axis=-1)
    return a, b


def _layout_w_in(w):
    d = w.shape[0]
    o = 0
    qa = w[:, o:o + A_WIDTH]; o += A_WIDTH
    ka = w[:, o:o + A_WIDTH]; o += A_WIDTH
    va = w[:, o:o + A_WIDTH]; o += A_WIDTH
    cq = w[:, o:o + Q_LORA]; o += Q_LORA
    ckv = w[:, o:o + KV_LORA]; o += KV_LORA
    kr = w[:, o:o + B_ROPE]; o += B_ROPE
    gates = w[:, o:]
    kra, krb = _rope_cols(kr.reshape(d, 1, B_ROPE))
    return jnp.concatenate([gates, qa, ka, va, cq, ckv, kra.reshape(d, LANE),
                            krb.reshape(d, LANE)], axis=1).astype(BF16)


def _layout_w_uq(w):
    k = w.shape[0]
    w3 = w.reshape(k, B_HEADS, B_QK)
    nope = w3[:, :, :B_NOPE].reshape(k, B_HEADS * LANE)
    a, b = _rope_cols(w3[:, :, B_NOPE:])
    return jnp.concatenate([nope, a.reshape(k, -1), b.reshape(k, -1)], axis=1).astype(BF16)


def _layout_w_ukv(w):
    k = w.shape[0]
    w3 = w.reshape(k, B_HEADS, B_NOPE + B_V)
    return jnp.concatenate([w3[:, :, :B_NOPE].reshape(k, -1),
                            w3[:, :, B_NOPE:].reshape(k, -1)], axis=1).astype(BF16)


def kernel(x, c, positions, ln1_g, ln2_g, w_ada, b_ada, w_in, q_norm_g, w_uq, kv_norm_g, w_ukv,
           w_a_up, w_b_up, w_o, w_grp, b_grp, w_exp, b_exp, w_gu, w_down, final_g):
    bsz, seq, d = x.shape
    depth = w_in.shape[0]
    n = bsz * seq
    n_blocks = n * TOP_K // EXPERT_TILE + N_EXPERTS
    n_slots = n_blocks * EXPERT_TILE
    tile_tok = 256

    mod = _ada(c, w_ada, b_ada).reshape(depth, bsz, N_MOD, 1, d)

    pos_col = positions.reshape(n, 1)
    ctab, stab = _rope_tables(pos_col)
    posc = positions.reshape(bsz, seq, 1)
    posr = [positions.reshape(bsz, seq // dil, dil).transpose(0, 2, 1) for _, dil in A_GROUPS]
    slopes = 2.0 ** (-ALIBI_MAX_BIAS * jnp.arange(1, A_HEADS + 1, dtype=F32) / A_HEADS)
    slopes = slopes.reshape(A_NG, A_HPG).T
    slopes = jnp.pad(slopes, ((0, 0), (0, 8 - A_NG)))
    slopes = jnp.broadcast_to(slopes[:, :, None], (A_HPG, 8, LANE))

    x2 = x.reshape(n, d)
    for l in range(depth):
        sh1, sc1, gt1, sh2, sc2, gt2 = (mod[l, :, m] for m in range(N_MOD))

        if l == 0:
            proj = _norm_proj(x2, ln1_g[l].reshape(1, d), sc1, sh1, _layout_w_in(w_in[l]), seq)
        else:
            proj = _proj(h_next, _layout_w_in(w_in[l]))
        q = _mla_q(proj, q_norm_g[l].reshape(1, Q_LORA), _layout_w_uq(w_uq[l]), ctab, stab)
        k, v = _mla_kv(proj, kv_norm_g[l].reshape(1, KV_LORA), _layout_w_ukv(w_ukv[l]), ctab, stab)
        yb = _mla_attn(q, k, v, bsz, seq)
        ya = _dilated_attn(proj.reshape(bsz, seq, PROJ_COLS), posc, posr, slopes).reshape(n, A_OUT)

        w_r = jnp.concatenate([w_grp[l], w_exp[l],
                               jnp.zeros((d, LANE - N_GROUPS - N_EXPERTS), F32)], axis=1)
        w_r_hi = w_r.astype(BF16)
        w_r = jnp.concatenate([w_r_hi, (w_r - w_r_hi.astype(F32)).astype(BF16)], axis=1)
        b_r = jnp.concatenate([b_grp[l], b_exp[l],
                               jnp.zeros((LANE - N_GROUPS - N_EXPERTS,), F32)]).reshape(1, LANE)
        x2, h2, logits = _merge_out(ya, yb, proj, x2, w_a_up[l].astype(BF16),
                                    w_b_up[l].astype(BF16), w_o[l].astype(BF16), gt1,
                                    ln2_g[l].reshape(1, d), sc2, sh2, w_r, b_r, seq)

        rt, rw, cnt = _route(logits)
        counts = cnt[0, N_GROUPS:N_GROUPS + N_EXPERTS].astype(I32)
        padded = (counts + EXPERT_TILE - 1) // EXPERT_TILE * EXPERT_TILE
        pends = jnp.cumsum(padded)
        pstarts = pends - padded
        eids = rt[:TOP_K].astype(I32)
        onehot = eids[:, :, None] == jnp.arange(N_EXPERTS, dtype=I32)
        dest = jnp.sum(jnp.where(onehot, pstarts, 0), axis=-1) + rt[TOP_K:2 * TOP_K].astype(I32)
        dest3 = dest.reshape(TOP_K, n // tile_tok, tile_tok).transpose(1, 0, 2).reshape(
            n // tile_tok, 1, TOP_K * tile_tok)
        n_used = (pends[-1:] // EXPERT_TILE).astype(I32)
        blk_start = jnp.arange(n_blocks, dtype=I32) * EXPERT_TILE
        block_e = jnp.minimum(jnp.sum((pends[None, :] <= blk_start[:, None]).astype(I32), axis=1),
                              N_EXPERTS - 1)

        eidx = jnp.arange(N_EXPERTS, dtype=I32)
        nxt_used = lax.cummin(jnp.where(counts > 0, eidx, N_EXPERTS), reverse=True)
        nxt_after = jnp.concatenate([nxt_used[1:], jnp.full((1,), N_EXPERTS, I32)])
        nxt_after = jnp.where(nxt_after >= N_EXPERTS, -1, nxt_after)
        next_e = jnp.sum(jnp.where(block_e[:, None] == eidx[None, :], nxt_after[None, :], 0),
                         axis=1).astype(I32)
        slot_buf = jnp.zeros((n_slots * PACK_ROWS, LANE), U32) if l == 0 else ys
        xs = _dispatch(dest3, h2, slot_buf)
        ys = _experts(block_e, next_e, n_used, xs, w_gu, w_down, l)
        if l == depth - 1:
            x2 = _combine(dest3, ys, x2, rw, gt2, final_g.reshape(1, d), gt2, gt2, seq, final=True)
        else:
            x2, h_next = _combine(dest3, ys, x2, rw, gt2, ln1_g[l + 1].reshape(1, d),
                                  mod[l + 1, :, 1], mod[l + 1, :, 0], seq, final=False)
    return x2.reshape(bsz, seq, d)
```

```python
import functools

import jax
import jax.numpy as jnp
from jax import lax
from jax.experimental import pallas as pl
from jax.experimental.pallas import tpu as pltpu

F32 = jnp.float32
BF16 = jnp.bfloat16
I32 = jnp.int32
U32 = jnp.uint32
HIGHEST = lax.Precision.HIGHEST

HEAD_DIM = 128
A_GROUPS = ((128, 1), (512, 4), (2048, 16))
A_HPG = 4
A_NG = len(A_GROUPS)
A_HEADS = A_HPG * A_NG
A_WIDTH = A_HEADS * HEAD_DIM
A_OUT = A_HPG * HEAD_DIM
ALIBI_MAX_BIAS = 8.0
B_HEADS = 8
B_NOPE = 128
B_ROPE = 64
B_V = 128
B_QK = B_NOPE + B_ROPE
Q_LORA = 512
KV_LORA = 512
B_OUT = B_HEADS * B_V
ROPE_THETA = 10000.0
N_GROUPS = 8
EPG = 8
N_EXPERTS = N_GROUPS * EPG
TOP_K = 2
D_EXPERT = 512
N_MOD = 6
EPS = 1e-6
NEG = -1e30
LOG2E = 1.4426950408889634

LANE = 128
VMEM_LIMIT = 56 * 1024 * 1024
QBLK = 128
EXPERT_TILE = 256

COL_GA = 0
COL_GB = 16
COL_QA = 32
COL_KA = 44
COL_VA = 56
COL_CQ = 68
COL_CKV = 72
COL_KR = 76
N_COLBLK = 78
PROJ_COLS = N_COLBLK * LANE


def _cparams(sem):
    return pltpu.CompilerParams(dimension_semantics=sem, vmem_limit_bytes=VMEM_LIMIT)


def _rms(x, g):
    return x * lax.rsqrt(jnp.mean(x * x, axis=-1, keepdims=True) + EPS) * g


def _ada_kernel(c_ref, w_ref, b_ref, o_ref):
    c = c_ref[...]
    cs = c * jax.nn.sigmoid(c)
    o_ref[0] = jnp.dot(cs, w_ref[0], precision=HIGHEST, preferred_element_type=F32) + b_ref[0]


def _ada(c, w_ada, b_ada):
    nl, d, nm = w_ada.shape
    bsz = c.shape[0]
    tn = 1536
    return pl.pallas_call(
        _ada_kernel,
        grid=(nl, nm // tn),
        in_specs=[
            pl.BlockSpec((bsz, d), lambda l, j: (0, 0)),
            pl.BlockSpec((1, d, tn), lambda l, j: (l, 0, j)),
            pl.BlockSpec((1, 1, tn), lambda l, j: (l, 0, j)),
        ],
        out_specs=pl.BlockSpec((1, bsz, tn), lambda l, j: (l, 0, j)),
        out_shape=jax.ShapeDtypeStruct((nl, bsz, nm), F32),
        compiler_params=_cparams(("arbitrary", "arbitrary")),
        name="ada",
    )(c, w_ada, b_ada.reshape(nl, 1, nm))


def _rope_table_kernel(pos_ref, inv_ref, c_ref, s_ref):
    ang = pos_ref[...].astype(F32) * inv_ref[...]
    lane = lax.broadcasted_iota(I32, ang.shape, 1)
    half = B_ROPE // 2
    cs = jnp.cos(ang)
    sn = jnp.sin(ang)
    c_ref[...] = jnp.where(lane < B_ROPE, cs, 0.0)
    s_ref[...] = jnp.where(lane < half, -sn, jnp.where(lane < B_ROPE, sn, 0.0))


def _rope_tables(pos_col):
    n = pos_col.shape[0]
    half = B_ROPE // 2
    inv = ROPE_THETA ** (-jnp.arange(half, dtype=F32) / half)
    inv_row = jnp.concatenate([inv, inv, jnp.zeros((LANE - B_ROPE,), F32)]).reshape(1, LANE)
    tm = 2048
    return pl.pallas_call(
        _rope_table_kernel,
        grid=(n // tm,),
        in_specs=[pl.BlockSpec((tm, 1), lambda i: (i, 0)),
                  pl.BlockSpec((1, LANE), lambda i: (0, 0))],
        out_specs=[pl.BlockSpec((tm, LANE), lambda i: (i, 0)),
                   pl.BlockSpec((tm, LANE), lambda i: (i, 0))],
        out_shape=[jax.ShapeDtypeStruct((n, LANE), F32)] * 2,
        compiler_params=_cparams(("arbitrary",)),
        name="rope_tables",
    )(pos_col, inv_row)


def _norm_proj_kernel(x_ref, g_ref, sc_ref, sh_ref, w_ref, o_ref, h_ref):
    @pl.when(pl.program_id(1) == 0)
    def _():
        h = _rms(x_ref[...], g_ref[...]) * (1.0 + sc_ref[0]) + sh_ref[0]
        h_ref[...] = h.astype(BF16)

    o_ref[...] = jnp.dot(h_ref[...], w_ref[...], preferred_element_type=F32).astype(o_ref.dtype)


def _norm_proj(x2, g, sc, sh, w, seq):
    n, d = x2.shape
    ncol = w.shape[1]
    tm = 1024
    tn = 13 * LANE
    per_b = seq // tm
    return pl.pallas_call(
        _norm_proj_kernel,
        grid=(n // tm, ncol // tn),
        in_specs=[
            pl.BlockSpec((tm, d), lambda i, j: (i, 0)),
            pl.BlockSpec((1, d), lambda i, j: (0, 0)),
            pl.BlockSpec((1, 1, d), lambda i, j: (i // per_b, 0, 0)),
            pl.BlockSpec((1, 1, d), lambda i, j: (i // per_b, 0, 0)),
            pl.BlockSpec((d, tn), lambda i, j: (0, j)),
        ],
        out_specs=pl.BlockSpec((tm, tn), lambda i, j: (i, j)),
        out_shape=jax.ShapeDtypeStruct((n, ncol), BF16),
        scratch_shapes=[pltpu.VMEM((tm, d), BF16)],
        compiler_params=_cparams(("arbitrary", "arbitrary")),
        name="norm_proj",
    )(x2, g, sc, sh, w)


def _proj_kernel(h_ref, w_ref, o_ref):
    o_ref[...] = jnp.dot(h_ref[...], w_ref[...], preferred_element_type=F32).astype(o_ref.dtype)


def _proj(h, w):
    n, d = h.shape
    ncol = w.shape[1]
    tm = 2048
    tn = 13 * LANE
    return pl.pallas_call(
        _proj_kernel,
        grid=(n // tm, ncol // tn),
        in_specs=[pl.BlockSpec((tm, d), lambda i, j: (i, 0)),
                  pl.BlockSpec((d, tn), lambda i, j: (0, j))],
        out_specs=pl.BlockSpec((tm, tn), lambda i, j: (i, j)),
        out_shape=jax.ShapeDtypeStruct((n, ncol), BF16),
        compiler_params=_cparams(("arbitrary", "arbitrary")),
        name="proj",
    )(h, w)


def _mla_q_kernel(cq_ref, g_ref, w_ref, c_ref, s_ref, q_ref):
    h = _rms(cq_ref[...].astype(F32), g_ref[...]).astype(BF16)
    r = jnp.dot(h, w_ref[...], preferred_element_type=F32)
    scale = B_QK ** -0.5 * LOG2E
    c = c_ref[...]
    s = s_ref[...]
    nw = B_HEADS * LANE
    for hd in range(B_HEADS):
        nope = r[:, hd * LANE:(hd + 1) * LANE]
        ra = r[:, nw + hd * LANE: nw + (hd + 1) * LANE]
        rb = r[:, 2 * nw + hd * LANE: 2 * nw + (hd + 1) * LANE]
        q_ref[:, 2 * hd * LANE:(2 * hd + 1) * LANE] = (nope * scale).astype(BF16)
        q_ref[:, (2 * hd + 1) * LANE:(2 * hd + 2) * LANE] = ((ra * c + rb * s) * scale).astype(BF16)


def _mla_q(proj, g, w, ctab, stab):
    n = proj.shape[0]
    tm = 512
    return pl.pallas_call(
        _mla_q_kernel,
        grid=(n // tm,),
        in_specs=[
            pl.BlockSpec((tm, Q_LORA), lambda i: (i, COL_CQ * LANE // Q_LORA)),
            pl.BlockSpec((1, Q_LORA), lambda i: (0, 0)),
            pl.BlockSpec(w.shape, lambda i: (0, 0)),
            pl.BlockSpec((tm, LANE), lambda i: (i, 0)),
            pl.BlockSpec((tm, LANE), lambda i: (i, 0)),
        ],
        out_specs=pl.BlockSpec((tm, 2 * B_HEADS * LANE), lambda i: (i, 0)),
        out_shape=jax.ShapeDtypeStruct((n, 2 * B_HEADS * LANE), BF16),
        compiler_params=_cparams(("arbitrary",)),
        name="mla_q",
    )(proj, g, w, ctab, stab)


def _mla_kv_kernel(ckv_ref, kr_ref, g_ref, w_ref, c_ref, s_ref, k_ref, v_ref):
    h = _rms(ckv_ref[...].astype(F32), g_ref[...]).astype(BF16)
    r = jnp.dot(h, w_ref[...], preferred_element_type=F32)
    kr = kr_ref[...].astype(F32)
    krope = (kr[:, :LANE] * c_ref[...] + kr[:, LANE:] * s_ref[...]).astype(BF16)
    nw = B_HEADS * LANE
    for hd in range(B_HEADS):
        k_ref[:, 2 * hd * LANE:(2 * hd + 1) * LANE] = r[:, hd * LANE:(hd + 1) * LANE].astype(BF16)
        k_ref[:, (2 * hd + 1) * LANE:(2 * hd + 2) * LANE] = krope
    v_ref[...] = r[:, nw:].astype(BF16)


def _mla_kv(proj, g, w, ctab, stab):
    n = proj.shape[0]
    tm = 512
    return pl.pallas_call(
        _mla_kv_kernel,
        grid=(n // tm,),
        in_specs=[
            pl.BlockSpec((tm, KV_LORA), lambda i: (i, COL_CKV * LANE // KV_LORA)),
            pl.BlockSpec((tm, 2 * LANE), lambda i: (i, COL_KR // 2)),
            pl.BlockSpec((1, KV_LORA), lambda i: (0, 0)),
            pl.BlockSpec(w.shape, lambda i: (0, 0)),
            pl.BlockSpec((tm, LANE), lambda i: (i, 0)),
            pl.BlockSpec((tm, LANE), lambda i: (i, 0)),
        ],
        out_specs=[pl.BlockSpec((tm, 2 * B_HEADS * LANE), lambda i: (i, 0)),
                   pl.BlockSpec((tm, B_HEADS * LANE), lambda i: (i, 0))],
        out_shape=[jax.ShapeDtypeStruct((n, 2 * B_HEADS * LANE), BF16),
                   jax.ShapeDtypeStruct((n, B_HEADS * LANE), BF16)],
        compiler_params=_cparams(("arbitrary",)),
        name="mla_kv",
    )(proj, proj, g, w, ctab, stab)


MLA_SUB = 512


MLA_HPS = 4


def _mla_attn_kernel(q_ref, k_ref, v_ref, o_ref, vx_ref):
    for hh in range(MLA_HPS):
        vx_ref[hh, :, :LANE] = v_ref[:, hh * LANE:(hh + 1) * LANE]
        vx_ref[hh, :, LANE:] = jnp.ones((vx_ref.shape[1], LANE), BF16)

    for hh in range(MLA_HPS):
        cols = slice(2 * hh * LANE, 2 * (hh + 1) * LANE)
        for j in range(q_ref.shape[0] // MLA_SUB):
            rows = slice(j * MLA_SUB, (j + 1) * MLA_SUB)
            s = lax.dot_general(q_ref[rows, cols], k_ref[:, cols], (((1,), (1,)), ((), ())),
                                preferred_element_type=F32)
            m = jnp.max(s, axis=-1, keepdims=True)
            p = jnp.exp2(s - m).astype(BF16)
            ox = jnp.dot(p, vx_ref[hh], preferred_element_type=F32)
            o_ref[rows, hh * LANE:(hh + 1) * LANE] = (
                ox[:, :LANE] / ox[:, LANE:LANE + 1]).astype(o_ref.dtype)


def _mla_attn(q, k, v, bsz, seq):
    n = q.shape[0]
    qk_w = 2 * LANE * MLA_HPS
    return pl.pallas_call(
        _mla_attn_kernel,
        grid=(bsz, B_HEADS // MLA_HPS),
        in_specs=[
            pl.BlockSpec((seq, qk_w), lambda b, h: (b, h)),
            pl.BlockSpec((seq, qk_w), lambda b, h: (b, h)),
            pl.BlockSpec((seq, LANE * MLA_HPS), lambda b, h: (b, h)),
        ],
        out_specs=pl.BlockSpec((seq, LANE * MLA_HPS), lambda b, h: (b, h)),
        out_shape=jax.ShapeDtypeStruct((n, B_HEADS * LANE), BF16),
        scratch_shapes=[pltpu.VMEM((MLA_HPS, seq, 2 * LANE), BF16)],
        compiler_params=_cparams(("arbitrary", "arbitrary")),
        name="mla_attn",
    )(q, k, v)


def _dilated_kernel(*refs, seq):
    qkv = refs[:3 * A_NG]
    posc_ref = refs[3 * A_NG]
    posr = refs[3 * A_NG + 1: 3 * A_NG + 1 + A_NG]
    sl_ref = refs[3 * A_NG + 1 + A_NG]
    o_ref = refs[3 * A_NG + 2 + A_NG]
    qf, kf, vf, og, lg = refs[3 * A_NG + 3 + A_NG:]
    scale = HEAD_DIM ** -0.5

    for g, (win, dil) in enumerate(A_GROUPS):
        radius = win // (2 * dil)
        length = seq // dil
        kw = min(2 * QBLK, length)
        q_in, k_in, v_in = qkv[3 * g], qkv[3 * g + 1], qkv[3 * g + 2]
        slope2 = sl_ref[0, g:g + 1, 0:1] * LOG2E
        if dil > 1:
            qf[...] = q_in[0].astype(F32)
            kf[...] = k_in[0].astype(F32)
            vf[...] = v_in[0].astype(F32)

        def rows(start, size, dil=dil):
            if dil == 1:
                return pl.ds(start, size)
            return pl.ds(start, size, stride=dil)

        def load(ref_in, ref_f32, rws, dil=dil):
            if dil == 1:
                return ref_in[0, rws, :]
            return ref_f32[rws, :].astype(BF16)

        units = []
        for r in range(dil):
            for ib in range(length // QBLK):
                i0 = ib * QBLK
                ws = min(max(i0 - radius, 0), length - kw)
                units.append((r, i0, ws, rows(r + dil * i0, QBLK), rows(r + dil * ws, kw)))
        qs = [(load(q_in, qf, u[3]).astype(F32) * (scale * LOG2E)).astype(BF16) for u in units]
        ks = [load(k_in, kf, u[4]) for u in units]
        biased = []
        for (r, i0, ws, q_rows, _), q, k in zip(units, qs, ks):
            s = lax.dot_general(q, k, (((1,), (1,)), ((), ())), preferred_element_type=F32)
            pq = posc_ref[0, q_rows, :]
            pk = posr[g][0, r:r + 1, ws:ws + kw]
            dist = jnp.abs(pq - pk).astype(F32)
            ii = i0 + lax.broadcasted_iota(I32, (QBLK, kw), 0)
            jj = ws + lax.broadcasted_iota(I32, (QBLK, kw), 1)
            biased.append(jnp.where(jnp.abs(ii - jj) <= radius, s - slope2 * dist, NEG))
        ms = [jnp.max(s, axis=-1, keepdims=True) for s in biased]
        ps, dens = [], []
        for s, m in zip(biased, ms):
            p = jnp.exp2(s - m)
            dens.append(jnp.sum(p, axis=-1, keepdims=True))
            ps.append(p.astype(BF16))
        vs = [load(v_in, vf, u[4]) for u in units]
        outs = [jnp.dot(p, v, preferred_element_type=F32) / den
                for p, v, den in zip(ps, vs, dens)]
        for u, o, m, den in zip(units, outs, ms, dens):
            og[g, u[3], :] = o
            lg[g, u[3], :] = jnp.broadcast_to(m * (1.0 / LOG2E) + jnp.log(den), (QBLK, HEAD_DIM))

    chunk = 256

    def merge(ci, carry):
        rs = pl.ds(pl.multiple_of(ci * chunk, chunk), chunk)
        l0, l1, l2 = lg[0, rs, :], lg[1, rs, :], lg[2, rs, :]
        m = jnp.maximum(jnp.maximum(l0, l1), l2)
        w0, w1, w2 = jnp.exp(l0 - m), jnp.exp(l1 - m), jnp.exp(l2 - m)
        o = (w0 * og[0, rs, :] + w1 * og[1, rs, :] + w2 * og[2, rs, :]) / (w0 + w1 + w2)
        o_ref[0, rs, :] = o.astype(o_ref.dtype)
        return carry

    lax.fori_loop(0, seq // chunk, merge, 0)


def _dilated_attn(proj3, posc, posr, slopes):
    bsz, seq, _ = proj3.shape
    in_specs = []
    args = []
    for g in range(A_NG):
        for base in (COL_QA, COL_KA, COL_VA):
            in_specs.append(pl.BlockSpec(
                (1, seq, LANE), lambda b, h, base=base, g=g: (b, 0, base + A_HPG * g + h)))
            args.append(proj3)
    in_specs.append(pl.BlockSpec((1, seq, 1), lambda b, h: (b, 0, 0)))
    args.append(posc)
    for g, (_, dil) in enumerate(A_GROUPS):
        in_specs.append(pl.BlockSpec((1, dil, seq // dil), lambda b, h: (b, 0, 0)))
        args.append(posr[g])
    in_specs.append(pl.BlockSpec((1, 8, LANE), lambda b, h: (h, 0, 0)))
    args.append(slopes)
    return pl.pallas_call(
        functools.partial(_dilated_kernel, seq=seq),
        grid=(bsz, A_HPG),
        in_specs=in_specs,
        out_specs=pl.BlockSpec((1, seq, LANE), lambda b, h: (b, 0, h)),
        out_shape=jax.ShapeDtypeStruct((bsz, seq, A_OUT), BF16),
        scratch_shapes=[pltpu.VMEM((seq, HEAD_DIM), F32)] * 3
        + [pltpu.VMEM((A_NG, seq, HEAD_DIM), F32)] * 2,
        compiler_params=_cparams(("arbitrary", "arbitrary")),
        name="dilated_attn",
    )(*args)


def _merge_out_kernel(ya_ref, yb_ref, ga_ref, gb_ref, x_ref, wa_ref, wb_ref, wo_ref, gt_ref,
                      g2_ref, sc_ref, sh_ref, wr_ref, br_ref, xo_ref, h_ref, lg_ref):
    a = jnp.dot(ya_ref[...], wa_ref[...], preferred_element_type=F32)
    b = jnp.dot(yb_ref[...], wb_ref[...], preferred_element_type=F32)
    ga = jax.nn.sigmoid(ga_ref[...].astype(F32))
    gb = jax.nn.sigmoid(gb_ref[...].astype(F32))
    merged = (ga * a + gb * b).astype(BF16)
    xn = x_ref[...] + (1.0 + gt_ref[0]) * jnp.dot(merged, wo_ref[...], preferred_element_type=F32)
    xo_ref[...] = xn
    h = _rms(xn, g2_ref[...]) * (1.0 + sc_ref[0]) + sh_ref[0]
    _pack_store(h, h_ref)
    h_hi = h.astype(BF16)
    h_lo = (h - h_hi.astype(F32)).astype(BF16)
    t = jnp.dot(h_hi, wr_ref[...], preferred_element_type=F32)
    u = jnp.dot(h_lo, wr_ref[:, :LANE], preferred_element_type=F32)
    lg_ref[...] = t[:, :LANE] + t[:, LANE:] + u + br_ref[...]


def _merge_out(ya, yb, proj, x2, wa, wb, wo, gt, g2, sc, sh, wr, br, seq):
    n, d = x2.shape
    tm = 512
    per_b = seq // tm
    const = lambda i: (0, 0)
    bidx = lambda i: (i // per_b, 0, 0)
    once = pl.Buffered(1)
    return pl.pallas_call(
        _merge_out_kernel,
        grid=(n // tm,),
        in_specs=[
            pl.BlockSpec((tm, A_OUT), lambda i: (i, 0)),
            pl.BlockSpec((tm, B_OUT), lambda i: (i, 0)),
            pl.BlockSpec((tm, d), lambda i: (i, COL_GA * LANE // d)),
            pl.BlockSpec((tm, d), lambda i: (i, COL_GB * LANE // d)),
            pl.BlockSpec((tm, d), lambda i: (i, 0)),
            pl.BlockSpec(wa.shape, const, pipeline_mode=once),
            pl.BlockSpec(wb.shape, const, pipeline_mode=once),
            pl.BlockSpec(wo.shape, const, pipeline_mode=once),
            pl.BlockSpec((1, 1, d), bidx),
            pl.BlockSpec((1, d), const),
            pl.BlockSpec((1, 1, d), bidx),
            pl.BlockSpec((1, 1, d), bidx),
            pl.BlockSpec(wr.shape, const),
            pl.BlockSpec((1, LANE), const),
        ],
        out_specs=[pl.BlockSpec((tm, d), lambda i: (i, 0)),
                   pl.BlockSpec((tm * PACK_ROWS, LANE), lambda i: (i, 0)),
                   pl.BlockSpec((tm, LANE), lambda i: (i, 0))],
        out_shape=[jax.ShapeDtypeStruct((n, d), F32),
                   jax.ShapeDtypeStruct((n * PACK_ROWS, LANE), U32),
                   jax.ShapeDtypeStruct((n, LANE), F32)],
        compiler_params=_cparams(("arbitrary",)),
        name="merge_out",
    )(ya, yb, proj, proj, x2, wa, wb, wo, gt, g2, sc, sh, wr, br)


def _route_kernel(lg_ref, rt_ref, rw_ref, cnt_ref, carry_ref):
    @pl.when(pl.program_id(0) == 0)
    def _():
        carry_ref[...] = jnp.zeros_like(carry_ref)

    lg = lg_ref[...]
    tm = lg.shape[0]
    lane = lax.broadcasted_iota(I32, lg.shape, 1).astype(F32)
    big = float(2 * LANE)
    is_grp = lane < N_GROUPS
    gl = jnp.where(is_grp, lg, NEG)
    gm = jnp.max(gl, axis=-1, keepdims=True)
    ge = jnp.where(is_grp, jnp.exp(gl - gm), 0.0)
    pg = ge / jnp.sum(ge, axis=-1, keepdims=True)
    p_top = jnp.max(pg, axis=-1, keepdims=True)
    g_idx = jnp.min(jnp.where(is_grp & (pg == p_top), lane, big), axis=-1, keepdims=True)

    lo = N_GROUPS + EPG * g_idx
    in_grp = (lane >= lo) & (lane < lo + EPG)
    el = jnp.where(in_grp, lg, NEG)
    em = jnp.max(el, axis=-1, keepdims=True)
    ee = jnp.where(in_grp, jnp.exp(el - em), 0.0)
    pin = ee / jnp.sum(ee, axis=-1, keepdims=True)
    v1 = jnp.max(pin, axis=-1, keepdims=True)
    i1 = jnp.min(jnp.where(in_grp & (pin == v1), lane, big), axis=-1, keepdims=True)
    rest = in_grp & (lane != i1)
    p2 = jnp.where(rest, pin, -1.0)
    v2 = jnp.max(p2, axis=-1, keepdims=True)
    i2 = jnp.min(jnp.where(rest & (p2 == v2), lane, big), axis=-1, keepdims=True)
    vs = v1 + v2
    w1 = p_top * v1 / vs
    w2 = p_top * v2 / vs

    oh1 = lane == i1
    oh2 = lane == i2
    mm = (oh1 | oh2).astype(BF16)
    tri = (lax.broadcasted_iota(I32, (tm, tm), 0) > lax.broadcasted_iota(I32, (tm, tm), 1))
    cnt = jnp.dot(tri.astype(BF16), mm, preferred_element_type=F32) + carry_ref[...]
    r1 = jnp.sum(jnp.where(oh1, cnt, 0.0), axis=-1, keepdims=True)
    r2 = jnp.sum(jnp.where(oh2, cnt, 0.0), axis=-1, keepdims=True)
    carry_ref[...] += jnp.sum(mm.astype(F32), axis=0, keepdims=True)
    cnt_ref[...] = carry_ref[...]

    e1 = i1 - N_GROUPS
    e2 = i2 - N_GROUPS
    packed = jnp.where(lane == 0, e1, jnp.where(lane == 1, e2, jnp.where(
        lane == 2, r1, jnp.where(lane == 3, r2, jnp.where(lane == 4, w1, w2)))))
    rt_ref[...] = packed.T[:8, :]
    rw_ref[...] = jnp.where(lane == 0, w1, w2)


def _route(logits):
    n = logits.shape[0]
    tm = 512
    return pl.pallas_call(
        _route_kernel,
        grid=(n // tm,),
        in_specs=[pl.BlockSpec((tm, LANE), lambda i: (i, 0))],
        out_specs=[pl.BlockSpec((8, tm), lambda i: (0, i)),
                   pl.BlockSpec((tm, LANE), lambda i: (i, 0)),
                   pl.BlockSpec((1, LANE), lambda i: (0, 0))],
        out_shape=[jax.ShapeDtypeStruct((8, n), F32),
                   jax.ShapeDtypeStruct((n, LANE), F32),
                   jax.ShapeDtypeStruct((1, LANE), F32)],
        scratch_shapes=[pltpu.VMEM((1, LANE), F32)],
        compiler_params=_cparams(("arbitrary",)),
        name="route",
    )(logits)


PACK_ROWS = 8
HI_MASK = 0xFFFF0000
WEIGHT_SLABS = 4


def _pack_store(val, ref):
    rows = val.shape[0]
    for c in range(PACK_ROWS):
        lo = val[:, 2 * c * LANE:(2 * c + 1) * LANE].astype(BF16).astype(F32)
        hi = val[:, (2 * c + 1) * LANE:(2 * c + 2) * LANE].astype(BF16).astype(F32)
        word = (pltpu.bitcast(hi, U32) & jnp.uint32(HI_MASK)) | (pltpu.bitcast(lo, U32) >> 16)
        ref[pl.ds(c, rows, stride=PACK_ROWS), :] = word


def _unpack_load(load_chunk):
    pieces = []
    for c in range(PACK_ROWS):
        w = load_chunk(c)
        pieces.append(pltpu.bitcast(w << 16, F32))
        pieces.append(pltpu.bitcast(w & jnp.uint32(HI_MASK), F32))
    return pieces


def _start_rows(src_ref, dst_ref, idx_ref, base, count, sem):
    for j in range(count):
        src_row = pl.multiple_of(idx_ref[0, 0, base + j] * PACK_ROWS, PACK_ROWS)
        pltpu.make_async_copy(src_ref.at[pl.ds(src_row, PACK_ROWS)],
                              dst_ref.at[pl.ds(j * PACK_ROWS, PACK_ROWS)], sem).start(priority=j % 2)


def _wait_rows(src_ref, dst_ref, sem):
    pltpu.make_async_copy(src_ref.at[pl.ds(0, dst_ref.shape[0])], dst_ref, sem).wait()


def _dispatch_kernel(dest_ref, h_ref, xs_in_ref, xs_ref, sem):
    del xs_in_ref
    tm = h_ref.shape[0] // PACK_ROWS
    for k in range(TOP_K):
        for j in range(tm):
            dst_row = pl.multiple_of(dest_ref[0, 0, k * tm + j] * PACK_ROWS, PACK_ROWS)
            pltpu.make_async_copy(h_ref.at[pl.ds(j * PACK_ROWS, PACK_ROWS)],
                                  xs_ref.at[pl.ds(dst_row, PACK_ROWS)],
                                  sem.at[k]).start(priority=j % 2)
    for k in range(TOP_K):
        pltpu.make_async_copy(h_ref, xs_ref.at[pl.ds(0, tm * PACK_ROWS)], sem.at[k]).wait()


def _dispatch(dest3, h_packed, xs0):
    tm = dest3.shape[2] // TOP_K
    nt = dest3.shape[0]
    return pl.pallas_call(
        _dispatch_kernel,
        grid=(nt,),
        in_specs=[pl.BlockSpec((1, 1, TOP_K * tm), lambda i: (i, 0, 0), memory_space=pltpu.SMEM),
                  pl.BlockSpec((tm * PACK_ROWS, LANE), lambda i: (i, 0)),
                  pl.BlockSpec(memory_space=pl.ANY)],
        out_specs=pl.BlockSpec(memory_space=pl.ANY),
        out_shape=jax.ShapeDtypeStruct(xs0.shape, xs0.dtype),
        scratch_shapes=[pltpu.SemaphoreType.DMA((TOP_K,))],
        input_output_aliases={2: 0},
        compiler_params=_cparams(("arbitrary",)),
        name="dispatch",
    )(dest3, h_packed, xs0)


def _expert_kernel(be_ref, nx_ref, nu_ref, x_ref, wgu_ref, wd_ref, o_ref,
                   wgu_f, wd_f, wgu_s, wd_s, wsem, *, layer):
    i = pl.program_id(0)
    n_used = nu_ref[0]
    tb = x_ref.shape[0] // PACK_ROWS

    def weight_copies(e):
        cps = []
        for src, dst, s in ((wgu_ref, wgu_f, wsem.at[0]), (wd_ref, wd_f, wsem.at[1])):
            rows = dst.shape[0] // WEIGHT_SLABS
            for c in range(WEIGHT_SLABS):
                cps.append(pltpu.make_async_copy(src.at[layer, e, pl.ds(c * rows, rows)],
                                                 dst.at[pl.ds(c * rows, rows)], s))
        return cps

    @pl.when(i == 0)
    def _():
        for cp in weight_copies(be_ref[0]):
            cp.start()

    @pl.when(i < n_used)
    def _():
        e = be_ref[i]
        prev = be_ref[jnp.maximum(i - 1, 0)]

        @pl.when((i == 0) | (e != prev))
        def _():
            for cp in weight_copies(e):
                cp.wait()
            wgu_s[...] = wgu_f[...].astype(BF16)
            wd_s[...] = wd_f[...].astype(BF16)

            @pl.when(nx_ref[i] >= 0)
            def _():
                for cp in weight_copies(nx_ref[i]):
                    cp.start()

        pieces = _unpack_load(lambda c: x_ref[pl.ds(c, tb, stride=PACK_ROWS), :])
        x = jnp.concatenate([p.astype(BF16) for p in pieces], axis=-1)
        gu = jnp.dot(x, wgu_s[...], preferred_element_type=F32)
        gate = gu[:, :D_EXPERT]
        up = gu[:, D_EXPERT:]
        act = (gate * jax.nn.sigmoid(gate) * up).astype(BF16)
        _pack_store(jnp.dot(act, wd_s[...], preferred_element_type=F32), o_ref)

    @pl.when(i >= n_used)
    def _():
        o_ref[...] = jnp.zeros_like(o_ref)


def _experts(block_e, next_e, n_used, xs, w_gu, w_down, layer):
    d = w_gu.shape[2]
    tile_rows = EXPERT_TILE * PACK_ROWS
    nb = xs.shape[0] // tile_rows

    grid_spec = pltpu.PrefetchScalarGridSpec(
        num_scalar_prefetch=3,
        grid=(nb,),
        in_specs=[
            pl.BlockSpec((tile_rows, LANE),
                         lambda i, be, nx, nu: (jnp.minimum(i, nu[0] - 1), 0)),
            pl.BlockSpec(memory_space=pl.ANY),
            pl.BlockSpec(memory_space=pl.ANY),
        ],
        out_specs=pl.BlockSpec((tile_rows, LANE), lambda i, be, nx, nu: (i, 0)),
        scratch_shapes=[pltpu.VMEM((d, 2 * D_EXPERT), F32), pltpu.VMEM((D_EXPERT, d), F32),
                        pltpu.VMEM((d, 2 * D_EXPERT), BF16), pltpu.VMEM((D_EXPERT, d), BF16),
                        pltpu.SemaphoreType.DMA((2,))],
    )
    return pl.pallas_call(
        functools.partial(_expert_kernel, layer=layer),
        grid_spec=grid_spec,
        out_shape=jax.ShapeDtypeStruct(xs.shape, U32),
        compiler_params=_cparams(("arbitrary",)),
        name="experts",
    )(block_e, next_e, n_used, xs, w_gu, w_down)


def _combine_kernel(dest0_ref, destn_ref, ys_ref, x_ref, rw_ref, gt_ref, g_ref, sc_ref, sh_ref,
                    *rest, final):
    if final:
        o_ref, ybuf, sem = rest
    else:
        o_ref, hn_ref, ybuf, sem = rest
    i = pl.program_id(0)
    tm = x_ref.shape[0]

    def start(dest_ref, slot):
        for k in range(TOP_K):
            _start_rows(ys_ref, ybuf.at[slot, k], dest_ref, k * tm, tm, sem.at[slot, k])

    @pl.when(i == 0)
    def _():
        start(dest0_ref, 0)

    @pl.when(i + 1 < pl.num_programs(0))
    def _():
        start(destn_ref, (i + 1) % 2)

    cur = i % 2
    for k in range(TOP_K):
        _wait_rows(ys_ref, ybuf.at[cur, k], sem.at[cur, k])
    rw = rw_ref[...]
    y0 = _unpack_load(lambda c: ybuf[cur, 0, pl.ds(c, tm, stride=PACK_ROWS), :])
    y1 = _unpack_load(lambda c: ybuf[cur, 1, pl.ds(c, tm, stride=PACK_ROWS), :])
    y = jnp.concatenate([rw[:, 0:1] * a + rw[:, 1:2] * b for a, b in zip(y0, y1)], axis=-1)
    xn = x_ref[...] + (1.0 + gt_ref[0]) * y
    if final:
        o_ref[...] = _rms(xn, g_ref[...])
    else:
        o_ref[...] = xn
        hn_ref[...] = (_rms(xn, g_ref[...]) * (1.0 + sc_ref[0]) + sh_ref[0]).astype(BF16)


def _combine(dest3, ys, x2, rw, gt, g, sc, sh, seq, final):
    n, d = x2.shape
    tm = dest3.shape[2] // TOP_K
    per_b = seq // tm
    nt = n // tm
    bidx = lambda i: (i // per_b, 0, 0)
    row = pl.BlockSpec((tm, d), lambda i: (i, 0))
    out_specs = row if final else [row, row]
    out_shape = jax.ShapeDtypeStruct((n, d), F32)
    if not final:
        out_shape = [out_shape, jax.ShapeDtypeStruct((n, d), BF16)]
    return pl.pallas_call(
        functools.partial(_combine_kernel, final=final),
        grid=(nt,),
        in_specs=[pl.BlockSpec((1, 1, TOP_K * tm), lambda i: (0, 0, 0), memory_space=pltpu.SMEM),
                  pl.BlockSpec((1, 1, TOP_K * tm), lambda i: (jnp.minimum(i + 1, nt - 1), 0, 0),
                               memory_space=pltpu.SMEM),
                  pl.BlockSpec(memory_space=pl.ANY),
                  row,
                  pl.BlockSpec((tm, LANE), lambda i: (i, 0)),
                  pl.BlockSpec((1, 1, d), bidx),
                  pl.BlockSpec((1, d), lambda i: (0, 0)),
                  pl.BlockSpec((1, 1, d), bidx),
                  pl.BlockSpec((1, 1, d), bidx)],
        out_specs=out_specs,
        out_shape=out_shape,
        scratch_shapes=[pltpu.VMEM((2, TOP_K, tm * PACK_ROWS, LANE), U32),
                        pltpu.SemaphoreType.DMA((2, TOP_K))],
        compiler_params=_cparams(("arbitrary",)),
        name="combine",
    )(dest3, dest3, ys, x2, rw, gt, g, sc, sh)


def _rope_cols(w3):
    half = B_ROPE // 2
    x1, x2 = w3[..., :half], w3[..., half:]
    z = jnp.zeros(w3.shape[:-1] + (LANE - B_ROPE,), w3.dtype)
    a = jnp.concatenate([x1, x2, z], axis=-1)
    b = jnp.concatenate([x2, x1, z], axis=-1)
    return a, b


def _layout_w_in(w):
    d = w.shape[0]
    o = 0
    qa = w[:, o:o + A_WIDTH]; o += A_WIDTH
    ka = w[:, o:o + A_WIDTH]; o += A_WIDTH
    va = w[:, o:o + A_WIDTH]; o += A_WIDTH
    cq = w[:, o:o + Q_LORA]; o += Q_LORA
    ckv = w[:, o:o + KV_LORA]; o += KV_LORA
    kr = w[:, o:o + B_ROPE]; o += B_ROPE
    gates = w[:, o:]
    kra, krb = _rope_cols(kr.reshape(d, 1, B_ROPE))
    return jnp.concatenate([gates, qa, ka, va, cq, ckv, kra.reshape(d, LANE),
                            krb.reshape(d, LANE)], axis=1).astype(BF16)


def _layout_w_uq(w):
    k = w.shape[0]
    w3 = w.reshape(k, B_HEADS, B_QK)
    nope = w3[:, :, :B_NOPE].reshape(k, B_HEADS * LANE)
    a, b = _rope_cols(w3[:, :, B_NOPE:])
    return jnp.concatenate([nope, a.reshape(k, -1), b.reshape(k, -1)], axis=1).astype(BF16)


def _layout_w_ukv(w):
    k = w.shape[0]
    w3 = w.reshape(k, B_HEADS, B_NOPE + B_V)
    return jnp.concatenate([w3[:, :, :B_NOPE].reshape(k, -1),
                            w3[:, :, B_NOPE:].reshape(k, -1)], axis=1).astype(BF16)


def kernel(x, c, positions, ln1_g, ln2_g, w_ada, b_ada, w_in, q_norm_g, w_uq, kv_norm_g, w_ukv,
           w_a_up, w_b_up, w_o, w_grp, b_grp, w_exp, b_exp, w_gu, w_down, final_g):
    bsz, seq, d = x.shape
    depth = w_in.shape[0]
    n = bsz * seq
    n_blocks = n * TOP_K // EXPERT_TILE + N_EXPERTS
    n_slots = n_blocks * EXPERT_TILE
    tile_tok = 512

    mod = _ada(c, w_ada, b_ada).reshape(depth, bsz, N_MOD, 1, d)

    pos_col = positions.reshape(n, 1)
    ctab, stab = _rope_tables(pos_col)
    posc = positions.reshape(bsz, seq, 1)
    posr = [positions.reshape(bsz, seq // dil, dil).transpose(0, 2, 1) for _, dil in A_GROUPS]
    slopes = 2.0 ** (-ALIBI_MAX_BIAS * jnp.arange(1, A_HEADS + 1, dtype=F32) / A_HEADS)
    slopes = slopes.reshape(A_NG, A_HPG).T
    slopes = jnp.pad(slopes, ((0, 0), (0, 8 - A_NG)))
    slopes = jnp.broadcast_to(slopes[:, :, None], (A_HPG, 8, LANE))

    x2 = x.reshape(n, d)
    for l in range(depth):
        sh1, sc1, gt1, sh2, sc2, gt2 = (mod[l, :, m] for m in range(N_MOD))

        if l == 0:
            proj = _norm_proj(x2, ln1_g[l].reshape(1, d), sc1, sh1, _layout_w_in(w_in[l]), seq)
        else:
            proj = _proj(h_next, _layout_w_in(w_in[l]))
        q = _mla_q(proj, q_norm_g[l].reshape(1, Q_LORA), _layout_w_uq(w_uq[l]), ctab, stab)
        k, v = _mla_kv(proj, kv_norm_g[l].reshape(1, KV_LORA), _layout_w_ukv(w_ukv[l]), ctab, stab)
        yb = _mla_attn(q, k, v, bsz, seq)
        ya = _dilated_attn(proj.reshape(bsz, seq, PROJ_COLS), posc, posr, slopes).reshape(n, A_OUT)

        w_r = jnp.concatenate([w_grp[l], w_exp[l],
                               jnp.zeros((d, LANE - N_GROUPS - N_EXPERTS), F32)], axis=1)
        w_r_hi = w_r.astype(BF16)
        w_r = jnp.concatenate([w_r_hi, (w_r - w_r_hi.astype(F32)).astype(BF16)], axis=1)
        b_r = jnp.concatenate([b_grp[l], b_exp[l],
                               jnp.zeros((LANE - N_GROUPS - N_EXPERTS,), F32)]).reshape(1, LANE)
        x2, h2, logits = _merge_out(ya, yb, proj, x2, w_a_up[l].astype(BF16),
                                    w_b_up[l].astype(BF16), w_o[l].astype(BF16), gt1,
                                    ln2_g[l].reshape(1, d), sc2, sh2, w_r, b_r, seq)

        rt, rw, cnt = _route(logits)
        counts = cnt[0, N_GROUPS:N_GROUPS + N_EXPERTS].astype(I32)
        padded = (counts + EXPERT_TILE - 1) // EXPERT_TILE * EXPERT_TILE
        pends = jnp.cumsum(padded)
        pstarts = pends - padded
        eids = rt[:TOP_K].astype(I32)
        onehot = eids[:, :, None] == jnp.arange(N_EXPERTS, dtype=I32)
        dest = jnp.sum(jnp.where(onehot, pstarts, 0), axis=-1) + rt[TOP_K:2 * TOP_K].astype(I32)
        dest3 = dest.reshape(TOP_K, n // tile_tok, tile_tok).transpose(1, 0, 2).reshape(
            n // tile_tok, 1, TOP_K * tile_tok)
        n_used = (pends[-1:] // EXPERT_TILE).astype(I32)
        blk_start = jnp.arange(n_blocks, dtype=I32) * EXPERT_TILE
        block_e = jnp.minimum(jnp.sum((pends[None, :] <= blk_start[:, None]).astype(I32), axis=1),
                              N_EXPERTS - 1)

        eidx = jnp.arange(N_EXPERTS, dtype=I32)
        nxt_used = lax.cummin(jnp.where(counts > 0, eidx, N_EXPERTS), reverse=True)
        nxt_after = jnp.concatenate([nxt_used[1:], jnp.full((1,), N_EXPERTS, I32)])
        nxt_after = jnp.where(nxt_after >= N_EXPERTS, -1, nxt_after)
        next_e = jnp.sum(jnp.where(block_e[:, None] == eidx[None, :], nxt_after[None, :], 0),
                         axis=1).astype(I32)
        slot_buf = jnp.zeros((n_slots * PACK_ROWS, LANE), U32) if l == 0 else ys
        xs = _dispatch(dest3, h2, slot_buf)
        ys = _experts(block_e, next_e, n_used, xs, w_gu, w_down, l)
        if l == depth - 1:
            x2 = _combine(dest3, ys, x2, rw, gt2, final_g.reshape(1, d), gt2, gt2, seq, final=True)
        else:
            x2, h_next = _combine(dest3, ys, x2, rw, gt2, ln1_g[l + 1].reshape(1, d),
                                  mod[l + 1, :, 1], mod[l + 1, :, 0], seq, final=False)
    return x2.reshape(bsz, seq, d)
```

```python
import functools

import jax
import jax.numpy as jnp
from jax import lax
from jax.experimental import pallas as pl
from jax.experimental.pallas import tpu as pltpu

F32 = jnp.float32
BF16 = jnp.bfloat16
I32 = jnp.int32
U32 = jnp.uint32
HIGHEST = lax.Precision.HIGHEST

HEAD_DIM = 128
A_GROUPS = ((128, 1), (512, 4), (2048, 16))
A_HPG = 4
A_NG = len(A_GROUPS)
A_HEADS = A_HPG * A_NG
A_WIDTH = A_HEADS * HEAD_DIM
A_OUT = A_HPG * HEAD_DIM
ALIBI_MAX_BIAS = 8.0
B_HEADS = 8
B_NOPE = 128
B_ROPE = 64
B_V = 128
B_QK = B_NOPE + B_ROPE
Q_LORA = 512
KV_LORA = 512
B_OUT = B_HEADS * B_V
ROPE_THETA = 10000.0
N_GROUPS = 8
EPG = 8
N_EXPERTS = N_GROUPS * EPG
TOP_K = 2
D_EXPERT = 512
N_MOD = 6
EPS = 1e-6
NEG = -1e30
LOG2E = 1.4426950408889634

LANE = 128
VMEM_LIMIT = 56 * 1024 * 1024
QBLK = 128
EXPERT_TILE = 256
DISPATCH_TILE = 1024
COMBINE_TILE = 256

COL_GA = 0
COL_GB = 16
COL_QA = 32
COL_KA = 44
COL_VA = 56
COL_CQ = 68
COL_CKV = 72
COL_KR = 76
N_COLBLK = 78
PROJ_COLS = N_COLBLK * LANE


def _cparams(sem):
    return pltpu.CompilerParams(dimension_semantics=sem, vmem_limit_bytes=VMEM_LIMIT)


def _rms(x, g):
    return x * lax.rsqrt(jnp.mean(x * x, axis=-1, keepdims=True) + EPS) * g


def _ada_kernel(c_ref, w_ref, b_ref, o_ref):
    c = c_ref[...]
    cs = c * jax.nn.sigmoid(c)
    o_ref[0] = jnp.dot(cs, w_ref[0], precision=HIGHEST, preferred_element_type=F32) + b_ref[0]


def _ada(c, w_ada, b_ada):
    nl, d, nm = w_ada.shape
    bsz = c.shape[0]
    tn = 1536
    return pl.pallas_call(
        _ada_kernel,
        grid=(nl, nm // tn),
        in_specs=[
            pl.BlockSpec((bsz, d), lambda l, j: (0, 0)),
            pl.BlockSpec((1, d, tn), lambda l, j: (l, 0, j)),
            pl.BlockSpec((1, 1, tn), lambda l, j: (l, 0, j)),
        ],
        out_specs=pl.BlockSpec((1, bsz, tn), lambda l, j: (l, 0, j)),
        out_shape=jax.ShapeDtypeStruct((nl, bsz, nm), F32),
        compiler_params=_cparams(("arbitrary", "arbitrary")),
        name="ada",
    )(c, w_ada, b_ada.reshape(nl, 1, nm))


def _rope_table_kernel(pos_ref, inv_ref, c_ref, s_ref):
    ang = pos_ref[...].astype(F32) * inv_ref[...]
    lane = lax.broadcasted_iota(I32, ang.shape, 1)
    half = B_ROPE // 2
    cs = jnp.cos(ang)
    sn = jnp.sin(ang)
    c_ref[...] = jnp.where(lane < B_ROPE, cs, 0.0)
    s_ref[...] = jnp.where(lane < half, -sn, jnp.where(lane < B_ROPE, sn, 0.0))


def _rope_tables(pos_col):
    n = pos_col.shape[0]
    half = B_ROPE // 2
    inv = ROPE_THETA ** (-jnp.arange(half, dtype=F32) / half)
    inv_row = jnp.concatenate([inv, inv, jnp.zeros((LANE - B_ROPE,), F32)]).reshape(1, LANE)
    tm = 2048
    return pl.pallas_call(
        _rope_table_kernel,
        grid=(n // tm,),
        in_specs=[pl.BlockSpec((tm, 1), lambda i: (i, 0)),
                  pl.BlockSpec((1, LANE), lambda i: (0, 0))],
        out_specs=[pl.BlockSpec((tm, LANE), lambda i: (i, 0)),
                   pl.BlockSpec((tm, LANE), lambda i: (i, 0))],
        out_shape=[jax.ShapeDtypeStruct((n, LANE), F32)] * 2,
        compiler_params=_cparams(("arbitrary",)),
        name="rope_tables",
    )(pos_col, inv_row)


def _norm_proj_kernel(x_ref, g_ref, sc_ref, sh_ref, w_ref, o_ref, h_ref):
    @pl.when(pl.program_id(1) == 0)
    def _():
        h = _rms(x_ref[...], g_ref[...]) * (1.0 + sc_ref[0]) + sh_ref[0]
        h_ref[...] = h.astype(BF16)

    o_ref[...] = jnp.dot(h_ref[...], w_ref[...], preferred_element_type=F32).astype(o_ref.dtype)


def _norm_proj(x2, g, sc, sh, w, seq):
    n, d = x2.shape
    ncol = w.shape[1]
    tm = 1024
    tn = 13 * LANE
    per_b = seq // tm
    return pl.pallas_call(
        _norm_proj_kernel,
        grid=(n // tm, ncol // tn),
        in_specs=[
            pl.BlockSpec((tm, d), lambda i, j: (i, 0)),
            pl.BlockSpec((1, d), lambda i, j: (0, 0)),
            pl.BlockSpec((1, 1, d), lambda i, j: (i // per_b, 0, 0)),
            pl.BlockSpec((1, 1, d), lambda i, j: (i // per_b, 0, 0)),
            pl.BlockSpec((d, tn), lambda i, j: (0, j)),
        ],
        out_specs=pl.BlockSpec((tm, tn), lambda i, j: (i, j)),
        out_shape=jax.ShapeDtypeStruct((n, ncol), BF16),
        scratch_shapes=[pltpu.VMEM((tm, d), BF16)],
        compiler_params=_cparams(("arbitrary", "arbitrary")),
        name="norm_proj",
    )(x2, g, sc, sh, w)


def _proj_kernel(h_ref, w_ref, o_ref):
    o_ref[...] = jnp.dot(h_ref[...], w_ref[...], preferred_element_type=F32).astype(o_ref.dtype)


def _proj(h, w):
    n, d = h.shape
    ncol = w.shape[1]
    tm = 2048
    tn = 13 * LANE
    return pl.pallas_call(
        _proj_kernel,
        grid=(n // tm, ncol // tn),
        in_specs=[pl.BlockSpec((tm, d), lambda i, j: (i, 0)),
                  pl.BlockSpec((d, tn), lambda i, j: (0, j))],
        out_specs=pl.BlockSpec((tm, tn), lambda i, j: (i, j)),
        out_shape=jax.ShapeDtypeStruct((n, ncol), BF16),
        compiler_params=_cparams(("arbitrary", "arbitrary")),
        name="proj",
    )(h, w)


def _mla_q_kernel(cq_ref, g_ref, w_ref, c_ref, s_ref, q_ref):
    h = _rms(cq_ref[...].astype(F32), g_ref[...]).astype(BF16)
    r = jnp.dot(h, w_ref[...], preferred_element_type=F32)
    scale = B_QK ** -0.5 * LOG2E
    c = c_ref[...]
    s = s_ref[...]
    nw = B_HEADS * LANE
    for hd in range(B_HEADS):
        nope = r[:, hd * LANE:(hd + 1) * LANE]
        ra = r[:, nw + hd * LANE: nw + (hd + 1) * LANE]
        rb = r[:, 2 * nw + hd * LANE: 2 * nw + (hd + 1) * LANE]
        q_ref[:, 2 * hd * LANE:(2 * hd + 1) * LANE] = (nope * scale).astype(BF16)
        q_ref[:, (2 * hd + 1) * LANE:(2 * hd + 2) * LANE] = ((ra * c + rb * s) * scale).astype(BF16)


def _mla_q(proj, g, w, ctab, stab):
    n = proj.shape[0]
    tm = 1024
    return pl.pallas_call(
        _mla_q_kernel,
        grid=(n // tm,),
        in_specs=[
            pl.BlockSpec((tm, Q_LORA), lambda i: (i, COL_CQ * LANE // Q_LORA)),
            pl.BlockSpec((1, Q_LORA), lambda i: (0, 0)),
            pl.BlockSpec(w.shape, lambda i: (0, 0)),
            pl.BlockSpec((tm, LANE), lambda i: (i, 0)),
            pl.BlockSpec((tm, LANE), lambda i: (i, 0)),
        ],
        out_specs=pl.BlockSpec((tm, 2 * B_HEADS * LANE), lambda i: (i, 0)),
        out_shape=jax.ShapeDtypeStruct((n, 2 * B_HEADS * LANE), BF16),
        compiler_params=_cparams(("arbitrary",)),
        name="mla_q",
    )(proj, g, w, ctab, stab)


def _mla_kv_kernel(ckv_ref, kr_ref, g_ref, w_ref, c_ref, s_ref, k_ref, v_ref):
    h = _rms(ckv_ref[...].astype(F32), g_ref[...]).astype(BF16)
    r = jnp.dot(h, w_ref[...], preferred_element_type=F32)
    kr = kr_ref[...].astype(F32)
    krope = (kr[:, :LANE] * c_ref[...] + kr[:, LANE:] * s_ref[...]).astype(BF16)
    nw = B_HEADS * LANE
    for hd in range(B_HEADS):
        k_ref[:, 2 * hd * LANE:(2 * hd + 1) * LANE] = r[:, hd * LANE:(hd + 1) * LANE].astype(BF16)
        k_ref[:, (2 * hd + 1) * LANE:(2 * hd + 2) * LANE] = krope
    v_ref[...] = r[:, nw:].astype(BF16)


def _mla_kv(proj, g, w, ctab, stab):
    n = proj.shape[0]
    tm = 1024
    return pl.pallas_call(
        _mla_kv_kernel,
        grid=(n // tm,),
        in_specs=[
            pl.BlockSpec((tm, KV_LORA), lambda i: (i, COL_CKV * LANE // KV_LORA)),
            pl.BlockSpec((tm, 2 * LANE), lambda i: (i, COL_KR // 2)),
            pl.BlockSpec((1, KV_LORA), lambda i: (0, 0)),
            pl.BlockSpec(w.shape, lambda i: (0, 0)),
            pl.BlockSpec((tm, LANE), lambda i: (i, 0)),
            pl.BlockSpec((tm, LANE), lambda i: (i, 0)),
        ],
        out_specs=[pl.BlockSpec((tm, 2 * B_HEADS * LANE), lambda i: (i, 0)),
                   pl.BlockSpec((tm, B_HEADS * LANE), lambda i: (i, 0))],
        out_shape=[jax.ShapeDtypeStruct((n, 2 * B_HEADS * LANE), BF16),
                   jax.ShapeDtypeStruct((n, B_HEADS * LANE), BF16)],
        compiler_params=_cparams(("arbitrary",)),
        name="mla_kv",
    )(proj, proj, g, w, ctab, stab)


MLA_SUB = 512


MLA_HPS = 4


def _mla_attn_kernel(q_ref, k_ref, v_ref, o_ref, vx_ref):
    for hh in range(MLA_HPS):
        vx_ref[hh, :, :LANE] = v_ref[:, hh * LANE:(hh + 1) * LANE]
        vx_ref[hh, :, LANE:] = jnp.ones((vx_ref.shape[1], LANE), BF16)

    for hh in range(MLA_HPS):
        cols = slice(2 * hh * LANE, 2 * (hh + 1) * LANE)
        for j in range(q_ref.shape[0] // MLA_SUB):
            rows = slice(j * MLA_SUB, (j + 1) * MLA_SUB)
            s = lax.dot_general(q_ref[rows, cols], k_ref[:, cols], (((1,), (1,)), ((), ())),
                                preferred_element_type=F32)
            m = jnp.max(s, axis=-1, keepdims=True)
            p = jnp.exp2(s - m).astype(BF16)
            ox = jnp.dot(p, vx_ref[hh], preferred_element_type=F32)
            o_ref[rows, hh * LANE:(hh + 1) * LANE] = (
                ox[:, :LANE] / ox[:, LANE:LANE + 1]).astype(o_ref.dtype)


def _mla_attn(q, k, v, bsz, seq):
    n = q.shape[0]
    qk_w = 2 * LANE * MLA_HPS
    return pl.pallas_call(
        _mla_attn_kernel,
        grid=(bsz, B_HEADS // MLA_HPS),
        in_specs=[
            pl.BlockSpec((seq, qk_w), lambda b, h: (b, h)),
            pl.BlockSpec((seq, qk_w), lambda b, h: (b, h)),
            pl.BlockSpec((seq, LANE * MLA_HPS), lambda b, h: (b, h)),
        ],
        out_specs=pl.BlockSpec((seq, LANE * MLA_HPS), lambda b, h: (b, h)),
        out_shape=jax.ShapeDtypeStruct((n, B_HEADS * LANE), BF16),
        scratch_shapes=[pltpu.VMEM((MLA_HPS, seq, 2 * LANE), BF16)],
        compiler_params=_cparams(("arbitrary", "arbitrary")),
        name="mla_attn",
    )(q, k, v)


def _dilated_kernel(*refs, seq):
    qkv = refs[:3 * A_NG]
    posc_ref = refs[3 * A_NG]
    posr = refs[3 * A_NG + 1: 3 * A_NG + 1 + A_NG]
    sl_ref = refs[3 * A_NG + 1 + A_NG]
    o_ref = refs[3 * A_NG + 2 + A_NG]
    qf, kf, vf, og, lg = refs[3 * A_NG + 3 + A_NG:]
    scale = HEAD_DIM ** -0.5

    for g, (win, dil) in enumerate(A_GROUPS):
        radius = win // (2 * dil)
        length = seq // dil
        kw = min(2 * QBLK, length)
        q_in, k_in, v_in = qkv[3 * g], qkv[3 * g + 1], qkv[3 * g + 2]
        slope2 = sl_ref[0, g:g + 1, 0:1] * LOG2E
        if dil > 1:
            qf[...] = q_in[0].astype(F32)
            kf[...] = k_in[0].astype(F32)
            vf[...] = v_in[0].astype(F32)

        def rows(start, size, dil=dil):
            if dil == 1:
                return pl.ds(start, size)
            return pl.ds(start, size, stride=dil)

        def load(ref_in, ref_f32, rws, dil=dil):
            if dil == 1:
                return ref_in[0, rws, :]
            return ref_f32[rws, :].astype(BF16)

        units = []
        for r in range(dil):
            for ib in range(length // QBLK):
                i0 = ib * QBLK
                ws = min(max(i0 - radius, 0), length - kw)
                units.append((r, i0, ws, rows(r + dil * i0, QBLK), rows(r + dil * ws, kw)))
        qs = [(load(q_in, qf, u[3]).astype(F32) * (scale * LOG2E)).astype(BF16) for u in units]
        ks = [load(k_in, kf, u[4]) for u in units]
        biased = []
        for (r, i0, ws, q_rows, _), q, k in zip(units, qs, ks):
            s = lax.dot_general(q, k, (((1,), (1,)), ((), ())), preferred_element_type=F32)
            pq = posc_ref[0, q_rows, :]
            pk = posr[g][0, r:r + 1, ws:ws + kw]
            dist = jnp.abs(pq - pk).astype(F32)
            ii = i0 + lax.broadcasted_iota(I32, (QBLK, kw), 0)
            jj = ws + lax.broadcasted_iota(I32, (QBLK, kw), 1)
            biased.append(jnp.where(jnp.abs(ii - jj) <= radius, s - slope2 * dist, NEG))
        ms = [jnp.max(s, axis=-1, keepdims=True) for s in biased]
        ps, dens = [], []
        for s, m in zip(biased, ms):
            p = jnp.exp2(s - m)
            dens.append(jnp.sum(p, axis=-1, keepdims=True))
            ps.append(p.astype(BF16))
        vs = [load(v_in, vf, u[4]) for u in units]
        outs = [jnp.dot(p, v, preferred_element_type=F32) / den
                for p, v, den in zip(ps, vs, dens)]
        for u, o, m, den in zip(units, outs, ms, dens):
            og[g, u[3], :] = o
            lg[g, u[3], :] = jnp.broadcast_to(m * (1.0 / LOG2E) + jnp.log(den), (QBLK, HEAD_DIM))

    chunk = 256

    def merge(ci, carry):
        rs = pl.ds(pl.multiple_of(ci * chunk, chunk), chunk)
        l0, l1, l2 = lg[0, rs, :], lg[1, rs, :], lg[2, rs, :]
        m = jnp.maximum(jnp.maximum(l0, l1), l2)
        w0, w1, w2 = jnp.exp(l0 - m), jnp.exp(l1 - m), jnp.exp(l2 - m)
        o = (w0 * og[0, rs, :] + w1 * og[1, rs, :] + w2 * og[2, rs, :]) / (w0 + w1 + w2)
        o_ref[0, rs, :] = o.astype(o_ref.dtype)
        return carry

    lax.fori_loop(0, seq // chunk, merge, 0)


def _dilated_attn(proj3, posc, posr, slopes):
    bsz, seq, _ = proj3.shape
    in_specs = []
    args = []
    for g in range(A_NG):
        for base in (COL_QA, COL_KA, COL_VA):
            in_specs.append(pl.BlockSpec(
                (1, seq, LANE), lambda b, h, base=base, g=g: (b, 0, base + A_HPG * g + h)))
            args.append(proj3)
    in_specs.append(pl.BlockSpec((1, seq, 1), lambda b, h: (b, 0, 0)))
    args.append(posc)
    for g, (_, dil) in enumerate(A_GROUPS):
        in_specs.append(pl.BlockSpec((1, dil, seq // dil), lambda b, h: (b, 0, 0)))
        args.append(posr[g])
    in_specs.append(pl.BlockSpec((1, 8, LANE), lambda b, h: (h, 0, 0)))
    args.append(slopes)
    return pl.pallas_call(
        functools.partial(_dilated_kernel, seq=seq),
        grid=(bsz, A_HPG),
        in_specs=in_specs,
        out_specs=pl.BlockSpec((1, seq, LANE), lambda b, h: (b, 0, h)),
        out_shape=jax.ShapeDtypeStruct((bsz, seq, A_OUT), BF16),
        scratch_shapes=[pltpu.VMEM((seq, HEAD_DIM), F32)] * 3
        + [pltpu.VMEM((A_NG, seq, HEAD_DIM), F32)] * 2,
        compiler_params=_cparams(("arbitrary", "arbitrary")),
        name="dilated_attn",
    )(*args)


def _merge_out_kernel(ya_ref, yb_ref, ga_ref, gb_ref, x_ref, wa_ref, wb_ref, wo_ref, gt_ref,
                      g2_ref, sc_ref, sh_ref, wr_ref, br_ref, xo_ref, h_ref, lg_ref):
    a = jnp.dot(ya_ref[...], wa_ref[...], preferred_element_type=F32)
    b = jnp.dot(yb_ref[...], wb_ref[...], preferred_element_type=F32)
    ga = jax.nn.sigmoid(ga_ref[...].astype(F32))
    gb = jax.nn.sigmoid(gb_ref[...].astype(F32))
    merged = (ga * a + gb * b).astype(BF16)
    xn = x_ref[...] + (1.0 + gt_ref[0]) * jnp.dot(merged, wo_ref[...], preferred_element_type=F32)
    xo_ref[...] = xn
    h = _rms(xn, g2_ref[...]) * (1.0 + sc_ref[0]) + sh_ref[0]
    _pack_store(h, h_ref)
    h_hi = h.astype(BF16)
    h_lo = (h - h_hi.astype(F32)).astype(BF16)
    t = jnp.dot(h_hi, wr_ref[...], preferred_element_type=F32)
    u = jnp.dot(h_lo, wr_ref[:, :LANE], preferred_element_type=F32)
    lg_ref[...] = t[:, :LANE] + t[:, LANE:] + u + br_ref[...]


def _merge_out(ya, yb, proj, x2, wa, wb, wo, gt, g2, sc, sh, wr, br, seq):
    n, d = x2.shape
    tm = 512
    per_b = seq // tm
    const = lambda i: (0, 0)
    bidx = lambda i: (i // per_b, 0, 0)
    once = pl.Buffered(1)
    return pl.pallas_call(
        _merge_out_kernel,
        grid=(n // tm,),
        in_specs=[
            pl.BlockSpec((tm, A_OUT), lambda i: (i, 0)),
            pl.BlockSpec((tm, B_OUT), lambda i: (i, 0)),
            pl.BlockSpec((tm, d), lambda i: (i, COL_GA * LANE // d)),
            pl.BlockSpec((tm, d), lambda i: (i, COL_GB * LANE // d)),
            pl.BlockSpec((tm, d), lambda i: (i, 0)),
            pl.BlockSpec(wa.shape, const, pipeline_mode=once),
            pl.BlockSpec(wb.shape, const, pipeline_mode=once),
            pl.BlockSpec(wo.shape, const, pipeline_mode=once),
            pl.BlockSpec((1, 1, d), bidx),
            pl.BlockSpec((1, d), const),
            pl.BlockSpec((1, 1, d), bidx),
            pl.BlockSpec((1, 1, d), bidx),
            pl.BlockSpec(wr.shape, const),
            pl.BlockSpec((1, LANE), const),
        ],
        out_specs=[pl.BlockSpec((tm, d), lambda i: (i, 0)),
                   pl.BlockSpec((tm * PACK_ROWS, LANE), lambda i: (i, 0)),
                   pl.BlockSpec((tm, LANE), lambda i: (i, 0))],
        out_shape=[jax.ShapeDtypeStruct((n, d), F32),
                   jax.ShapeDtypeStruct((n * PACK_ROWS, LANE), U32),
                   jax.ShapeDtypeStruct((n, LANE), F32)],
        compiler_params=_cparams(("arbitrary",)),
        name="merge_out",
    )(ya, yb, proj, proj, x2, wa, wb, wo, gt, g2, sc, sh, wr, br)


def _route_kernel(lg_ref, rt_ref, rw_ref, cnt_ref, carry_ref):
    @pl.when(pl.program_id(0) == 0)
    def _():
        carry_ref[...] = jnp.zeros_like(carry_ref)

    lg = lg_ref[...]
    tm = lg.shape[0]
    lane = lax.broadcasted_iota(I32, lg.shape, 1).astype(F32)
    big = float(2 * LANE)
    is_grp = lane < N_GROUPS
    gl = jnp.where(is_grp, lg, NEG)
    gm = jnp.max(gl, axis=-1, keepdims=True)
    ge = jnp.where(is_grp, jnp.exp(gl - gm), 0.0)
    pg = ge / jnp.sum(ge, axis=-1, keepdims=True)
    p_top = jnp.max(pg, axis=-1, keepdims=True)
    g_idx = jnp.min(jnp.where(is_grp & (pg == p_top), lane, big), axis=-1, keepdims=True)

    lo = N_GROUPS + EPG * g_idx
    in_grp = (lane >= lo) & (lane < lo + EPG)
    el = jnp.where(in_grp, lg, NEG)
    em = jnp.max(el, axis=-1, keepdims=True)
    ee = jnp.where(in_grp, jnp.exp(el - em), 0.0)
    pin = ee / jnp.sum(ee, axis=-1, keepdims=True)
    v1 = jnp.max(pin, axis=-1, keepdims=True)
    i1 = jnp.min(jnp.where(in_grp & (pin == v1), lane, big), axis=-1, keepdims=True)
    rest = in_grp & (lane != i1)
    p2 = jnp.where(rest, pin, -1.0)
    v2 = jnp.max(p2, axis=-1, keepdims=True)
    i2 = jnp.min(jnp.where(rest & (p2 == v2), lane, big), axis=-1, keepdims=True)
    vs = v1 + v2
    w1 = p_top * v1 / vs
    w2 = p_top * v2 / vs

    oh1 = lane == i1
    oh2 = lane == i2
    mm = (oh1 | oh2).astype(BF16)
    tri = (lax.broadcasted_iota(I32, (tm, tm), 0) > lax.broadcasted_iota(I32, (tm, tm), 1))
    cnt = jnp.dot(tri.astype(BF16), mm, preferred_element_type=F32) + carry_ref[...]
    r1 = jnp.sum(jnp.where(oh1, cnt, 0.0), axis=-1, keepdims=True)
    r2 = jnp.sum(jnp.where(oh2, cnt, 0.0), axis=-1, keepdims=True)
    carry_ref[...] += jnp.sum(mm.astype(F32), axis=0, keepdims=True)
    cnt_ref[...] = carry_ref[...]

    e1 = i1 - N_GROUPS
    e2 = i2 - N_GROUPS
    packed = jnp.where(lane == 0, e1, jnp.where(lane == 1, e2, jnp.where(
        lane == 2, r1, jnp.where(lane == 3, r2, jnp.where(lane == 4, w1, w2)))))
    rt_ref[...] = packed.T[:8, :]
    rw_ref[...] = jnp.where(lane == 0, w1, w2)


def _route(logits):
    n = logits.shape[0]
    tm = 1024
    return pl.pallas_call(
        _route_kernel,
        grid=(n // tm,),
        in_specs=[pl.BlockSpec((tm, LANE), lambda i: (i, 0))],
        out_specs=[pl.BlockSpec((8, tm), lambda i: (0, i)),
                   pl.BlockSpec((tm, LANE), lambda i: (i, 0)),
                   pl.BlockSpec((1, LANE), lambda i: (0, 0))],
        out_shape=[jax.ShapeDtypeStruct((8, n), F32),
                   jax.ShapeDtypeStruct((n, LANE), F32),
                   jax.ShapeDtypeStruct((1, LANE), F32)],
        scratch_shapes=[pltpu.VMEM((1, LANE), F32)],
        compiler_params=_cparams(("arbitrary",)),
        name="route",
    )(logits)


PACK_ROWS = 8
HI_MASK = 0xFFFF0000
WEIGHT_SLABS = 4


def _pack_store(val, ref):
    rows = val.shape[0]
    for c in range(PACK_ROWS):
        lo = val[:, 2 * c * LANE:(2 * c + 1) * LANE].astype(BF16).astype(F32)
        hi = val[:, (2 * c + 1) * LANE:(2 * c + 2) * LANE].astype(BF16).astype(F32)
        word = (pltpu.bitcast(hi, U32) & jnp.uint32(HI_MASK)) | (pltpu.bitcast(lo, U32) >> 16)
        ref[pl.ds(c, rows, stride=PACK_ROWS), :] = word


def _unpack_load(load_chunk):
    pieces = []
    for c in range(PACK_ROWS):
        w = load_chunk(c)
        pieces.append(pltpu.bitcast(w << 16, F32))
        pieces.append(pltpu.bitcast(w & jnp.uint32(HI_MASK), F32))
    return pieces


def _start_rows(src_ref, dst_ref, idx_ref, base, count, sem):
    for j in range(count):
        src_row = pl.multiple_of(idx_ref[0, 0, base + j] * PACK_ROWS, PACK_ROWS)
        pltpu.make_async_copy(src_ref.at[pl.ds(src_row, PACK_ROWS)],
                              dst_ref.at[pl.ds(j * PACK_ROWS, PACK_ROWS)], sem).start(priority=j % 2)


def _wait_rows(src_ref, dst_ref, sem):
    pltpu.make_async_copy(src_ref.at[pl.ds(0, dst_ref.shape[0])], dst_ref, sem).wait()


def _dispatch_kernel(dest_ref, h_ref, xs_in_ref, xs_ref, sem):
    del xs_in_ref
    tm = h_ref.shape[0] // PACK_ROWS
    for k in range(TOP_K):
        for j in range(tm):
            dst_row = pl.multiple_of(dest_ref[0, 0, k * tm + j] * PACK_ROWS, PACK_ROWS)
            pltpu.make_async_copy(h_ref.at[pl.ds(j * PACK_ROWS, PACK_ROWS)],
                                  xs_ref.at[pl.ds(dst_row, PACK_ROWS)],
                                  sem.at[k]).start(priority=j % 2)
    for k in range(TOP_K):
        pltpu.make_async_copy(h_ref, xs_ref.at[pl.ds(0, tm * PACK_ROWS)], sem.at[k]).wait()


def _dispatch(dest3, h_packed, xs0):
    tm = dest3.shape[2] // TOP_K
    nt = dest3.shape[0]
    return pl.pallas_call(
        _dispatch_kernel,
        grid=(nt,),
        in_specs=[pl.BlockSpec((1, 1, TOP_K * tm), lambda i: (i, 0, 0), memory_space=pltpu.SMEM),
                  pl.BlockSpec((tm * PACK_ROWS, LANE), lambda i: (i, 0)),
                  pl.BlockSpec(memory_space=pl.ANY)],
        out_specs=pl.BlockSpec(memory_space=pl.ANY),
        out_shape=jax.ShapeDtypeStruct(xs0.shape, xs0.dtype),
        scratch_shapes=[pltpu.SemaphoreType.DMA((TOP_K,))],
        input_output_aliases={2: 0},
        compiler_params=_cparams(("arbitrary",)),
        name="dispatch",
    )(dest3, h_packed, xs0)


def _expert_kernel(be_ref, nx_ref, nu_ref, x_ref, wgu_ref, wd_ref, o_ref,
                   wgu_f, wd_f, wgu_s, wd_s, wsem, *, layer):
    i = pl.program_id(0)
    n_used = nu_ref[0]
    tb = x_ref.shape[0] // PACK_ROWS

    def weight_copies(e):
        cps = []
        for src, dst, s in ((wgu_ref, wgu_f, wsem.at[0]), (wd_ref, wd_f, wsem.at[1])):
            rows = dst.shape[0] // WEIGHT_SLABS
            for c in range(WEIGHT_SLABS):
                cps.append(pltpu.make_async_copy(src.at[layer, e, pl.ds(c * rows, rows)],
                                                 dst.at[pl.ds(c * rows, rows)], s))
        return cps

    @pl.when(i == 0)
    def _():
        for cp in weight_copies(be_ref[0]):
            cp.start()

    @pl.when(i < n_used)
    def _():
        e = be_ref[i]
        prev = be_ref[jnp.maximum(i - 1, 0)]

        @pl.when((i == 0) | (e != prev))
        def _():
            for cp in weight_copies(e):
                cp.wait()
            wgu_s[...] = wgu_f[...].astype(BF16)
            wd_s[...] = wd_f[...].astype(BF16)

            @pl.when(nx_ref[i] >= 0)
            def _():
                for cp in weight_copies(nx_ref[i]):
                    cp.start()

        pieces = _unpack_load(lambda c: x_ref[pl.ds(c, tb, stride=PACK_ROWS), :])
        x = jnp.concatenate([p.astype(BF16) for p in pieces], axis=-1)
        gu = jnp.dot(x, wgu_s[...], preferred_element_type=F32)
        gate = gu[:, :D_EXPERT]
        up = gu[:, D_EXPERT:]
        act = (gate * jax.nn.sigmoid(gate) * up).astype(BF16)
        _pack_store(jnp.dot(act, wd_s[...], preferred_element_type=F32), o_ref)

    @pl.when(i >= n_used)
    def _():
        o_ref[...] = jnp.zeros_like(o_ref)


def _experts(block_e, next_e, n_used, xs, w_gu, w_down, layer):
    d = w_gu.shape[2]
    tile_rows = EXPERT_TILE * PACK_ROWS
    nb = xs.shape[0] // tile_rows

    grid_spec = pltpu.PrefetchScalarGridSpec(
        num_scalar_prefetch=3,
        grid=(nb,),
        in_specs=[
            pl.BlockSpec((tile_rows, LANE),
                         lambda i, be, nx, nu: (jnp.minimum(i, nu[0] - 1), 0)),
            pl.BlockSpec(memory_space=pl.ANY),
            pl.BlockSpec(memory_space=pl.ANY),
        ],
        out_specs=pl.BlockSpec((tile_rows, LANE), lambda i, be, nx, nu: (i, 0)),
        scratch_shapes=[pltpu.VMEM((d, 2 * D_EXPERT), F32), pltpu.VMEM((D_EXPERT, d), F32),
                        pltpu.VMEM((d, 2 * D_EXPERT), BF16), pltpu.VMEM((D_EXPERT, d), BF16),
                        pltpu.SemaphoreType.DMA((2,))],
    )
    return pl.pallas_call(
        functools.partial(_expert_kernel, layer=layer),
        grid_spec=grid_spec,
        out_shape=jax.ShapeDtypeStruct(xs.shape, U32),
        compiler_params=_cparams(("arbitrary",)),
        name="experts",
    )(block_e, next_e, n_used, xs, w_gu, w_down)


def _combine_kernel(dest0_ref, destn_ref, ys_ref, x_ref, rw_ref, gt_ref, g_ref, sc_ref, sh_ref,
                    *rest, final):
    if final:
        o_ref, ybuf, sem = rest
    else:
        o_ref, hn_ref, ybuf, sem = rest
    i = pl.program_id(0)
    tm = x_ref.shape[0]

    def start(dest_ref, slot):
        for k in range(TOP_K):
            _start_rows(ys_ref, ybuf.at[slot, k], dest_ref, k * tm, tm, sem.at[slot, k])

    @pl.when(i == 0)
    def _():
        start(dest0_ref, 0)

    @pl.when(i + 1 < pl.num_programs(0))
    def _():
        start(destn_ref, (i + 1) % 2)

    cur = i % 2
    for k in range(TOP_K):
        _wait_rows(ys_ref, ybuf.at[cur, k], sem.at[cur, k])
    rw = rw_ref[...]
    y0 = _unpack_load(lambda c: ybuf[cur, 0, pl.ds(c, tm, stride=PACK_ROWS), :])
    y1 = _unpack_load(lambda c: ybuf[cur, 1, pl.ds(c, tm, stride=PACK_ROWS), :])
    y = jnp.concatenate([rw[:, 0:1] * a + rw[:, 1:2] * b for a, b in zip(y0, y1)], axis=-1)
    xn = x_ref[...] + (1.0 + gt_ref[0]) * y
    if final:
        o_ref[...] = _rms(xn, g_ref[...])
    else:
        o_ref[...] = xn
        hn_ref[...] = (_rms(xn, g_ref[...]) * (1.0 + sc_ref[0]) + sh_ref[0]).astype(BF16)


def _combine(dest3, ys, x2, rw, gt, g, sc, sh, seq, final):
    n, d = x2.shape
    tm = dest3.shape[2] // TOP_K
    per_b = seq // tm
    nt = n // tm
    bidx = lambda i: (i // per_b, 0, 0)
    row = pl.BlockSpec((tm, d), lambda i: (i, 0))
    out_specs = row if final else [row, row]
    out_shape = jax.ShapeDtypeStruct((n, d), F32)
    if not final:
        out_shape = [out_shape, jax.ShapeDtypeStruct((n, d), BF16)]
    return pl.pallas_call(
        functools.partial(_combine_kernel, final=final),
        grid=(nt,),
        in_specs=[pl.BlockSpec((1, 1, TOP_K * tm), lambda i: (0, 0, 0), memory_space=pltpu.SMEM),
                  pl.BlockSpec((1, 1, TOP_K * tm), lambda i: (jnp.minimum(i + 1, nt - 1), 0, 0),
                               memory_space=pltpu.SMEM),
                  pl.BlockSpec(memory_space=pl.ANY),
                  row,
                  pl.BlockSpec((tm, LANE), lambda i: (i, 0)),
                  pl.BlockSpec((1, 1, d), bidx),
                  pl.BlockSpec((1, d), lambda i: (0, 0)),
                  pl.BlockSpec((1, 1, d), bidx),
                  pl.BlockSpec((1, 1, d), bidx)],
        out_specs=out_specs,
        out_shape=out_shape,
        scratch_shapes=[pltpu.VMEM((2, TOP_K, tm * PACK_ROWS, LANE), U32),
                        pltpu.SemaphoreType.DMA((2, TOP_K))],
        compiler_params=_cparams(("arbitrary",)),
        name="combine",
    )(dest3, dest3, ys, x2, rw, gt, g, sc, sh)


def _rope_cols(w3):
    half = B_ROPE // 2
    x1, x2 = w3[..., :half], w3[..., half:]
    z = jnp.zeros(w3.shape[:-1] + (LANE - B_ROPE,), w3.dtype)
    a = jnp.concatenate([x1, x2, z], axis=-1)
    b = jnp.concatenate([x2, x1, z], axis=-1)
    return a, b


def _layout_w_in(w):
    d = w.shape[0]
    o = 0
    qa = w[:, o:o + A_WIDTH]; o += A_WIDTH
    ka = w[:, o:o + A_WIDTH]; o += A_WIDTH
    va = w[:, o:o + A_WIDTH]; o += A_WIDTH
    cq = w[:, o:o + Q_LORA]; o += Q_LORA
    ckv = w[:, o:o + KV_LORA]; o += KV_LORA
    kr = w[:, o:o + B_ROPE]; o += B_ROPE
    gates = w[:, o:]
    kra, krb = _rope_cols(kr.reshape(d, 1, B_ROPE))
    return jnp.concatenate([gates, qa, ka, va, cq, ckv, kra.reshape(d, LANE),
                            krb.reshape(d, LANE)], axis=1).astype(BF16)


def _layout_w_uq(w):
    k = w.shape[0]
    w3 = w.reshape(k, B_HEADS, B_QK)
    nope = w3[:, :, :B_NOPE].reshape(k, B_HEADS * LANE)
    a, b = _rope_cols(w3[:, :, B_NOPE:])
    return jnp.concatenate([nope, a.reshape(k, -1), b.reshape(k, -1)], axis=1).astype(BF16)


def _layout_w_ukv(w):
    k = w.shape[0]
    w3 = w.reshape(k, B_HEADS, B_NOPE + B_V)
    return jnp.concatenate([w3[:, :, :B_NOPE].reshape(k, -1),
                            w3[:, :, B_NOPE:].reshape(k, -1)], axis=1).astype(BF16)


def kernel(x, c, positions, ln1_g, ln2_g, w_ada, b_ada, w_in, q_norm_g, w_uq, kv_norm_g, w_ukv,
           w_a_up, w_b_up, w_o, w_grp, b_grp, w_exp, b_exp, w_gu, w_down, final_g):
    bsz, seq, d = x.shape
    depth = w_in.shape[0]
    n = bsz * seq
    n_blocks = n * TOP_K // EXPERT_TILE + N_EXPERTS
    n_slots = n_blocks * EXPERT_TILE

    mod = _ada(c, w_ada, b_ada).reshape(depth, bsz, N_MOD, 1, d)

    pos_col = positions.reshape(n, 1)
    ctab, stab = _rope_tables(pos_col)
    posc = positions.reshape(bsz, seq, 1)
    posr = [positions.reshape(bsz, seq // dil, dil).transpose(0, 2, 1) for _, dil in A_GROUPS]
    slopes = 2.0 ** (-ALIBI_MAX_BIAS * jnp.arange(1, A_HEADS + 1, dtype=F32) / A_HEADS)
    slopes = slopes.reshape(A_NG, A_HPG).T
    slopes = jnp.pad(slopes, ((0, 0), (0, 8 - A_NG)))
    slopes = jnp.broadcast_to(slopes[:, :, None], (A_HPG, 8, LANE))

    x2 = x.reshape(n, d)
    for l in range(depth):
        sh1, sc1, gt1, sh2, sc2, gt2 = (mod[l, :, m] for m in range(N_MOD))

        if l == 0:
            proj = _norm_proj(x2, ln1_g[l].reshape(1, d), sc1, sh1, _layout_w_in(w_in[l]), seq)
        else:
            proj = _proj(h_next, _layout_w_in(w_in[l]))
        q = _mla_q(proj, q_norm_g[l].reshape(1, Q_LORA), _layout_w_uq(w_uq[l]), ctab, stab)
        k, v = _mla_kv(proj, kv_norm_g[l].reshape(1, KV_LORA), _layout_w_ukv(w_ukv[l]), ctab, stab)
        yb = _mla_attn(q, k, v, bsz, seq)
        ya = _dilated_attn(proj.reshape(bsz, seq, PROJ_COLS), posc, posr, slopes).reshape(n, A_OUT)

        w_r = jnp.concatenate([w_grp[l], w_exp[l],
                               jnp.zeros((d, LANE - N_GROUPS - N_EXPERTS), F32)], axis=1)
        w_r_hi = w_r.astype(BF16)
        w_r = jnp.concatenate([w_r_hi, (w_r - w_r_hi.astype(F32)).astype(BF16)], axis=1)
        b_r = jnp.concatenate([b_grp[l], b_exp[l],
                               jnp.zeros((LANE - N_GROUPS - N_EXPERTS,), F32)]).reshape(1, LANE)
        x2, h2, logits = _merge_out(ya, yb, proj, x2, w_a_up[l].astype(BF16),
                                    w_b_up[l].astype(BF16), w_o[l].astype(BF16), gt1,
                                    ln2_g[l].reshape(1, d), sc2, sh2, w_r, b_r, seq)

        rt, rw, cnt = _route(logits)
        counts = cnt[0, N_GROUPS:N_GROUPS + N_EXPERTS].astype(I32)
        padded = (counts + EXPERT_TILE - 1) // EXPERT_TILE * EXPERT_TILE
        pends = jnp.cumsum(padded)
        pstarts = pends - padded
        eids = rt[:TOP_K].astype(I32)
        onehot = eids[:, :, None] == jnp.arange(N_EXPERTS, dtype=I32)
        dest = jnp.sum(jnp.where(onehot, pstarts, 0), axis=-1) + rt[TOP_K:2 * TOP_K].astype(I32)

        def tiled(tile):
            return dest.reshape(TOP_K, n // tile, tile).transpose(1, 0, 2).reshape(
                n // tile, 1, TOP_K * tile)

        n_used = (pends[-1:] // EXPERT_TILE).astype(I32)
        blk_start = jnp.arange(n_blocks, dtype=I32) * EXPERT_TILE
        block_e = jnp.minimum(jnp.sum((pends[None, :] <= blk_start[:, None]).astype(I32), axis=1),
                              N_EXPERTS - 1)

        eidx = jnp.arange(N_EXPERTS, dtype=I32)
        nxt_used = lax.cummin(jnp.where(counts > 0, eidx, N_EXPERTS), reverse=True)
        nxt_after = jnp.concatenate([nxt_used[1:], jnp.full((1,), N_EXPERTS, I32)])
        nxt_after = jnp.where(nxt_after >= N_EXPERTS, -1, nxt_after)
        next_e = jnp.sum(jnp.where(block_e[:, None] == eidx[None, :], nxt_after[None, :], 0),
                         axis=1).astype(I32)
        slot_buf = jnp.zeros((n_slots * PACK_ROWS, LANE), U32) if l == 0 else ys
        xs = _dispatch(tiled(DISPATCH_TILE), h2, slot_buf)
        ys = _experts(block_e, next_e, n_used, xs, w_gu, w_down, l)
        dest3 = tiled(COMBINE_TILE)
        if l == depth - 1:
            x2 = _combine(dest3, ys, x2, rw, gt2, final_g.reshape(1, d), gt2, gt2, seq, final=True)
        else:
            x2, h_next = _combine(dest3, ys, x2, rw, gt2, ln1_g[l + 1].reshape(1, d),
                                  mod[l + 1, :, 1], mod[l + 1, :, 0], seq, final=False)
    return x2.reshape(bsz, seq, d)
```
